```python
import jax, jax.numpy as jnp
from jax import lax
import numpy as np

D_MODEL = 1024
BATCH = 2
SEQ = 8192
DEPTH = 1

N_HEADS_A = 8
N_KV_A = 2
HEAD_DIM_A = 64
WINDOW = 128
BLOCK = 128
N_HEADS_B = 8
QK_NOPE = 64
QK_ROPE = 32
V_DIM_B = 64
Q_LORA = 256
KV_LORA = 128
ROPE_THETA = 10000.0
D_FF = 4 * D_MODEL
EPS = 1e-6

WIDTH_A = N_HEADS_A * HEAD_DIM_A
WIDTH_B = N_HEADS_B * V_DIM_B
KV_WIDTH_A = N_KV_A * HEAD_DIM_A
Q_HEAD_B = QK_NOPE + QK_ROPE
KV_HEAD_B = QK_NOPE + V_DIM_B
SPLITS = (D_MODEL, D_MODEL, WIDTH_A, KV_WIDTH_A, KV_WIDTH_A, Q_LORA, KV_LORA, QK_ROPE)
D_IN = int(sum(SPLITS))
SPLIT_IDX = tuple(int(i) for i in np.cumsum(SPLITS)[:-1])

kernel_name = "hybrid_swa_sink_alibi_mla_gated_sqrelu"


def rmsnorm(x, g):
    x32 = x.astype(jnp.float32)
    y = x32 * lax.rsqrt(jnp.mean(x32 * x32, axis=-1, keepdims=True) + EPS)
    return y.astype(x.dtype) * g


def alibi_slopes(n):
    return 2.0 ** (-8.0 * jnp.arange(1, n + 1, dtype=jnp.float32) / n)


def rope(x, pos):
    d = x.shape[-1]
    freqs = ROPE_THETA ** (-jnp.arange(0, d, 2, dtype=jnp.float32) / d)
    ang = pos.astype(jnp.float32)[..., None] * freqs
    cos, sin = jnp.cos(ang)[:, :, None, :], jnp.sin(ang)[:, :, None, :]
    x32 = x.astype(jnp.float32)
    x1, x2 = x32[..., : d // 2], x32[..., d // 2:]
    return jnp.concatenate([x1 * cos - x2 * sin, x2 * cos + x1 * sin], axis=-1).astype(x.dtype)


def swa_sink_alibi_attention(q, k, v, pos, sinks):
    B, S = q.shape[0], q.shape[1]
    nb = S // BLOCK
    G = N_HEADS_A // N_KV_A
    qb = q.reshape(B, nb, BLOCK, N_KV_A, G, HEAD_DIM_A)

    def band(t):
        padded = jnp.pad(t, [(0, 0), (BLOCK, 0)] + [(0, 0)] * (t.ndim - 2))
        prev = padded[:, :S].reshape((B, nb, BLOCK) + t.shape[2:])
        cur = t.reshape((B, nb, BLOCK) + t.shape[2:])
        return jnp.concatenate([prev, cur], axis=2)

    kb, vb, pb = band(k), band(v), band(pos)
    qpos = pos.reshape(B, nb, BLOCK)
    scale = HEAD_DIM_A ** -0.5
    s = jnp.einsum('bnqkgd,bnskd->bnkgqs', qb, kb).astype(jnp.float32) * scale
    dist = jnp.abs(qpos[:, :, :, None] - pb[:, :, None, :]).astype(jnp.float32)
    slopes = alibi_slopes(N_HEADS_A).reshape(N_KV_A, G)
    s = s - slopes[None, None, :, :, None, None] * dist[:, :, None, None]
    qi = jnp.arange(BLOCK)[:, None] + BLOCK
    si = jnp.arange(2 * BLOCK)[None, :]
    diff = qi - si
    valid = (diff >= 0) & (diff < WINDOW)
    not_pad = (jnp.arange(nb)[:, None, None] > 0) | (si[None] >= BLOCK)
    mask = valid[None] & not_pad
    s = jnp.where(mask[None, :, None, None], s, -jnp.inf)
    sink = sinks.astype(jnp.float32).reshape(1, 1, N_KV_A, G, 1, 1)
    m = jnp.maximum(jnp.max(s, axis=-1, keepdims=True), sink)
    e = jnp.exp(s - m)
    p = e / (jnp.sum(e, axis=-1, keepdims=True) + jnp.exp(sink - m))
    o = jnp.einsum('bnkgqs,bnskd->bnqkgd', p.astype(v.dtype), vb)
    return o.reshape(B, S, WIDTH_A)


def mla_attention(q_nope, q_rope, k_nope, k_rope, v):
    B, S = q_nope.shape[0], q_nope.shape[1]
    nb = S // BLOCK
    scale = Q_HEAD_B ** -0.5
    qn = q_nope.reshape(B, nb, BLOCK, N_HEADS_B, QK_NOPE).transpose(1, 0, 2, 3, 4)
    qr = q_rope.reshape(B, nb, BLOCK, N_HEADS_B, QK_ROPE).transpose(1, 0, 2, 3, 4)
    kidx = jnp.arange(S)

    def one_block(args):
        qn_b, qr_b, i = args
        s = (jnp.einsum('bqhd,bshd->bhqs', qn_b, k_nope)
             + jnp.einsum('bqhd,bsd->bhqs', qr_b, k_rope)).astype(jnp.float32) * scale
        qidx = i * BLOCK + jnp.arange(BLOCK)
        s = jnp.where(kidx[None, :] <= qidx[:, None], s, -jnp.inf)
        p = jax.nn.softmax(s, axis=-1)
        return jnp.einsum('bhqs,bshd->bqhd', p.astype(v.dtype), v)

    o = lax.map(one_block, (qn, qr, jnp.arange(nb)))
    return o.transpose(1, 0, 2, 3, 4).reshape(B, S, WIDTH_B)


def setup_inputs(seed: int = 0) -> dict:
    key = jax.random.key(seed)
    ks = jax.random.split(key, 20)

    def w(k, shape, fan_in):
        return jax.random.normal(k, shape, jnp.float32) * fan_in ** -0.5

    def gain(k, n):
        return 1.0 + 0.02 * jax.random.normal(k, (DEPTH, n), jnp.float32)

    x = jax.random.normal(ks[0], (BATCH, SEQ, D_MODEL), jnp.float32)
    offset = jax.random.randint(ks[1], (BATCH, 1), 0, 1024, dtype=jnp.int32)
    positions = (offset + jnp.arange(SEQ, dtype=jnp.int32)[None, :]).astype(jnp.int32)
    return {
        "x": x,
        "positions": positions,
        "pre_norm_mix": gain(ks[2], D_MODEL),
        "w_in": w(ks[3], (DEPTH, D_MODEL, D_IN), D_MODEL),
        "q_a_norm": gain(ks[4], Q_LORA),
        "w_q_b": w(ks[5], (DEPTH, Q_LORA, N_HEADS_B * Q_HEAD_B), Q_LORA),
        "kv_a_norm": gain(ks[6], KV_LORA),
        "w_kv_b": w(ks[7], (DEPTH, KV_LORA, N_HEADS_B * KV_HEAD_B), KV_LORA),
        "sinks": jax.random.normal(ks[8], (DEPTH, N_HEADS_A), jnp.float32),
        "w_o_a": w(ks[9], (DEPTH, WIDTH_A, D_MODEL), WIDTH_A),
        "w_o_b": w(ks[10], (DEPTH, WIDTH_B, D_MODEL), WIDTH_B),
        "w_out": w(ks[11], (DEPTH, D_MODEL, D_MODEL), D_MODEL),
        "post_norm_mix": gain(ks[12], D_MODEL),
        "pre_norm_mlp": gain(ks[13], D_MODEL),
        "w_up": w(ks[14], (DEPTH, D_MODEL, D_FF), D_MODEL),
        "w_down": w(ks[15], (DEPTH, D_FF, D_MODEL), D_FF),
        "post_norm_mlp": gain(ks[16], D_MODEL),
    }


def reference(x, positions, pre_norm_mix, w_in, q_a_norm, w_q_b, kv_a_norm, w_kv_b, sinks,
              w_o_a, w_o_b, w_out, post_norm_mix, pre_norm_mlp, w_up, w_down, post_norm_mlp):
    B, S = x.shape[0], x.shape[1]
    for l in range(DEPTH):
        h = rmsnorm(x, pre_norm_mix[l])
        proj = h @ w_in[l]
        g_a, g_b, qa, ka, va, cq, ckv, kr = jnp.split(proj, SPLIT_IDX, axis=-1)
        qa = qa.reshape(B, S, N_HEADS_A, HEAD_DIM_A)
        ka = ka.reshape(B, S, N_KV_A, HEAD_DIM_A)
        va = va.reshape(B, S, N_KV_A, HEAD_DIM_A)
        out_a = swa_sink_alibi_attention(qa, ka, va, positions, sinks[l])
        qb = (rmsnorm(cq, q_a_norm[l]) @ w_q_b[l]).reshape(B, S, N_HEADS_B, Q_HEAD_B)
        kvb = (rmsnorm(ckv, kv_a_norm[l]) @ w_kv_b[l]).reshape(B, S, N_HEADS_B, KV_HEAD_B)
        q_nope, q_rope = qb[..., :QK_NOPE], rope(qb[..., QK_NOPE:], positions)
        k_nope, v_b = kvb[..., :QK_NOPE], kvb[..., QK_NOPE:]
        k_rope = rope(kr[:, :, None, :], positions)[:, :, 0, :]
        out_b = mla_attention(q_nope, q_rope, k_nope, k_rope, v_b)
        merged = jax.nn.sigmoid(g_a) * (out_a @ w_o_a[l]) + jax.nn.sigmoid(g_b) * (out_b @ w_o_b[l])
        x = x + rmsnorm(merged @ w_out[l], post_norm_mix[l])
        h2 = rmsnorm(x, pre_norm_mlp[l])
        y = jnp.square(jax.nn.relu(h2 @ w_up[l])) @ w_down[l]
        x = x + rmsnorm(y, post_norm_mlp[l])
    return x
```

```python
import functools

import jax
import jax.numpy as jnp
from jax import lax
from jax.experimental import pallas as pl
from jax.experimental.pallas import tpu as pltpu

D_MODEL = 1024
N_HEADS_A = 8
N_KV_A = 2
HEAD_DIM_A = 64
WINDOW = 128
BLOCK = 128
N_HEADS_B = 8
QK_NOPE = 64
QK_ROPE = 32
V_DIM_B = 64
Q_LORA = 256
KV_LORA = 128
ROPE_THETA = 10000.0
D_FF = 4 * D_MODEL
EPS = 1e-6

WIDTH_A = N_HEADS_A * HEAD_DIM_A
WIDTH_B = N_HEADS_B * V_DIM_B
Q_HEAD_B = QK_NOPE + QK_ROPE
KV_HEAD_B = QK_NOPE + V_DIM_B

LANES = 128
HALF = LANES // 2
SCALE_A = HEAD_DIM_A ** -0.5
SCALE_B = Q_HEAD_B ** -0.5
ALIBI_SLOPES = tuple(2.0 ** (-8.0 * (h + 1) / N_HEADS_A) for h in range(N_HEADS_A))

W1_QA = 0
W1_KA = W1_QA + WIDTH_A
W1_VA = W1_KA + 2 * LANES
W1_CQ = W1_VA + 2 * LANES
W1_CKV = W1_CQ + Q_LORA
W1_KR = W1_CKV + KV_LORA
W1_KRS = W1_KR + LANES
W1_COLS = W1_KRS + LANES

PROJ_TM = 512
SWA_TQ = 512
MLA_BQ = 512
MERGE_TM = 512
FF_CHUNK = 1024
VMEM_LIMIT = 60 * 1024 * 1024


def _rms(v):
    return v * lax.rsqrt(jnp.mean(v * v, axis=-1, keepdims=True) + EPS)


def _dot(a, b):
    return jnp.dot(a, b, preferred_element_type=jnp.float32)


def _dot_nt(a, b):
    return lax.dot_general(a, b, (((1,), (1,)), ((), ())), preferred_element_type=jnp.float32)


def _proj_kernel(x_ref, pos_ref, gpre_ref, w1_ref, gq_ref, wqm_ref, wqs_ref, gkv_ref, wkk_ref,
                 wkv_ref, freq_ref, sgn_ref, vone_ref,
                 qa_ref, ka_ref, va_ref, qb_ref, kb_ref, vb_ref):
    bf16 = jnp.bfloat16
    hb = (_rms(x_ref[...]) * gpre_ref[...]).astype(bf16)
    proj = _dot(hb, w1_ref[...])
    qa_ref[...] = proj[:, W1_QA:W1_KA].astype(bf16)
    ka_ref[...] = proj[:, W1_KA:W1_VA].astype(bf16)
    va_ref[...] = proj[:, W1_VA:W1_CQ].astype(bf16)
    cq = proj[:, W1_CQ:W1_CKV]
    ckv = proj[:, W1_CKV:W1_KR]
    kr_main = proj[:, W1_KR:W1_KRS]
    kr_swap = proj[:, W1_KRS:W1_COLS]

    ang = pos_ref[...].astype(jnp.float32) * freq_ref[...]
    cos = jnp.cos(ang)
    sin = jnp.sin(ang) * sgn_ref[...]

    cqn = (_rms(cq) * gq_ref[...]).astype(bf16)
    q_main = _dot(cqn, wqm_ref[...])
    q_swap = _dot(cqn, wqs_ref[...])
    ckvn = (_rms(ckv) * gkv_ref[...]).astype(bf16)
    k_nope = _dot(ckvn, wkk_ref[...])
    vb_ref[...] = (_dot(ckvn, wkv_ref[...]) + vone_ref[...]).astype(bf16)
    k_rot = kr_main * cos + kr_swap * sin
    for h in range(N_HEADS_B):
        sl = slice(h * LANES, (h + 1) * LANES)
        qb_ref[:, sl] = ((q_main[:, sl] * cos + q_swap[:, sl] * sin) * SCALE_B).astype(bf16)
        kb_ref[:, sl] = (k_nope[:, sl] + k_rot).astype(bf16)


def _swa_kernel(sink_ref, q_ref, kc_ref, kp_ref, vc_ref, vp_ref, qpos_ref, kposc_ref, kposp_ref,
                o_ref):
    bf16 = jnp.bfloat16
    i = pl.program_id(1)
    row = lax.broadcasted_iota(jnp.int32, (BLOCK, 2 * BLOCK), 0)
    col = lax.broadcasted_iota(jnp.int32, (BLOCK, 2 * BLOCK), 1)
    diff = row + BLOCK - col
    band = (diff >= 0) & (diff < WINDOW)
    lane = lax.broadcasted_iota(jnp.int32, (BLOCK, LANES), 1)
    low_half = lane < HALF
    neg_inf = jnp.float32(-jnp.inf)

    for blk in range(SWA_TQ // BLOCK):
        r0 = blk * BLOCK
        if blk == 0:
            kband = jnp.concatenate([kp_ref[0], kc_ref[0, 0:BLOCK, :]], axis=0)
            vband = jnp.concatenate([vp_ref[0], vc_ref[0, 0:BLOCK, :]], axis=0)
            kpos = jnp.concatenate([kposp_ref[0], kposc_ref[0, :, 0:BLOCK]], axis=1)
            mask = band & ((col >= BLOCK) | (i > 0))
        else:
            kband = kc_ref[0, r0 - BLOCK:r0 + BLOCK, :]
            vband = vc_ref[0, r0 - BLOCK:r0 + BLOCK, :]
            kpos = kposc_ref[0, :, r0 - BLOCK:r0 + BLOCK]
            mask = band
        qpos = qpos_ref[0, r0:r0 + BLOCK, :]
        dist = jnp.abs(qpos - kpos).astype(jnp.float32)
        for pair in range(N_HEADS_A // 2):
            kv = (2 * pair) // (N_HEADS_A // N_KV_A)
            qp = q_ref[0, r0:r0 + BLOCK, pair * LANES:(pair + 1) * LANES]
            kx = kband[:, kv * LANES:(kv + 1) * LANES]
            vx = vband[:, kv * LANES:(kv + 1) * LANES]
            outs = []
            for e in range(2):
                h = 2 * pair + e
                keep = low_half if e == 0 else jnp.logical_not(low_half)
                qm = jnp.where(keep, qp, jnp.zeros_like(qp))
                s = _dot_nt(qm, kx) - ALIBI_SLOPES[h] * dist
                s = jnp.where(mask, s, neg_inf)
                sink = sink_ref[h]
                m = jnp.maximum(jnp.max(s, axis=-1, keepdims=True), sink)
                ex = jnp.exp(s - m)
                denom = jnp.sum(ex, axis=-1, keepdims=True) + jnp.exp(sink - m)
                outs.append(_dot(ex.astype(bf16), vx) / denom)
            o_ref[0, r0:r0 + BLOCK, pair * LANES:(pair + 1) * LANES] = (
                jnp.where(low_half, outs[0], outs[1]).astype(bf16))


def _mla_kernel(q_ref, k_ref, v_ref, o_ref, acc_ref, m_ref):
    bf16 = jnp.bfloat16
    bq = MLA_BQ
    qi = pl.program_id(2)
    row = lax.broadcasted_iota(jnp.int32, (bq, bq), 0)
    col = lax.broadcasted_iota(jnp.int32, (bq, bq), 1)
    causal = col <= row
    neg_inf = jnp.float32(-jnp.inf)
    outs = []
    for e in range(2):
        sl = slice(e * LANES, (e + 1) * LANES)
        q = q_ref[0, :, sl]
        m_ref[...] = jnp.full(m_ref.shape, neg_inf, jnp.float32)
        acc_ref[...] = jnp.zeros(acc_ref.shape, jnp.float32)

        def step(kb, masked):
            start = pl.multiple_of(kb * bq, bq)
            k = k_ref[0, pl.ds(start, bq), sl]
            v = v_ref[0, pl.ds(start, bq), sl]
            s = _dot_nt(q, k)
            if masked:
                s = jnp.where(causal, s, neg_inf)
            m_prev = m_ref[...]
            m_new = jnp.maximum(m_prev, jnp.max(s, axis=-1, keepdims=True))
            p = jnp.exp(s - m_new)
            alpha = jnp.exp(m_prev - m_new)
            acc_ref[...] = alpha * acc_ref[...] + _dot(p.astype(bf16), v)
            m_ref[...] = m_new

        def body(kb, carry):
            step(kb, False)
            return carry

        lax.fori_loop(0, qi, body, 0)
        step(qi, True)
        acc = acc_ref[...]
        denom = acc[:, HALF:HALF + 1] if e == 0 else acc[:, 0:1]
        outs.append(acc / denom)
    lane = lax.broadcasted_iota(jnp.int32, (bq, LANES), 1)
    o_ref[0] = jnp.where(lane < HALF, outs[0], outs[1]).astype(bf16)


def _merge_kernel(x_ref, oa_ref, ob_ref, gpre_ref, wg_ref, woa_ref, wob_ref, wout_ref, gpost_ref,
                  gpre2_ref, wup_ref, wdn_ref, gpost2_ref, o_ref):
    bf16 = jnp.bfloat16
    x = x_ref[...]
    hb = (_rms(x) * gpre_ref[...]).astype(bf16)
    gate_a = jax.nn.sigmoid(_dot(hb, wg_ref[:, 0:D_MODEL]))
    gate_b = jax.nn.sigmoid(_dot(hb, wg_ref[:, D_MODEL:2 * D_MODEL]))
    merged = gate_a * _dot(oa_ref[...], woa_ref[...]) + gate_b * _dot(ob_ref[...], wob_ref[...])
    y = _dot(merged.astype(bf16), wout_ref[...])
    x1 = x + _rms(y) * gpost_ref[...]
    h2 = (_rms(x1) * gpre2_ref[...]).astype(bf16)
    y2 = jnp.zeros(x.shape, jnp.float32)
    for c in range(D_FF // FF_CHUNK):
        cs = slice(c * FF_CHUNK, (c + 1) * FF_CHUNK)
        up = jnp.maximum(_dot(h2, wup_ref[:, cs]), 0.0)
        y2 = y2 + _dot((up * up).astype(bf16), wdn_ref[cs, :])
    o_ref[...] = x1 + _rms(y2) * gpost2_ref[...]


def _const_spec(shape):
    return pl.BlockSpec(shape, lambda *_: (0,) * len(shape), pipeline_mode=pl.Buffered(1))


def kernel(x, positions, pre_norm_mix, w_in, q_a_norm, w_q_b, kv_a_norm, w_kv_b, sinks, w_o_a,
           w_o_b, w_out, post_norm_mix, pre_norm_mlp, w_up, w_down, post_norm_mlp):
    f32, bf16 = jnp.float32, jnp.bfloat16
    B, S, D = x.shape
    T = B * S
    depth = w_in.shape[0]
    for l in range(depth):
        wi = w_in[l]
        o_ga, o_qa = 0, 2 * D_MODEL
        o_ka = o_qa + WIDTH_A
        o_va = o_ka + N_KV_A * HEAD_DIM_A
        o_cq = o_va + N_KV_A * HEAD_DIM_A
        o_ckv = o_cq + Q_LORA
        o_kr = o_ckv + KV_LORA
        ka = wi[:, o_ka:o_va]
        va = wi[:, o_va:o_cq]
        kr = wi[:, o_kr:o_kr + QK_ROPE]
        hr = QK_ROPE // 2
        z = lambda n: jnp.zeros((D_MODEL, n), f32)
        w1 = jnp.concatenate([
            wi[:, o_qa:o_ka] * SCALE_A,
            ka[:, :HEAD_DIM_A], ka[:, :HEAD_DIM_A], ka[:, HEAD_DIM_A:], ka[:, HEAD_DIM_A:],
            va[:, :HEAD_DIM_A], va[:, :HEAD_DIM_A], va[:, HEAD_DIM_A:], va[:, HEAD_DIM_A:],
            wi[:, o_cq:o_ckv], wi[:, o_ckv:o_kr],
            z(QK_NOPE), kr, z(LANES - Q_HEAD_B),
            z(QK_NOPE), kr[:, hr:], kr[:, :hr], z(LANES - Q_HEAD_B),
        ], axis=1).astype(bf16)
        w_gates = wi[:, o_ga:o_qa].astype(bf16)

        wq = w_q_b[l].reshape(Q_LORA, N_HEADS_B, Q_HEAD_B)
        q_nope, q_rope = wq[..., :QK_NOPE], wq[..., QK_NOPE:]
        zq = lambda n: jnp.zeros((Q_LORA, N_HEADS_B, n), f32)
        wq_main = jnp.concatenate([q_nope, q_rope, zq(LANES - Q_HEAD_B)], -1)
        wq_swap = jnp.concatenate([zq(QK_NOPE), q_rope[..., hr:], q_rope[..., :hr],
                                   zq(LANES - Q_HEAD_B)], -1)
        wq_main = wq_main.reshape(Q_LORA, N_HEADS_B * LANES).astype(bf16)
        wq_swap = wq_swap.reshape(Q_LORA, N_HEADS_B * LANES).astype(bf16)

        wkv = w_kv_b[l].reshape(KV_LORA, N_HEADS_B, KV_HEAD_B)
        kv_k, kv_v = wkv[..., :QK_NOPE], wkv[..., QK_NOPE:]
        zk = jnp.zeros((KV_LORA, N_HEADS_B, HALF), f32)
        wkv_k = jnp.concatenate([kv_k, zk], -1).reshape(KV_LORA, N_HEADS_B * LANES).astype(bf16)
        even = (jnp.arange(N_HEADS_B) % 2 == 0)[None, :, None]
        wkv_v = jnp.where(even, jnp.concatenate([kv_v, zk], -1), jnp.concatenate([zk, kv_v], -1))
        wkv_v = wkv_v.reshape(KV_LORA, N_HEADS_B * LANES).astype(bf16)
        lane_ids = jnp.arange(N_HEADS_B * LANES)
        head_ids = lane_ids // LANES
        one_lane = jnp.where(head_ids % 2 == 0, HALF, 0)
        v_one = (lane_ids % LANES == one_lane).astype(f32)[None, :]

        freqs = ROPE_THETA ** (-jnp.arange(0, QK_ROPE, 2, dtype=f32) / QK_ROPE)
        zf = jnp.zeros((QK_NOPE,), f32)
        zt = jnp.zeros((LANES - Q_HEAD_B,), f32)
        freq_slab = jnp.concatenate([zf, freqs, freqs, zt])[None, :]
        sgn_slab = jnp.concatenate([zf, -jnp.ones((hr,), f32), jnp.ones((hr,), f32), zt])[None, :]

        row = lambda g: g.reshape(1, -1).astype(f32)
        x2 = x.reshape(T, D)
        pos_col = positions.reshape(T, 1)

        tm = PROJ_TM
        tok = lambda w: pl.BlockSpec((tm, w), lambda i: (i, 0))
        qa, ka_x, va_x, qb, kb, vb = pl.pallas_call(
            _proj_kernel,
            grid=(T // tm,),
            in_specs=[tok(D), tok(1), _const_spec((1, D)), _const_spec((D, W1_COLS)),
                      _const_spec((1, Q_LORA)), _const_spec((Q_LORA, N_HEADS_B * LANES)),
                      _const_spec((Q_LORA, N_HEADS_B * LANES)), _const_spec((1, KV_LORA)),
                      _const_spec((KV_LORA, N_HEADS_B * LANES)),
                      _const_spec((KV_LORA, N_HEADS_B * LANES)),
                      _const_spec((1, LANES)), _const_spec((1, LANES)),
                      _const_spec((1, N_HEADS_B * LANES))],
            out_specs=[tok(WIDTH_A), tok(2 * LANES), tok(2 * LANES), tok(N_HEADS_B * LANES),
                       tok(N_HEADS_B * LANES), tok(N_HEADS_B * LANES)],
            out_shape=[jax.ShapeDtypeStruct((T, WIDTH_A), bf16),
                       jax.ShapeDtypeStruct((T, 2 * LANES), bf16),
                       jax.ShapeDtypeStruct((T, 2 * LANES), bf16),
                       jax.ShapeDtypeStruct((T, N_HEADS_B * LANES), bf16),
                       jax.ShapeDtypeStruct((T, N_HEADS_B * LANES), bf16),
                       jax.ShapeDtypeStruct((T, N_HEADS_B * LANES), bf16)],
            compiler_params=pltpu.CompilerParams(dimension_semantics=("arbitrary",),
                                                 vmem_limit_bytes=VMEM_LIMIT),
            name="proj",
        )(x2, pos_col, row(pre_norm_mix[l]), w1, row(q_a_norm[l]), wq_main, wq_swap,
          row(kv_a_norm[l]), wkv_k, wkv_v, freq_slab, sgn_slab, v_one)

        tq = SWA_TQ
        nb_per = tq // BLOCK
        qa3 = qa.reshape(B, S, WIDTH_A)
        ka3 = ka_x.reshape(B, S, 2 * LANES)
        va3 = va_x.reshape(B, S, 2 * LANES)
        pos_c3 = positions.reshape(B, S, 1)
        pos_r3 = positions.reshape(B, 1, S)
        cur = lambda w: pl.BlockSpec((1, tq, w), lambda b, i: (b, i, 0))
        prev = lambda w: pl.BlockSpec((1, BLOCK, w),
                                      lambda b, i: (b, jnp.maximum(i * nb_per - 1, 0), 0))
        out_a = pl.pallas_call(
            _swa_kernel,
            grid=(B, S // tq),
            in_specs=[pl.BlockSpec(memory_space=pltpu.SMEM),
                      cur(WIDTH_A), cur(2 * LANES), prev(2 * LANES), cur(2 * LANES),
                      prev(2 * LANES), cur(1),
                      pl.BlockSpec((1, 1, tq), lambda b, i: (b, 0, i)),
                      pl.BlockSpec((1, 1, BLOCK),
                                   lambda b, i: (b, 0, jnp.maximum(i * nb_per - 1, 0)))],
            out_specs=cur(WIDTH_A),
            out_shape=jax.ShapeDtypeStruct((B, S, WIDTH_A), bf16),
            compiler_params=pltpu.CompilerParams(dimension_semantics=("arbitrary", "arbitrary"),
                                                 vmem_limit_bytes=VMEM_LIMIT),
            name="swa",
        )(sinks[l].astype(f32), qa3, ka3, ka3, va3, va3, pos_c3, pos_r3, pos_r3)

        bq = MLA_BQ
        qb3 = qb.reshape(B, S, N_HEADS_B * LANES)
        kb3 = kb.reshape(B, S, N_HEADS_B * LANES)
        vb3 = vb.reshape(B, S, N_HEADS_B * LANES)
        out_b = pl.pallas_call(
            _mla_kernel,
            grid=(B, N_HEADS_B // 2, S // bq),
            in_specs=[pl.BlockSpec((1, bq, 2 * LANES), lambda b, j, i: (b, i, j)),
                      pl.BlockSpec((1, S, 2 * LANES), lambda b, j, i: (b, 0, j)),
                      pl.BlockSpec((1, S, 2 * LANES), lambda b, j, i: (b, 0, j))],
            out_specs=pl.BlockSpec((1, bq, LANES), lambda b, j, i: (b, i, j)),
            out_shape=jax.ShapeDtypeStruct((B, S, WIDTH_B), bf16),
            scratch_shapes=[pltpu.VMEM((bq, LANES), f32), pltpu.VMEM((bq, 1), f32)],
            compiler_params=pltpu.CompilerParams(
                dimension_semantics=("arbitrary", "arbitrary", "arbitrary"),
                vmem_limit_bytes=VMEM_LIMIT),
            name="mla",
        )(qb3, kb3, vb3)

        tm = MERGE_TM
        tok = lambda w: pl.BlockSpec((tm, w), lambda i: (i, 0))
        x2 = pl.pallas_call(
            _merge_kernel,
            grid=(T // tm,),
            in_specs=[tok(D), tok(WIDTH_A), tok(WIDTH_B), _const_spec((1, D)),
                      _const_spec((D, 2 * D)), _const_spec((WIDTH_A, D)), _const_spec((WIDTH_B, D)),
                      _const_spec((D, D)), _const_spec((1, D)), _const_spec((1, D)),
                      _const_spec((D, D_FF)), _const_spec((D_FF, D)), _const_spec((1, D))],
            out_specs=tok(D),
            out_shape=jax.ShapeDtypeStruct((T, D), f32),
            compiler_params=pltpu.CompilerParams(dimension_semantics=("arbitrary",),
                                                 vmem_limit_bytes=VMEM_LIMIT),
            name="merge_mlp",
        )(x2, out_a.reshape(T, WIDTH_A), out_b.reshape(T, WIDTH_B), row(pre_norm_mix[l]), w_gates,
          w_o_a[l].astype(bf16), w_o_b[l].astype(bf16), w_out[l].astype(bf16),
          row(post_norm_mix[l]), row(pre_norm_mlp[l]), w_up[l].astype(bf16),
          w_down[l].astype(bf16), row(post_norm_mlp[l]))
        x = x2.reshape(B, S, D)
    return x
```

```python
import functools

import jax
import jax.numpy as jnp
from jax import lax
from jax.experimental import pallas as pl
from jax.experimental.pallas import tpu as pltpu

D_MODEL = 1024
N_HEADS_A = 8
N_KV_A = 2
HEAD_DIM_A = 64
WINDOW = 128
BLOCK = 128
N_HEADS_B = 8
QK_NOPE = 64
QK_ROPE = 32
V_DIM_B = 64
Q_LORA = 256
KV_LORA = 128
ROPE_THETA = 10000.0
D_FF = 4 * D_MODEL
EPS = 1e-6

WIDTH_A = N_HEADS_A * HEAD_DIM_A
WIDTH_B = N_HEADS_B * V_DIM_B
Q_HEAD_B = QK_NOPE + QK_ROPE
KV_HEAD_B = QK_NOPE + V_DIM_B

LANES = 128
HALF = LANES // 2
SCALE_A = HEAD_DIM_A ** -0.5
SCALE_B = Q_HEAD_B ** -0.5
LOG2_E = 1.4426950408889634
QSCALE_B = SCALE_B * LOG2_E
ALIBI_SLOPES = tuple(2.0 ** (-8.0 * (h + 1) / N_HEADS_A) for h in range(N_HEADS_A))

W1_QA = 0
W1_KA = W1_QA + WIDTH_A
W1_VA = W1_KA + 2 * LANES
W1_CQ = W1_VA + 2 * LANES
W1_CKV = W1_CQ + Q_LORA
W1_KR = W1_CKV + KV_LORA
W1_KRS = W1_KR + LANES
W1_COLS = W1_KRS + LANES

PROJ_TM = 512
SWA_TQ = 512
MLA_BQ = 512
MERGE_TM = 512
FF_CHUNK = 1024
VMEM_LIMIT = 60 * 1024 * 1024


def _rms(v):
    return v * lax.rsqrt(jnp.mean(v * v, axis=-1, keepdims=True) + EPS)


def _dot(a, b):
    return jnp.dot(a, b, preferred_element_type=jnp.float32)


def _dot_nt(a, b):
    return lax.dot_general(a, b, (((1,), (1,)), ((), ())), preferred_element_type=jnp.float32)


def _proj_kernel(x_ref, pos_ref, gpre_ref, w1_ref, gq_ref, wqm_ref, wqs_ref, gkv_ref, wkk_ref,
                 wkvt_ref, freq_ref, sgn_ref,
                 qa_ref, ka_ref, va_ref, qb_ref, kb_ref, vt_ref):
    bf16 = jnp.bfloat16
    hb = (_rms(x_ref[...]) * gpre_ref[...]).astype(bf16)
    proj = _dot(hb, w1_ref[...])
    qa_ref[...] = proj[:, W1_QA:W1_KA].astype(bf16)
    ka_ref[...] = proj[:, W1_KA:W1_VA].astype(bf16)
    va_ref[...] = proj[:, W1_VA:W1_CQ].astype(bf16)
    cq = proj[:, W1_CQ:W1_CKV]
    ckv = proj[:, W1_CKV:W1_KR]
    kr_main = proj[:, W1_KR:W1_KRS]
    kr_swap = proj[:, W1_KRS:W1_COLS]

    ang = pos_ref[...].astype(jnp.float32) * freq_ref[...]
    cos = jnp.cos(ang)
    sin = jnp.sin(ang) * sgn_ref[...]

    cqn = (_rms(cq) * gq_ref[...]).astype(bf16)
    q_main = _dot(cqn, wqm_ref[...])
    q_swap = _dot(cqn, wqs_ref[...])
    ckvn = (_rms(ckv) * gkv_ref[...]).astype(bf16)
    k_nope = _dot(ckvn, wkk_ref[...])
    v_t = _dot_nt(wkvt_ref[...], ckvn)
    vrow = lax.broadcasted_iota(jnp.int32, v_t.shape, 0)
    vt_ref[0, 0] = jnp.where(vrow % LANES == V_DIM_B, 1.0, v_t).astype(bf16)
    k_rot = kr_main * cos + kr_swap * sin
    for h in range(N_HEADS_B):
        sl = slice(h * LANES, (h + 1) * LANES)
        qb_ref[:, sl] = ((q_main[:, sl] * cos + q_swap[:, sl] * sin) * QSCALE_B).astype(bf16)
        kb_ref[:, sl] = (k_nope[:, sl] + k_rot).astype(bf16)


def _swa_kernel(sink_ref, q_ref, kc_ref, kp_ref, vc_ref, vp_ref, qpos_ref, kposc_ref, kposp_ref,
                o_ref):
    bf16 = jnp.bfloat16
    i = pl.program_id(1)
    row = lax.broadcasted_iota(jnp.int32, (BLOCK, 2 * BLOCK), 0)
    col = lax.broadcasted_iota(jnp.int32, (BLOCK, 2 * BLOCK), 1)
    diff = row + BLOCK - col
    band = (diff >= 0) & (diff < WINDOW)
    lane = lax.broadcasted_iota(jnp.int32, (BLOCK, LANES), 1)
    low_half = lane < HALF
    neg_inf = jnp.float32(-jnp.inf)

    for blk in range(SWA_TQ // BLOCK):
        r0 = blk * BLOCK
        if blk == 0:
            kband = jnp.concatenate([kp_ref[0], kc_ref[0, 0:BLOCK, :]], axis=0)
            vband = jnp.concatenate([vp_ref[0], vc_ref[0, 0:BLOCK, :]], axis=0)
            kpos = jnp.concatenate([kposp_ref[0], kposc_ref[0, :, 0:BLOCK]], axis=1)
            mask = band & ((col >= BLOCK) | (i > 0))
        else:
            kband = kc_ref[0, r0 - BLOCK:r0 + BLOCK, :]
            vband = vc_ref[0, r0 - BLOCK:r0 + BLOCK, :]
            kpos = kposc_ref[0, :, r0 - BLOCK:r0 + BLOCK]
            mask = band
        qpos = qpos_ref[0, r0:r0 + BLOCK, :]
        dist = jnp.abs(qpos - kpos).astype(jnp.float32)
        for pair in range(N_HEADS_A // 2):
            kv = (2 * pair) // (N_HEADS_A // N_KV_A)
            qp = q_ref[0, r0:r0 + BLOCK, pair * LANES:(pair + 1) * LANES]
            kx = kband[:, kv * LANES:(kv + 1) * LANES]
            vx = vband[:, kv * LANES:(kv + 1) * LANES]
            outs = []
            for e in range(2):
                h = 2 * pair + e
                keep = low_half if e == 0 else jnp.logical_not(low_half)
                qm = jnp.where(keep, qp, jnp.zeros_like(qp))
                s = _dot_nt(qm, kx) - ALIBI_SLOPES[h] * dist
                s = jnp.where(mask, s, neg_inf)
                sink = sink_ref[h]
                m = jnp.maximum(jnp.max(s, axis=-1, keepdims=True), sink)
                ex = jnp.exp(s - m)
                denom = jnp.sum(ex, axis=-1, keepdims=True) + jnp.exp(sink - m)
                outs.append(_dot(ex.astype(bf16), vx) / denom)
            o_ref[0, r0:r0 + BLOCK, pair * LANES:(pair + 1) * LANES] = (
                jnp.where(low_half, outs[0], outs[1]).astype(bf16))


def _mla_kernel(q_ref, k_ref, vt_ref, o_ref, acc_ref, m_ref):
    bf16 = jnp.bfloat16
    bq = MLA_BQ
    qi = pl.program_id(2)
    krow = lax.broadcasted_iota(jnp.int32, (bq, bq), 0)
    qcol = lax.broadcasted_iota(jnp.int32, (bq, bq), 1)
    causal = krow <= qcol
    neg_inf = jnp.float32(-jnp.inf)
    m_ref[...] = jnp.full(m_ref.shape, neg_inf, jnp.float32)
    acc_ref[...] = jnp.zeros(acc_ref.shape, jnp.float32)

    def step(kb, masked):
        start = pl.multiple_of(kb * bq, bq)
        scores = []
        for e in range(2):
            sl = slice(e * LANES, (e + 1) * LANES)
            q = q_ref[0, :, sl]
            k = k_ref[0, pl.ds(start, bq), sl]
            scores.append(_dot_nt(k, q))
        for e in range(2):
            sl = slice(e * LANES, (e + 1) * LANES)
            vt = vt_ref[0, kb, sl, :]
            s = scores[e]
            if masked:
                s = jnp.where(causal, s, neg_inf)
            m_prev = m_ref[e]
            m_new = jnp.maximum(m_prev, jnp.max(s, axis=0, keepdims=True))
            p = jnp.exp2(s - m_new)
            alpha = jnp.exp2(m_prev - m_new)
            acc_ref[e] = alpha * acc_ref[e] + _dot(vt, p.astype(bf16))
            m_ref[e] = m_new

    def body(kb, carry):
        step(kb, False)
        return carry

    lax.fori_loop(0, qi, body, 0)
    step(qi, True)
    outs = []
    for e in range(2):
        acc = acc_ref[e]
        outs.append(acc[0:V_DIM_B] / acc[V_DIM_B:V_DIM_B + 1])
    o_ref[0] = jnp.concatenate(outs, axis=0).T.astype(bf16)


def _merge_kernel(x_ref, oa_ref, ob_ref, gpre_ref, wg_ref, woa_ref, wob_ref, wout_ref, gpost_ref,
                  gpre2_ref, wup_ref, wdn_ref, gpost2_ref, o_ref):
    bf16 = jnp.bfloat16
    x = x_ref[...]
    hb = (_rms(x) * gpre_ref[...]).astype(bf16)
    gate_a = jax.nn.sigmoid(_dot(hb, wg_ref[:, 0:D_MODEL]))
    gate_b = jax.nn.sigmoid(_dot(hb, wg_ref[:, D_MODEL:2 * D_MODEL]))
    merged = gate_a * _dot(oa_ref[...], woa_ref[...]) + gate_b * _dot(ob_ref[...], wob_ref[...])
    y = _dot(merged.astype(bf16), wout_ref[...])
    x1 = x + _rms(y) * gpost_ref[...]
    h2 = (_rms(x1) * gpre2_ref[...]).astype(bf16)
    y2 = jnp.zeros(x.shape, jnp.float32)
    for c in range(D_FF // FF_CHUNK):
        cs = slice(c * FF_CHUNK, (c + 1) * FF_CHUNK)
        up = jnp.maximum(_dot(h2, wup_ref[:, cs]), 0.0)
        y2 = y2 + _dot((up * up).astype(bf16), wdn_ref[cs, :])
    o_ref[...] = x1 + _rms(y2) * gpost2_ref[...]


def _const_spec(shape):
    return pl.BlockSpec(shape, lambda *_: (0,) * len(shape), pipeline_mode=pl.Buffered(1))


def kernel(x, positions, pre_norm_mix, w_in, q_a_norm, w_q_b, kv_a_norm, w_kv_b, sinks, w_o_a,
           w_o_b, w_out, post_norm_mix, pre_norm_mlp, w_up, w_down, post_norm_mlp):
    f32, bf16 = jnp.float32, jnp.bfloat16
    B, S, D = x.shape
    T = B * S
    depth = w_in.shape[0]
    for l in range(depth):
        wi = w_in[l]
        o_ga, o_qa = 0, 2 * D_MODEL
        o_ka = o_qa + WIDTH_A
        o_va = o_ka + N_KV_A * HEAD_DIM_A
        o_cq = o_va + N_KV_A * HEAD_DIM_A
        o_ckv = o_cq + Q_LORA
        o_kr = o_ckv + KV_LORA
        ka = wi[:, o_ka:o_va]
        va = wi[:, o_va:o_cq]
        kr = wi[:, o_kr:o_kr + QK_ROPE]
        hr = QK_ROPE // 2
        z = lambda n: jnp.zeros((D_MODEL, n), f32)
        w1 = jnp.concatenate([
            wi[:, o_qa:o_ka] * SCALE_A,
            ka[:, :HEAD_DIM_A], ka[:, :HEAD_DIM_A], ka[:, HEAD_DIM_A:], ka[:, HEAD_DIM_A:],
            va[:, :HEAD_DIM_A], va[:, :HEAD_DIM_A], va[:, HEAD_DIM_A:], va[:, HEAD_DIM_A:],
            wi[:, o_cq:o_ckv], wi[:, o_ckv:o_kr],
            z(QK_NOPE), kr, z(LANES - Q_HEAD_B),
            z(QK_NOPE), kr[:, hr:], kr[:, :hr], z(LANES - Q_HEAD_B),
        ], axis=1).astype(bf16)
        w_gates = wi[:, o_ga:o_qa].astype(bf16)

        wq = w_q_b[l].reshape(Q_LORA, N_HEADS_B, Q_HEAD_B)
        q_nope, q_rope = wq[..., :QK_NOPE], wq[..., QK_NOPE:]
        zq = lambda n: jnp.zeros((Q_LORA, N_HEADS_B, n), f32)
        wq_main = jnp.concatenate([q_nope, q_rope, zq(LANES - Q_HEAD_B)], -1)
        wq_swap = jnp.concatenate([zq(QK_NOPE), q_rope[..., hr:], q_rope[..., :hr],
                                   zq(LANES - Q_HEAD_B)], -1)
        wq_main = wq_main.reshape(Q_LORA, N_HEADS_B * LANES).astype(bf16)
        wq_swap = wq_swap.reshape(Q_LORA, N_HEADS_B * LANES).astype(bf16)

        wkv = w_kv_b[l].reshape(KV_LORA, N_HEADS_B, KV_HEAD_B)
        kv_k, kv_v = wkv[..., :QK_NOPE], wkv[..., QK_NOPE:]
        zk = jnp.zeros((KV_LORA, N_HEADS_B, HALF), f32)
        wkv_k = jnp.concatenate([kv_k, zk], -1).reshape(KV_LORA, N_HEADS_B * LANES).astype(bf16)
        wkv_vt = jnp.concatenate([kv_v, zk], -1).reshape(KV_LORA, N_HEADS_B * LANES).T.astype(bf16)

        freqs = ROPE_THETA ** (-jnp.arange(0, QK_ROPE, 2, dtype=f32) / QK_ROPE)
        zf = jnp.zeros((QK_NOPE,), f32)
        zt = jnp.zeros((LANES - Q_HEAD_B,), f32)
        freq_slab = jnp.concatenate([zf, freqs, freqs, zt])[None, :]
        sgn_slab = jnp.concatenate([zf, -jnp.ones((hr,), f32), jnp.ones((hr,), f32), zt])[None, :]

        row = lambda g: g.reshape(1, -1).astype(f32)
        x2 = x.reshape(T, D)
        pos_col = positions.reshape(T, 1)

        tm = PROJ_TM
        assert tm == MLA_BQ and S % tm == 0
        nkb = S // tm
        tok = lambda w: pl.BlockSpec((tm, w), lambda i: (i, 0))
        qa, ka_x, va_x, qb, kb, vt = pl.pallas_call(
            _proj_kernel,
            grid=(T // tm,),
            in_specs=[tok(D), tok(1), _const_spec((1, D)), _const_spec((D, W1_COLS)),
                      _const_spec((1, Q_LORA)), _const_spec((Q_LORA, N_HEADS_B * LANES)),
                      _const_spec((Q_LORA, N_HEADS_B * LANES)), _const_spec((1, KV_LORA)),
                      _const_spec((KV_LORA, N_HEADS_B * LANES)),
                      _const_spec((N_HEADS_B * LANES, KV_LORA)),
                      _const_spec((1, LANES)), _const_spec((1, LANES))],
            out_specs=[tok(WIDTH_A), tok(2 * LANES), tok(2 * LANES), tok(N_HEADS_B * LANES),
                       tok(N_HEADS_B * LANES),
                       pl.BlockSpec((1, 1, N_HEADS_B * LANES, tm),
                                    lambda i: (i // nkb, i % nkb, 0, 0))],
            out_shape=[jax.ShapeDtypeStruct((T, WIDTH_A), bf16),
                       jax.ShapeDtypeStruct((T, 2 * LANES), bf16),
                       jax.ShapeDtypeStruct((T, 2 * LANES), bf16),
                       jax.ShapeDtypeStruct((T, N_HEADS_B * LANES), bf16),
                       jax.ShapeDtypeStruct((T, N_HEADS_B * LANES), bf16),
                       jax.ShapeDtypeStruct((B, nkb, N_HEADS_B * LANES, tm), bf16)],
            compiler_params=pltpu.CompilerParams(dimension_semantics=("arbitrary",),
                                                 vmem_limit_bytes=VMEM_LIMIT),
            name="proj",
        )(x2, pos_col, row(pre_norm_mix[l]), w1, row(q_a_norm[l]), wq_main, wq_swap,
          row(kv_a_norm[l]), wkv_k, wkv_vt, freq_slab, sgn_slab)

        tq = SWA_TQ
        nb_per = tq // BLOCK
        qa3 = qa.reshape(B, S, WIDTH_A)
        ka3 = ka_x.reshape(B, S, 2 * LANES)
        va3 = va_x.reshape(B, S, 2 * LANES)
        pos_c3 = positions.reshape(B, S, 1)
        pos_r3 = positions.reshape(B, 1, S)
        cur = lambda w: pl.BlockSpec((1, tq, w), lambda b, i: (b, i, 0))
        prev = lambda w: pl.BlockSpec((1, BLOCK, w),
                                      lambda b, i: (b, jnp.maximum(i * nb_per - 1, 0), 0))
        out_a = pl.pallas_call(
            _swa_kernel,
            grid=(B, S // tq),
            in_specs=[pl.BlockSpec(memory_space=pltpu.SMEM),
                      cur(WIDTH_A), cur(2 * LANES), prev(2 * LANES), cur(2 * LANES),
                      prev(2 * LANES), cur(1),
                      pl.BlockSpec((1, 1, tq), lambda b, i: (b, 0, i)),
                      pl.BlockSpec((1, 1, BLOCK),
                                   lambda b, i: (b, 0, jnp.maximum(i * nb_per - 1, 0)))],
            out_specs=cur(WIDTH_A),
            out_shape=jax.ShapeDtypeStruct((B, S, WIDTH_A), bf16),
            compiler_params=pltpu.CompilerParams(dimension_semantics=("arbitrary", "arbitrary"),
                                                 vmem_limit_bytes=VMEM_LIMIT),
            name="swa",
        )(sinks[l].astype(f32), qa3, ka3, ka3, va3, va3, pos_c3, pos_r3, pos_r3)

        bq = MLA_BQ
        qb3 = qb.reshape(B, S, N_HEADS_B * LANES)
        kb3 = kb.reshape(B, S, N_HEADS_B * LANES)
        out_b = pl.pallas_call(
            _mla_kernel,
            grid=(B, N_HEADS_B // 2, S // bq),
            in_specs=[pl.BlockSpec((1, bq, 2 * LANES), lambda b, j, i: (b, i, j)),
                      pl.BlockSpec((1, S, 2 * LANES), lambda b, j, i: (b, 0, j)),
                      pl.BlockSpec((1, nkb, 2 * LANES, bq), lambda b, j, i: (b, 0, j, 0))],
            out_specs=pl.BlockSpec((1, bq, LANES), lambda b, j, i: (b, i, j)),
            out_shape=jax.ShapeDtypeStruct((B, S, WIDTH_B), bf16),
            scratch_shapes=[pltpu.VMEM((2, LANES, bq), f32), pltpu.VMEM((2, 1, bq), f32)],
            compiler_params=pltpu.CompilerParams(
                dimension_semantics=("arbitrary", "arbitrary", "arbitrary"),
                vmem_limit_bytes=VMEM_LIMIT),
            name="mla",
        )(qb3, kb3, vt)

        tm = MERGE_TM
        tok = lambda w: pl.BlockSpec((tm, w), lambda i: (i, 0))
        x2 = pl.pallas_call(
            _merge_kernel,
            grid=(T // tm,),
            in_specs=[tok(D), tok(WIDTH_A), tok(WIDTH_B), _const_spec((1, D)),
                      _const_spec((D, 2 * D)), _const_spec((WIDTH_A, D)), _const_spec((WIDTH_B, D)),
                      _const_spec((D, D)), _const_spec((1, D)), _const_spec((1, D)),
                      _const_spec((D, D_FF)), _const_spec((D_FF, D)), _const_spec((1, D))],
            out_specs=tok(D),
            out_shape=jax.ShapeDtypeStruct((T, D), f32),
            compiler_params=pltpu.CompilerParams(dimension_semantics=("arbitrary",),
                                                 vmem_limit_bytes=VMEM_LIMIT),
            name="merge_mlp",
        )(x2, out_a.reshape(T, WIDTH_A), out_b.reshape(T, WIDTH_B), row(pre_norm_mix[l]), w_gates,
          w_o_a[l].astype(bf16), w_o_b[l].astype(bf16), w_out[l].astype(bf16),
          row(post_norm_mix[l]), row(pre_norm_mlp[l]), w_up[l].astype(bf16),
          w_down[l].astype(bf16), row(post_norm_mlp[l]))
        x = x2.reshape(B, S, D)
    return x
```

```python
import functools

import jax
import jax.numpy as jnp
from jax import lax
from jax.experimental import pallas as pl
from jax.experimental.pallas import tpu as pltpu

D_MODEL = 1024
N_HEADS_A = 8
N_KV_A = 2
HEAD_DIM_A = 64
WINDOW = 128
BLOCK = 128
N_HEADS_B = 8
QK_NOPE = 64
QK_ROPE = 32
V_DIM_B = 64
Q_LORA = 256
KV_LORA = 128
ROPE_THETA = 10000.0
D_FF = 4 * D_MODEL
EPS = 1e-6

WIDTH_A = N_HEADS_A * HEAD_DIM_A
WIDTH_B = N_HEADS_B * V_DIM_B
Q_HEAD_B = QK_NOPE + QK_ROPE
KV_HEAD_B = QK_NOPE + V_DIM_B

LANES = 128
HALF = LANES // 2
SCALE_A = HEAD_DIM_A ** -0.5
SCALE_B = Q_HEAD_B ** -0.5
LOG2_E = 1.4426950408889634
QSCALE_B = SCALE_B * LOG2_E
ALIBI_SLOPES = tuple(2.0 ** (-8.0 * (h + 1) / N_HEADS_A) for h in range(N_HEADS_A))

W1_QA = 0
W1_KA = W1_QA + WIDTH_A
W1_VA = W1_KA + 2 * LANES
W1_CQ = W1_VA + 2 * LANES
W1_CKV = W1_CQ + Q_LORA
W1_KR = W1_CKV + KV_LORA
W1_KRS = W1_KR + LANES
W1_COLS = W1_KRS + LANES

PROJ_TM = 512
SWA_TQ = 512
MLA_BQ = 512
MLA_CHUNK = 256
MERGE_TM = 512
FF_CHUNK = 1024
VMEM_LIMIT = 60 * 1024 * 1024


def _rms(v):
    return v * lax.rsqrt(jnp.mean(v * v, axis=-1, keepdims=True) + EPS)


def _dot(a, b):
    return jnp.dot(a, b, preferred_element_type=jnp.float32)


def _dot_nt(a, b):
    return lax.dot_general(a, b, (((1,), (1,)), ((), ())), preferred_element_type=jnp.float32)


def _proj_kernel(x_ref, pos_ref, gpre_ref, w1_ref, gq_ref, wqm_ref, wqs_ref, gkv_ref, wkk_ref,
                 wkvt_ref, freq_ref, sgn_ref,
                 qa_ref, ka_ref, va_ref, qb_ref, kb_ref, vt_ref):
    bf16 = jnp.bfloat16
    hb = (_rms(x_ref[...]) * gpre_ref[...]).astype(bf16)
    proj = _dot(hb, w1_ref[...])
    qa_ref[...] = proj[:, W1_QA:W1_KA].astype(bf16)
    ka_ref[...] = proj[:, W1_KA:W1_VA].astype(bf16)
    va_ref[...] = proj[:, W1_VA:W1_CQ].astype(bf16)
    cq = proj[:, W1_CQ:W1_CKV]
    ckv = proj[:, W1_CKV:W1_KR]
    kr_main = proj[:, W1_KR:W1_KRS]
    kr_swap = proj[:, W1_KRS:W1_COLS]

    ang = pos_ref[...].astype(jnp.float32) * freq_ref[...]
    cos = jnp.cos(ang)
    sin = jnp.sin(ang) * sgn_ref[...]

    cqn = (_rms(cq) * gq_ref[...]).astype(bf16)
    q_main = _dot(cqn, wqm_ref[...])
    q_swap = _dot(cqn, wqs_ref[...])
    ckvn = (_rms(ckv) * gkv_ref[...]).astype(bf16)
    k_nope = _dot(ckvn, wkk_ref[...])
    v_t = _dot_nt(wkvt_ref[...], ckvn)
    vrow = lax.broadcasted_iota(jnp.int32, v_t.shape, 0)
    vt_ref[0, 0] = jnp.where(vrow % LANES == V_DIM_B, 1.0, v_t).astype(bf16)
    k_rot = kr_main * cos + kr_swap * sin
    for h in range(N_HEADS_B):
        sl = slice(h * LANES, (h + 1) * LANES)
        qb_ref[:, sl] = ((q_main[:, sl] * cos + q_swap[:, sl] * sin) * QSCALE_B).astype(bf16)
        kb_ref[:, sl] = (k_nope[:, sl] + k_rot).astype(bf16)


def _swa_kernel(sink_ref, q_ref, kc_ref, kp_ref, vc_ref, vp_ref, qpos_ref, kposc_ref, kposp_ref,
                o_ref):
    bf16 = jnp.bfloat16
    i = pl.program_id(1)
    row = lax.broadcasted_iota(jnp.int32, (BLOCK, 2 * BLOCK), 0)
    col = lax.broadcasted_iota(jnp.int32, (BLOCK, 2 * BLOCK), 1)
    diff = row + BLOCK - col
    band = (diff >= 0) & (diff < WINDOW)
    lane = lax.broadcasted_iota(jnp.int32, (BLOCK, LANES), 1)
    low_half = lane < HALF
    neg_inf = jnp.float32(-jnp.inf)

    for blk in range(SWA_TQ // BLOCK):
        r0 = blk * BLOCK
        if blk == 0:
            kband = jnp.concatenate([kp_ref[0], kc_ref[0, 0:BLOCK, :]], axis=0)
            vband = jnp.concatenate([vp_ref[0], vc_ref[0, 0:BLOCK, :]], axis=0)
            kpos = jnp.concatenate([kposp_ref[0], kposc_ref[0, :, 0:BLOCK]], axis=1)
            mask = band & ((col >= BLOCK) | (i > 0))
        else:
            kband = kc_ref[0, r0 - BLOCK:r0 + BLOCK, :]
            vband = vc_ref[0, r0 - BLOCK:r0 + BLOCK, :]
            kpos = kposc_ref[0, :, r0 - BLOCK:r0 + BLOCK]
            mask = band
        qpos = qpos_ref[0, r0:r0 + BLOCK, :]
        dist = jnp.abs(qpos - kpos).astype(jnp.float32)
        for pair in range(N_HEADS_A // 2):
            kv = (2 * pair) // (N_HEADS_A // N_KV_A)
            qp = q_ref[0, r0:r0 + BLOCK, pair * LANES:(pair + 1) * LANES]
            kx = kband[:, kv * LANES:(kv + 1) * LANES]
            vx = vband[:, kv * LANES:(kv + 1) * LANES]
            outs = []
            for e in range(2):
                h = 2 * pair + e
                keep = low_half if e == 0 else jnp.logical_not(low_half)
                qm = jnp.where(keep, qp, jnp.zeros_like(qp))
                s = _dot_nt(qm, kx) - ALIBI_SLOPES[h] * dist
                s = jnp.where(mask, s, neg_inf)
                sink = sink_ref[h]
                m = jnp.maximum(jnp.max(s, axis=-1, keepdims=True), sink)
                ex = jnp.exp(s - m)
                denom = jnp.sum(ex, axis=-1, keepdims=True) + jnp.exp(sink - m)
                outs.append(_dot(ex.astype(bf16), vx) / denom)
            o_ref[0, r0:r0 + BLOCK, pair * LANES:(pair + 1) * LANES] = (
                jnp.where(low_half, outs[0], outs[1]).astype(bf16))


def _mla_kernel(q_ref, k_ref, vt_ref, o_ref, s_ref, acc_ref, m_ref):
    bf16 = jnp.bfloat16
    bq = MLA_BQ
    ch = MLA_CHUNK
    qi = pl.program_id(2)
    krow = lax.broadcasted_iota(jnp.int32, (bq, ch), 0)
    qcol = lax.broadcasted_iota(jnp.int32, (bq, ch), 1)
    neg_inf = jnp.float32(-jnp.inf)
    m_ref[...] = jnp.full(m_ref.shape, neg_inf, jnp.float32)
    acc_ref[...] = jnp.zeros(acc_ref.shape, jnp.float32)

    def scores_into(buf, kb):
        start = pl.multiple_of(kb * bq, bq)
        for e in range(2):
            sl = slice(e * LANES, (e + 1) * LANES)
            s_ref[buf, e] = _dot_nt(k_ref[0, pl.ds(start, bq), sl], q_ref[0, :, sl])

    def softmax_pv(buf, kb, masked):
        for e in range(2):
            vt = vt_ref[0, kb, e * LANES:(e + 1) * LANES, :]
            for c in range(bq // ch):
                cs = slice(c * ch, (c + 1) * ch)
                s = s_ref[buf, e, :, cs]
                if masked:
                    s = jnp.where(krow <= qcol + c * ch, s, neg_inf)
                m_prev = m_ref[e, :, cs]
                m_new = jnp.maximum(m_prev, jnp.max(s, axis=0, keepdims=True))
                p = jnp.exp2(s - m_new).astype(bf16)
                alpha = jnp.exp2(m_prev - m_new)
                acc_ref[e, :, cs] = alpha * acc_ref[e, :, cs] + _dot(vt, p)
                m_ref[e, :, cs] = m_new

    def half(src, dst, kb):
        scores_into(dst, kb + 1)
        softmax_pv(src, kb, False)

    def pair_body(t, carry):
        half(0, 1, 2 * t)
        half(1, 0, 2 * t + 1)
        return carry

    scores_into(0, 0)
    lax.fori_loop(0, qi // 2, pair_body, 0)

    @pl.when(qi % 2 == 1)
    def _():
        half(0, 1, qi - 1)
        softmax_pv(1, qi, True)

    @pl.when(qi % 2 == 0)
    def _():
        softmax_pv(0, qi, True)

    outs = []
    for e in range(2):
        acc = acc_ref[e]
        outs.append(acc[0:V_DIM_B] / acc[V_DIM_B:V_DIM_B + 1])
    o_ref[0] = jnp.concatenate(outs, axis=0).T.astype(bf16)


def _merge_kernel(x_ref, oa_ref, ob_ref, gpre_ref, wg_ref, woa_ref, wob_ref, wout_ref, gpost_ref,
                  gpre2_ref, wup_ref, wdn_ref, gpost2_ref, o_ref):
    bf16 = jnp.bfloat16
    x = x_ref[...]
    hb = (_rms(x) * gpre_ref[...]).astype(bf16)
    gate_a = jax.nn.sigmoid(_dot(hb, wg_ref[:, 0:D_MODEL]))
    gate_b = jax.nn.sigmoid(_dot(hb, wg_ref[:, D_MODEL:2 * D_MODEL]))
    merged = gate_a * _dot(oa_ref[...], woa_ref[...]) + gate_b * _dot(ob_ref[...], wob_ref[...])
    y = _dot(merged.astype(bf16), wout_ref[...])
    x1 = x + _rms(y) * gpost_ref[...]
    h2 = (_rms(x1) * gpre2_ref[...]).astype(bf16)
    y2 = jnp.zeros(x.shape, jnp.float32)
    for c in range(D_FF // FF_CHUNK):
        cs = slice(c * FF_CHUNK, (c + 1) * FF_CHUNK)
        up = jnp.maximum(_dot(h2, wup_ref[:, cs]), 0.0)
        y2 = y2 + _dot((up * up).astype(bf16), wdn_ref[cs, :])
    o_ref[...] = x1 + _rms(y2) * gpost2_ref[...]


def _const_spec(shape):
    return pl.BlockSpec(shape, lambda *_: (0,) * len(shape), pipeline_mode=pl.Buffered(1))


def kernel(x, positions, pre_norm_mix, w_in, q_a_norm, w_q_b, kv_a_norm, w_kv_b, sinks, w_o_a,
           w_o_b, w_out, post_norm_mix, pre_norm_mlp, w_up, w_down, post_norm_mlp):
    f32, bf16 = jnp.float32, jnp.bfloat16
    B, S, D = x.shape
    T = B * S
    depth = w_in.shape[0]
    for l in range(depth):
        wi = w_in[l]
        o_ga, o_qa = 0, 2 * D_MODEL
        o_ka = o_qa + WIDTH_A
        o_va = o_ka + N_KV_A * HEAD_DIM_A
        o_cq = o_va + N_KV_A * HEAD_DIM_A
        o_ckv = o_cq + Q_LORA
        o_kr = o_ckv + KV_LORA
        ka = wi[:, o_ka:o_va]
        va = wi[:, o_va:o_cq]
        kr = wi[:, o_kr:o_kr + QK_ROPE]
        hr = QK_ROPE // 2
        z = lambda n: jnp.zeros((D_MODEL, n), f32)
        w1 = jnp.concatenate([
            wi[:, o_qa:o_ka] * SCALE_A,
            ka[:, :HEAD_DIM_A], ka[:, :HEAD_DIM_A], ka[:, HEAD_DIM_A:], ka[:, HEAD_DIM_A:],
            va[:, :HEAD_DIM_A], va[:, :HEAD_DIM_A], va[:, HEAD_DIM_A:], va[:, HEAD_DIM_A:],
            wi[:, o_cq:o_ckv], wi[:, o_ckv:o_kr],
            z(QK_NOPE), kr, z(LANES - Q_HEAD_B),
            z(QK_NOPE), kr[:, hr:], kr[:, :hr], z(LANES - Q_HEAD_B),
        ], axis=1).astype(bf16)
        w_gates = wi[:, o_ga:o_qa].astype(bf16)

        wq = w_q_b[l].reshape(Q_LORA, N_HEADS_B, Q_HEAD_B)
        q_nope, q_rope = wq[..., :QK_NOPE], wq[..., QK_NOPE:]
        zq = lambda n: jnp.zeros((Q_LORA, N_HEADS_B, n), f32)
        wq_main = jnp.concatenate([q_nope, q_rope, zq(LANES - Q_HEAD_B)], -1)
        wq_swap = jnp.concatenate([zq(QK_NOPE), q_rope[..., hr:], q_rope[..., :hr],
                                   zq(LANES - Q_HEAD_B)], -1)
        wq_main = wq_main.reshape(Q_LORA, N_HEADS_B * LANES).astype(bf16)
        wq_swap = wq_swap.reshape(Q_LORA, N_HEADS_B * LANES).astype(bf16)

        wkv = w_kv_b[l].reshape(KV_LORA, N_HEADS_B, KV_HEAD_B)
        kv_k, kv_v = wkv[..., :QK_NOPE], wkv[..., QK_NOPE:]
        zk = jnp.zeros((KV_LORA, N_HEADS_B, HALF), f32)
        wkv_k = jnp.concatenate([kv_k, zk], -1).reshape(KV_LORA, N_HEADS_B * LANES).astype(bf16)
        wkv_vt = jnp.concatenate([kv_v, zk], -1).reshape(KV_LORA, N_HEADS_B * LANES).T.astype(bf16)

        freqs = ROPE_THETA ** (-jnp.arange(0, QK_ROPE, 2, dtype=f32) / QK_ROPE)
        zf = jnp.zeros((QK_NOPE,), f32)
        zt = jnp.zeros((LANES - Q_HEAD_B,), f32)
        freq_slab = jnp.concatenate([zf, freqs, freqs, zt])[None, :]
        sgn_slab = jnp.concatenate([zf, -jnp.ones((hr,), f32), jnp.ones((hr,), f32), zt])[None, :]

        row = lambda g: g.reshape(1, -1).astype(f32)
        x2 = x.reshape(T, D)
        pos_col = positions.reshape(T, 1)

        tm = PROJ_TM
        assert tm == MLA_BQ and S % tm == 0
        nkb = S // tm
        tok = lambda w: pl.BlockSpec((tm, w), lambda i: (i, 0))
        qa, ka_x, va_x, qb, kb, vt = pl.pallas_call(
            _proj_kernel,
            grid=(T // tm,),
            in_specs=[tok(D), tok(1), _const_spec((1, D)), _const_spec((D, W1_COLS)),
                      _const_spec((1, Q_LORA)), _const_spec((Q_LORA, N_HEADS_B * LANES)),
                      _const_spec((Q_LORA, N_HEADS_B * LANES)), _const_spec((1, KV_LORA)),
                      _const_spec((KV_LORA, N_HEADS_B * LANES)),
                      _const_spec((N_HEADS_B * LANES, KV_LORA)),
                      _const_spec((1, LANES)), _const_spec((1, LANES))],
            out_specs=[tok(WIDTH_A), tok(2 * LANES), tok(2 * LANES), tok(N_HEADS_B * LANES),
                       tok(N_HEADS_B * LANES),
                       pl.BlockSpec((1, 1, N_HEADS_B * LANES, tm),
                                    lambda i: (i // nkb, i % nkb, 0, 0))],
            out_shape=[jax.ShapeDtypeStruct((T, WIDTH_A), bf16),
                       jax.ShapeDtypeStruct((T, 2 * LANES), bf16),
                       jax.ShapeDtypeStruct((T, 2 * LANES), bf16),
                       jax.ShapeDtypeStruct((T, N_HEADS_B * LANES), bf16),
                       jax.ShapeDtypeStruct((T, N_HEADS_B * LANES), bf16),
                       jax.ShapeDtypeStruct((B, nkb, N_HEADS_B * LANES, tm), bf16)],
            compiler_params=pltpu.CompilerParams(dimension_semantics=("arbitrary",),
                                                 vmem_limit_bytes=VMEM_LIMIT),
            name="proj",
        )(x2, pos_col, row(pre_norm_mix[l]), w1, row(q_a_norm[l]), wq_main, wq_swap,
          row(kv_a_norm[l]), wkv_k, wkv_vt, freq_slab, sgn_slab)

        tq = SWA_TQ
        nb_per = tq // BLOCK
        qa3 = qa.reshape(B, S, WIDTH_A)
        ka3 = ka_x.reshape(B, S, 2 * LANES)
        va3 = va_x.reshape(B, S, 2 * LANES)
        pos_c3 = positions.reshape(B, S, 1)
        pos_r3 = positions.reshape(B, 1, S)
        cur = lambda w: pl.BlockSpec((1, tq, w), lambda b, i: (b, i, 0))
        prev = lambda w: pl.BlockSpec((1, BLOCK, w),
                                      lambda b, i: (b, jnp.maximum(i * nb_per - 1, 0), 0))
        out_a = pl.pallas_call(
            _swa_kernel,
            grid=(B, S // tq),
            in_specs=[pl.BlockSpec(memory_space=pltpu.SMEM),
                      cur(WIDTH_A), cur(2 * LANES), prev(2 * LANES), cur(2 * LANES),
                      prev(2 * LANES), cur(1),
                      pl.BlockSpec((1, 1, tq), lambda b, i: (b, 0, i)),
                      pl.BlockSpec((1, 1, BLOCK),
                                   lambda b, i: (b, 0, jnp.maximum(i * nb_per - 1, 0)))],
            out_specs=cur(WIDTH_A),
            out_shape=jax.ShapeDtypeStruct((B, S, WIDTH_A), bf16),
            compiler_params=pltpu.CompilerParams(dimension_semantics=("arbitrary", "arbitrary"),
                                                 vmem_limit_bytes=VMEM_LIMIT),
            name="swa",
        )(sinks[l].astype(f32), qa3, ka3, ka3, va3, va3, pos_c3, pos_r3, pos_r3)

        bq = MLA_BQ
        qb3 = qb.reshape(B, S, N_HEADS_B * LANES)
        kb3 = kb.reshape(B, S, N_HEADS_B * LANES)
        out_b = pl.pallas_call(
            _mla_kernel,
            grid=(B, N_HEADS_B // 2, S // bq),
            in_specs=[pl.BlockSpec((1, bq, 2 * LANES), lambda b, j, i: (b, i, j)),
                      pl.BlockSpec((1, S, 2 * LANES), lambda b, j, i: (b, 0, j)),
                      pl.BlockSpec((1, nkb, 2 * LANES, bq), lambda b, j, i: (b, 0, j, 0))],
            out_specs=pl.BlockSpec((1, bq, LANES), lambda b, j, i: (b, i, j)),
            out_shape=jax.ShapeDtypeStruct((B, S, WIDTH_B), bf16),
            scratch_shapes=[pltpu.VMEM((2, 2, bq, bq), f32),
                            pltpu.VMEM((2, LANES, bq), f32), pltpu.VMEM((2, 1, bq), f32)],
            compiler_params=pltpu.CompilerParams(
                dimension_semantics=("arbitrary", "arbitrary", "arbitrary"),
                vmem_limit_bytes=VMEM_LIMIT),
            name="mla",
        )(qb3, kb3, vt)

        tm = MERGE_TM
        tok = lambda w: pl.BlockSpec((tm, w), lambda i: (i, 0))
        x2 = pl.pallas_call(
            _merge_kernel,
            grid=(T // tm,),
            in_specs=[tok(D), tok(WIDTH_A), tok(WIDTH_B), _const_spec((1, D)),
                      _const_spec((D, 2 * D)), _const_spec((WIDTH_A, D)), _const_spec((WIDTH_B, D)),
                      _const_spec((D, D)), _const_spec((1, D)), _const_spec((1, D)),
                      _const_spec((D, D_FF)), _const_spec((D_FF, D)), _const_spec((1, D))],
            out_specs=tok(D),
            out_shape=jax.ShapeDtypeStruct((T, D), f32),
            compiler_params=pltpu.CompilerParams(dimension_semantics=("arbitrary",),
                                                 vmem_limit_bytes=VMEM_LIMIT),
            name="merge_mlp",
        )(x2, out_a.reshape(T, WIDTH_A), out_b.reshape(T, WIDTH_B), row(pre_norm_mix[l]), w_gates,
          w_o_a[l].astype(bf16), w_o_b[l].astype(bf16), w_out[l].astype(bf16),
          row(post_norm_mix[l]), row(pre_norm_mlp[l]), w_up[l].astype(bf16),
          w_down[l].astype(bf16), row(post_norm_mlp[l]))
        x = x2.reshape(B, S, D)
    return x
```

```python
import functools

import jax
import jax.numpy as jnp
from jax import lax
from jax.experimental import pallas as pl
from jax.experimental.pallas import tpu as pltpu

D_MODEL = 1024
N_HEADS_A = 8
N_KV_A = 2
HEAD_DIM_A = 64
WINDOW = 128
BLOCK = 128
N_HEADS_B = 8
QK_NOPE = 64
QK_ROPE = 32
V_DIM_B = 64
Q_LORA = 256
KV_LORA = 128
ROPE_THETA = 10000.0
D_FF = 4 * D_MODEL
EPS = 1e-6

WIDTH_A = N_HEADS_A * HEAD_DIM_A
WIDTH_B = N_HEADS_B * V_DIM_B
Q_HEAD_B = QK_NOPE + QK_ROPE
KV_HEAD_B = QK_NOPE + V_DIM_B

LANES = 128
HALF = LANES // 2
SCALE_A = HEAD_DIM_A ** -0.5
SCALE_B = Q_HEAD_B ** -0.5
LOG2_E = 1.4426950408889634
QSCALE_B = SCALE_B * LOG2_E
ALIBI_SLOPES = tuple(2.0 ** (-8.0 * (h + 1) / N_HEADS_A) for h in range(N_HEADS_A))

W1_QA = 0
W1_KA = W1_QA + WIDTH_A
W1_VA = W1_KA + 2 * LANES
W1_CQ = W1_VA + 2 * LANES
W1_CKV = W1_CQ + Q_LORA
W1_KR = W1_CKV + KV_LORA
W1_KRS = W1_KR + LANES
W1_COLS = W1_KRS + LANES

PROJ_TM = 512
SWA_TQ = 512
MLA_BQ = 512
MLA_CHUNK = 256
MLA_HEADS = 4
MLA_LEAD = 2
MERGE_TM = 512
FF_CHUNK = 1024
VMEM_LIMIT = 60 * 1024 * 1024


def _rms(v):
    return v * lax.rsqrt(jnp.mean(v * v, axis=-1, keepdims=True) + EPS)


def _dot(a, b):
    return jnp.dot(a, b, preferred_element_type=jnp.float32)


def _dot_nt(a, b):
    return lax.dot_general(a, b, (((1,), (1,)), ((), ())), preferred_element_type=jnp.float32)


def _proj_kernel(x_ref, pos_ref, gpre_ref, w1_ref, gq_ref, wqm_ref, wqs_ref, gkv_ref, wkk_ref,
                 wkvt_ref, freq_ref, sgn_ref,
                 qa_ref, ka_ref, va_ref, qb_ref, kb_ref, vt_ref):
    bf16 = jnp.bfloat16
    hb = (_rms(x_ref[...]) * gpre_ref[...]).astype(bf16)
    proj = _dot(hb, w1_ref[...])
    qa_ref[...] = proj[:, W1_QA:W1_KA].astype(bf16)
    ka_ref[...] = proj[:, W1_KA:W1_VA].astype(bf16)
    va_ref[...] = proj[:, W1_VA:W1_CQ].astype(bf16)
    cq = proj[:, W1_CQ:W1_CKV]
    ckv = proj[:, W1_CKV:W1_KR]
    kr_main = proj[:, W1_KR:W1_KRS]
    kr_swap = proj[:, W1_KRS:W1_COLS]

    ang = pos_ref[...].astype(jnp.float32) * freq_ref[...]
    cos = jnp.cos(ang)
    sin = jnp.sin(ang) * sgn_ref[...]

    cqn = (_rms(cq) * gq_ref[...]).astype(bf16)
    q_main = _dot(cqn, wqm_ref[...])
    q_swap = _dot(cqn, wqs_ref[...])
    ckvn = (_rms(ckv) * gkv_ref[...]).astype(bf16)
    k_nope = _dot(ckvn, wkk_ref[...])
    v_t = _dot_nt(wkvt_ref[...], ckvn)
    vrow = lax.broadcasted_iota(jnp.int32, v_t.shape, 0)
    vt_ref[0, 0] = jnp.where(vrow % LANES == V_DIM_B, 1.0, v_t).astype(bf16)
    k_rot = kr_main * cos + kr_swap * sin
    for h in range(N_HEADS_B):
        sl = slice(h * LANES, (h + 1) * LANES)
        qb_ref[:, sl] = ((q_main[:, sl] * cos + q_swap[:, sl] * sin) * QSCALE_B).astype(bf16)
        kb_ref[:, sl] = (k_nope[:, sl] + k_rot).astype(bf16)


def _swa_kernel(sink_ref, q_ref, kc_ref, kp_ref, vc_ref, vp_ref, qpos_ref, kposc_ref, kposp_ref,
                o_ref):
    bf16 = jnp.bfloat16
    i = pl.program_id(1)
    row = lax.broadcasted_iota(jnp.int32, (BLOCK, 2 * BLOCK), 0)
    col = lax.broadcasted_iota(jnp.int32, (BLOCK, 2 * BLOCK), 1)
    diff = row + BLOCK - col
    band = (diff >= 0) & (diff < WINDOW)
    lane = lax.broadcasted_iota(jnp.int32, (BLOCK, LANES), 1)
    low_half = lane < HALF
    neg_inf = jnp.float32(-jnp.inf)

    for blk in range(SWA_TQ // BLOCK):
        r0 = blk * BLOCK
        if blk == 0:
            kband = jnp.concatenate([kp_ref[0], kc_ref[0, 0:BLOCK, :]], axis=0)
            vband = jnp.concatenate([vp_ref[0], vc_ref[0, 0:BLOCK, :]], axis=0)
            kpos = jnp.concatenate([kposp_ref[0], kposc_ref[0, :, 0:BLOCK]], axis=1)
            mask = band & ((col >= BLOCK) | (i > 0))
        else:
            kband = kc_ref[0, r0 - BLOCK:r0 + BLOCK, :]
            vband = vc_ref[0, r0 - BLOCK:r0 + BLOCK, :]
            kpos = kposc_ref[0, :, r0 - BLOCK:r0 + BLOCK]
            mask = band
        qpos = qpos_ref[0, r0:r0 + BLOCK, :]
        dist = jnp.abs(qpos - kpos).astype(jnp.float32)
        for pair in range(N_HEADS_A // 2):
            kv = (2 * pair) // (N_HEADS_A // N_KV_A)
            qp = q_ref[0, r0:r0 + BLOCK, pair * LANES:(pair + 1) * LANES]
            kx = kband[:, kv * LANES:(kv + 1) * LANES]
            vx = vband[:, kv * LANES:(kv + 1) * LANES]
            outs = []
            for e in range(2):
                h = 2 * pair + e
                keep = low_half if e == 0 else jnp.logical_not(low_half)
                qm = jnp.where(keep, qp, jnp.zeros_like(qp))
                s = _dot_nt(qm, kx) - ALIBI_SLOPES[h] * dist
                s = jnp.where(mask, s, neg_inf)
                sink = sink_ref[h]
                m = jnp.maximum(jnp.max(s, axis=-1, keepdims=True), sink)
                ex = jnp.exp(s - m)
                denom = jnp.sum(ex, axis=-1, keepdims=True) + jnp.exp(sink - m)
                outs.append(_dot(ex.astype(bf16), vx) / denom)
            o_ref[0, r0:r0 + BLOCK, pair * LANES:(pair + 1) * LANES] = (
                jnp.where(low_half, outs[0], outs[1]).astype(bf16))


def _mla_kernel(q_ref, k_ref, vt_ref, o_ref, s_ref, acc_ref, m_ref):
    bf16 = jnp.bfloat16
    bq = MLA_BQ
    ch = MLA_CHUNK
    qi = pl.program_id(2)
    krow = lax.broadcasted_iota(jnp.int32, (bq, ch), 0)
    qcol = lax.broadcasted_iota(jnp.int32, (bq, ch), 1)
    neg_inf = jnp.float32(-jnp.inf)
    m_ref[...] = jnp.full(m_ref.shape, neg_inf, jnp.float32)
    acc_ref[...] = jnp.zeros(acc_ref.shape, jnp.float32)

    def scores_head(buf, kb, e):
        start = pl.multiple_of(kb * bq, bq)
        sl = slice(e * LANES, (e + 1) * LANES)
        s_ref[buf, e] = _dot_nt(k_ref[0, pl.ds(start, bq), sl], q_ref[0, :, sl])

    def scores_into(buf, kb):
        for e in range(MLA_HEADS):
            scores_head(buf, kb, e)

    def softmax_pv_head(buf, kb, e, masked):
        vt = vt_ref[0, kb, e * LANES:(e + 1) * LANES, :]
        for c in range(bq // ch):
            cs = slice(c * ch, (c + 1) * ch)
            s = s_ref[buf, e, :, cs]
            if masked:
                s = jnp.where(krow <= qcol + c * ch, s, neg_inf)
            m_prev = m_ref[e, :, cs]
            m_new = jnp.maximum(m_prev, jnp.max(s, axis=0, keepdims=True))
            p = jnp.exp2(s - m_new).astype(bf16)
            alpha = jnp.exp2(m_prev - m_new)
            acc_ref[e, :, cs] = alpha * acc_ref[e, :, cs] + _dot(vt, p)
            m_ref[e, :, cs] = m_new

    def softmax_pv(buf, kb, masked):
        for e in range(MLA_HEADS):
            softmax_pv_head(buf, kb, e, masked)

    def half(src, dst, kb):
        for e in range(min(MLA_LEAD, MLA_HEADS)):
            scores_head(dst, kb + 1, e)
        for e in range(MLA_HEADS):
            softmax_pv_head(src, kb, e, False)
            if e + MLA_LEAD < MLA_HEADS:
                scores_head(dst, kb + 1, e + MLA_LEAD)

    def pair_body(t, carry):
        half(0, 1, 2 * t)
        half(1, 0, 2 * t + 1)
        return carry

    scores_into(0, 0)
    lax.fori_loop(0, qi // 2, pair_body, 0)

    @pl.when(qi % 2 == 1)
    def _():
        half(0, 1, qi - 1)
        softmax_pv(1, qi, True)

    @pl.when(qi % 2 == 0)
    def _():
        softmax_pv(0, qi, True)

    outs = []
    for e in range(MLA_HEADS):
        acc = acc_ref[e]
        outs.append(acc[0:V_DIM_B] / acc[V_DIM_B:V_DIM_B + 1])
    o_ref[0] = jnp.concatenate(outs, axis=0).T.astype(bf16)


def _merge_kernel(x_ref, oa_ref, ob_ref, gpre_ref, wg_ref, woa_ref, wob_ref, wout_ref, gpost_ref,
                  gpre2_ref, wup_ref, wdn_ref, gpost2_ref, o_ref):
    bf16 = jnp.bfloat16
    x = x_ref[...]
    hb = (_rms(x) * gpre_ref[...]).astype(bf16)
    gate_a = jax.nn.sigmoid(_dot(hb, wg_ref[:, 0:D_MODEL]))
    gate_b = jax.nn.sigmoid(_dot(hb, wg_ref[:, D_MODEL:2 * D_MODEL]))
    merged = gate_a * _dot(oa_ref[...], woa_ref[...]) + gate_b * _dot(ob_ref[...], wob_ref[...])
    y = _dot(merged.astype(bf16), wout_ref[...])
    x1 = x + _rms(y) * gpost_ref[...]
    h2 = (_rms(x1) * gpre2_ref[...]).astype(bf16)
    y2 = jnp.zeros(x.shape, jnp.float32)
    for c in range(D_FF // FF_CHUNK):
        cs = slice(c * FF_CHUNK, (c + 1) * FF_CHUNK)
        up = jnp.maximum(_dot(h2, wup_ref[:, cs]), 0.0)
        y2 = y2 + _dot((up * up).astype(bf16), wdn_ref[cs, :])
    o_ref[...] = x1 + _rms(y2) * gpost2_ref[...]


def _const_spec(shape):
    return pl.BlockSpec(shape, lambda *_: (0,) * len(shape), pipeline_mode=pl.Buffered(1))


def kernel(x, positions, pre_norm_mix, w_in, q_a_norm, w_q_b, kv_a_norm, w_kv_b, sinks, w_o_a,
           w_o_b, w_out, post_norm_mix, pre_norm_mlp, w_up, w_down, post_norm_mlp):
    f32, bf16 = jnp.float32, jnp.bfloat16
    B, S, D = x.shape
    T = B * S
    depth = w_in.shape[0]
    for l in range(depth):
        wi = w_in[l]
        o_ga, o_qa = 0, 2 * D_MODEL
        o_ka = o_qa + WIDTH_A
        o_va = o_ka + N_KV_A * HEAD_DIM_A
        o_cq = o_va + N_KV_A * HEAD_DIM_A
        o_ckv = o_cq + Q_LORA
        o_kr = o_ckv + KV_LORA
        ka = wi[:, o_ka:o_va]
        va = wi[:, o_va:o_cq]
        kr = wi[:, o_kr:o_kr + QK_ROPE]
        hr = QK_ROPE // 2
        z = lambda n: jnp.zeros((D_MODEL, n), f32)
        w1 = jnp.concatenate([
            wi[:, o_qa:o_ka] * SCALE_A,
            ka[:, :HEAD_DIM_A], ka[:, :HEAD_DIM_A], ka[:, HEAD_DIM_A:], ka[:, HEAD_DIM_A:],
            va[:, :HEAD_DIM_A], va[:, :HEAD_DIM_A], va[:, HEAD_DIM_A:], va[:, HEAD_DIM_A:],
            wi[:, o_cq:o_ckv], wi[:, o_ckv:o_kr],
            z(QK_NOPE), kr, z(LANES - Q_HEAD_B),
            z(QK_NOPE), kr[:, hr:], kr[:, :hr], z(LANES - Q_HEAD_B),
        ], axis=1).astype(bf16)
        w_gates = wi[:, o_ga:o_qa].astype(bf16)

        wq = w_q_b[l].reshape(Q_LORA, N_HEADS_B, Q_HEAD_B)
        q_nope, q_rope = wq[..., :QK_NOPE], wq[..., QK_NOPE:]
        zq = lambda n: jnp.zeros((Q_LORA, N_HEADS_B, n), f32)
        wq_main = jnp.concatenate([q_nope, q_rope, zq(LANES - Q_HEAD_B)], -1)
        wq_swap = jnp.concatenate([zq(QK_NOPE), q_rope[..., hr:], q_rope[..., :hr],
                                   zq(LANES - Q_HEAD_B)], -1)
        wq_main = wq_main.reshape(Q_LORA, N_HEADS_B * LANES).astype(bf16)
        wq_swap = wq_swap.reshape(Q_LORA, N_HEADS_B * LANES).astype(bf16)

        wkv = w_kv_b[l].reshape(KV_LORA, N_HEADS_B, KV_HEAD_B)
        kv_k, kv_v = wkv[..., :QK_NOPE], wkv[..., QK_NOPE:]
        zk = jnp.zeros((KV_LORA, N_HEADS_B, HALF), f32)
        wkv_k = jnp.concatenate([kv_k, zk], -1).reshape(KV_LORA, N_HEADS_B * LANES).astype(bf16)
        wkv_vt = jnp.concatenate([kv_v, zk], -1).reshape(KV_LORA, N_HEADS_B * LANES).T.astype(bf16)

        freqs = ROPE_THETA ** (-jnp.arange(0, QK_ROPE, 2, dtype=f32) / QK_ROPE)
        zf = jnp.zeros((QK_NOPE,), f32)
        zt = jnp.zeros((LANES - Q_HEAD_B,), f32)
        freq_slab = jnp.concatenate([zf, freqs, freqs, zt])[None, :]
        sgn_slab = jnp.concatenate([zf, -jnp.ones((hr,), f32), jnp.ones((hr,), f32), zt])[None, :]

        row = lambda g: g.reshape(1, -1).astype(f32)
        x2 = x.reshape(T, D)
        pos_col = positions.reshape(T, 1)

        tm = PROJ_TM
        assert tm == MLA_BQ and S % tm == 0
        nkb = S // tm
        tok = lambda w: pl.BlockSpec((tm, w), lambda i: (i, 0))
        qa, ka_x, va_x, qb, kb, vt = pl.pallas_call(
            _proj_kernel,
            grid=(T // tm,),
            in_specs=[tok(D), tok(1), _const_spec((1, D)), _const_spec((D, W1_COLS)),
                      _const_spec((1, Q_LORA)), _const_spec((Q_LORA, N_HEADS_B * LANES)),
                      _const_spec((Q_LORA, N_HEADS_B * LANES)), _const_spec((1, KV_LORA)),
                      _const_spec((KV_LORA, N_HEADS_B * LANES)),
                      _const_spec((N_HEADS_B * LANES, KV_LORA)),
                      _const_spec((1, LANES)), _const_spec((1, LANES))],
            out_specs=[tok(WIDTH_A), tok(2 * LANES), tok(2 * LANES), tok(N_HEADS_B * LANES),
                       tok(N_HEADS_B * LANES),
                       pl.BlockSpec((1, 1, N_HEADS_B * LANES, tm),
                                    lambda i: (i // nkb, i % nkb, 0, 0))],
            out_shape=[jax.ShapeDtypeStruct((T, WIDTH_A), bf16),
                       jax.ShapeDtypeStruct((T, 2 * LANES), bf16),
                       jax.ShapeDtypeStruct((T, 2 * LANES), bf16),
                       jax.ShapeDtypeStruct((T, N_HEADS_B * LANES), bf16),
                       jax.ShapeDtypeStruct((T, N_HEADS_B * LANES), bf16),
                       jax.ShapeDtypeStruct((B, nkb, N_HEADS_B * LANES, tm), bf16)],
            compiler_params=pltpu.CompilerParams(dimension_semantics=("arbitrary",),
                                                 vmem_limit_bytes=VMEM_LIMIT),
            name="proj",
        )(x2, pos_col, row(pre_norm_mix[l]), w1, row(q_a_norm[l]), wq_main, wq_swap,
          row(kv_a_norm[l]), wkv_k, wkv_vt, freq_slab, sgn_slab)

        tq = SWA_TQ
        nb_per = tq // BLOCK
        qa3 = qa.reshape(B, S, WIDTH_A)
        ka3 = ka_x.reshape(B, S, 2 * LANES)
        va3 = va_x.reshape(B, S, 2 * LANES)
        pos_c3 = positions.reshape(B, S, 1)
        pos_r3 = positions.reshape(B, 1, S)
        cur = lambda w: pl.BlockSpec((1, tq, w), lambda b, i: (b, i, 0))
        prev = lambda w: pl.BlockSpec((1, BLOCK, w),
                                      lambda b, i: (b, jnp.maximum(i * nb_per - 1, 0), 0))
        out_a = pl.pallas_call(
            _swa_kernel,
            grid=(B, S // tq),
            in_specs=[pl.BlockSpec(memory_space=pltpu.SMEM),
                      cur(WIDTH_A), cur(2 * LANES), prev(2 * LANES), cur(2 * LANES),
                      prev(2 * LANES), cur(1),
                      pl.BlockSpec((1, 1, tq), lambda b, i: (b, 0, i)),
                      pl.BlockSpec((1, 1, BLOCK),
                                   lambda b, i: (b, 0, jnp.maximum(i * nb_per - 1, 0)))],
            out_specs=cur(WIDTH_A),
            out_shape=jax.ShapeDtypeStruct((B, S, WIDTH_A), bf16),
            compiler_params=pltpu.CompilerParams(dimension_semantics=("arbitrary", "arbitrary"),
                                                 vmem_limit_bytes=VMEM_LIMIT),
            name="swa",
        )(sinks[l].astype(f32), qa3, ka3, ka3, va3, va3, pos_c3, pos_r3, pos_r3)

        bq, nh = MLA_BQ, MLA_HEADS
        qb3 = qb.reshape(B, S, N_HEADS_B * LANES)
        kb3 = kb.reshape(B, S, N_HEADS_B * LANES)
        out_b = pl.pallas_call(
            _mla_kernel,
            grid=(B, N_HEADS_B // nh, S // bq),
            in_specs=[pl.BlockSpec((1, bq, nh * LANES), lambda b, j, i: (b, i, j)),
                      pl.BlockSpec((1, S, nh * LANES), lambda b, j, i: (b, 0, j)),
                      pl.BlockSpec((1, nkb, nh * LANES, bq), lambda b, j, i: (b, 0, j, 0))],
            out_specs=pl.BlockSpec((1, bq, nh * V_DIM_B), lambda b, j, i: (b, i, j)),
            out_shape=jax.ShapeDtypeStruct((B, S, WIDTH_B), bf16),
            scratch_shapes=[pltpu.VMEM((2, nh, bq, bq), f32),
                            pltpu.VMEM((nh, LANES, bq), f32), pltpu.VMEM((nh, 1, bq), f32)],
            compiler_params=pltpu.CompilerParams(
                dimension_semantics=("arbitrary", "arbitrary", "arbitrary"),
                vmem_limit_bytes=VMEM_LIMIT),
            name="mla",
        )(qb3, kb3, vt)

        tm = MERGE_TM
        tok = lambda w: pl.BlockSpec((tm, w), lambda i: (i, 0))
        x2 = pl.pallas_call(
            _merge_kernel,
            grid=(T // tm,),
            in_specs=[tok(D), tok(WIDTH_A), tok(WIDTH_B), _const_spec((1, D)),
                      _const_spec((D, 2 * D)), _const_spec((WIDTH_A, D)), _const_spec((WIDTH_B, D)),
                      _const_spec((D, D)), _const_spec((1, D)), _const_spec((1, D)),
                      _const_spec((D, D_FF)), _const_spec((D_FF, D)), _const_spec((1, D))],
            out_specs=tok(D),
            out_shape=jax.ShapeDtypeStruct((T, D), f32),
            compiler_params=pltpu.CompilerParams(dimension_semantics=("arbitrary",),
                                                 vmem_limit_bytes=VMEM_LIMIT),
            name="merge_mlp",
        )(x2, out_a.reshape(T, WIDTH_A), out_b.reshape(T, WIDTH_B), row(pre_norm_mix[l]), w_gates,
          w_o_a[l].astype(bf16), w_o_b[l].astype(bf16), w_out[l].astype(bf16),
          row(post_norm_mix[l]), row(pre_norm_mlp[l]), w_up[l].astype(bf16),
          w_down[l].astype(bf16), row(post_norm_mlp[l]))
        x = x2.reshape(B, S, D)
    return x
```

```python
import functools

import jax
import jax.numpy as jnp
from jax import lax
from jax.experimental import pallas as pl
from jax.experimental.pallas import tpu as pltpu

D_MODEL = 1024
N_HEADS_A = 8
N_KV_A = 2
HEAD_DIM_A = 64
WINDOW = 128
BLOCK = 128
N_HEADS_B = 8
QK_NOPE = 64
QK_ROPE = 32
V_DIM_B = 64
Q_LORA = 256
KV_LORA = 128
ROPE_THETA = 10000.0
D_FF = 4 * D_MODEL
EPS = 1e-6

WIDTH_A = N_HEADS_A * HEAD_DIM_A
WIDTH_B = N_HEADS_B * V_DIM_B
Q_HEAD_B = QK_NOPE + QK_ROPE
KV_HEAD_B = QK_NOPE + V_DIM_B

LANES = 128
HALF = LANES // 2
SCALE_A = HEAD_DIM_A ** -0.5
SCALE_B = Q_HEAD_B ** -0.5
LOG2_E = 1.4426950408889634
QSCALE_B = SCALE_B * LOG2_E
ALIBI_SLOPES = tuple(2.0 ** (-8.0 * (h + 1) / N_HEADS_A) for h in range(N_HEADS_A))

W1_QA = 0
W1_KA = W1_QA + WIDTH_A
W1_VA = W1_KA + 2 * LANES
W1_CQ = W1_VA + 2 * LANES
W1_CKV = W1_CQ + Q_LORA
W1_KR = W1_CKV + KV_LORA
W1_KRS = W1_KR + LANES
W1_COLS = W1_KRS + LANES

PROJ_TM = 512
SWA_TQ = 512
MLA_BQ = 512
MLA_CHUNK = 256
MLA_HEADS = 4
MLA_LEAD = 2
MERGE_TM = 512
FF_CHUNK = 1024
VMEM_LIMIT = 60 * 1024 * 1024


def _rms(v):
    return v * lax.rsqrt(jnp.mean(v * v, axis=-1, keepdims=True) + EPS)


def _dot(a, b):
    return jnp.dot(a, b, preferred_element_type=jnp.float32)


def _dot_nt(a, b):
    return lax.dot_general(a, b, (((1,), (1,)), ((), ())), preferred_element_type=jnp.float32)


def _proj_kernel(x_ref, pos_ref, gpre_ref, w1_ref, gq_ref, wqm_ref, wqs_ref, gkv_ref, wkk_ref,
                 wkvt_ref, freq_ref, sgn_ref,
                 qa_ref, ka_ref, va_ref, qb_ref, kb_ref, vt_ref):
    bf16 = jnp.bfloat16
    hb = (_rms(x_ref[...]) * gpre_ref[...]).astype(bf16)
    proj = _dot(hb, w1_ref[...])
    qa_ref[...] = proj[:, W1_QA:W1_KA].astype(bf16)
    ka_ref[...] = proj[:, W1_KA:W1_VA].astype(bf16)
    va_ref[...] = proj[:, W1_VA:W1_CQ].astype(bf16)
    cq = proj[:, W1_CQ:W1_CKV]
    ckv = proj[:, W1_CKV:W1_KR]
    kr_main = proj[:, W1_KR:W1_KRS]
    kr_swap = proj[:, W1_KRS:W1_COLS]

    ang = pos_ref[...].astype(jnp.float32) * freq_ref[...]
    cos = jnp.cos(ang)
    sin = jnp.sin(ang) * sgn_ref[...]

    cqn = (_rms(cq) * gq_ref[...]).astype(bf16)
    q_main = _dot(cqn, wqm_ref[...])
    q_swap = _dot(cqn, wqs_ref[...])
    ckvn = (_rms(ckv) * gkv_ref[...]).astype(bf16)
    k_nope = _dot(ckvn, wkk_ref[...])
    v_t = _dot_nt(wkvt_ref[...], ckvn)
    vrow = lax.broadcasted_iota(jnp.int32, v_t.shape, 0)
    vt_ref[0, 0] = jnp.where(vrow % LANES == V_DIM_B, 1.0, v_t).astype(bf16)
    k_rot = kr_main * cos + kr_swap * sin
    for h in range(N_HEADS_B):
        sl = slice(h * LANES, (h + 1) * LANES)
        qb_ref[:, sl] = ((q_main[:, sl] * cos + q_swap[:, sl] * sin) * QSCALE_B).astype(bf16)
        kb_ref[:, sl] = (k_nope[:, sl] + k_rot).astype(bf16)


def _swa_kernel(sink_ref, q_ref, kc_ref, kp_ref, vc_ref, vp_ref, qpos_ref, kposc_ref, kposp_ref,
                o_ref):
    bf16 = jnp.bfloat16
    i = pl.program_id(1)
    row = lax.broadcasted_iota(jnp.int32, (BLOCK, 2 * BLOCK), 0)
    col = lax.broadcasted_iota(jnp.int32, (BLOCK, 2 * BLOCK), 1)
    diff = row + BLOCK - col
    band = (diff >= 0) & (diff < WINDOW)
    lane = lax.broadcasted_iota(jnp.int32, (BLOCK, LANES), 1)
    low_half = lane < HALF
    neg_inf = jnp.float32(-jnp.inf)

    for blk in range(SWA_TQ // BLOCK):
        r0 = blk * BLOCK
        if blk == 0:
            kband = jnp.concatenate([kp_ref[0], kc_ref[0, 0:BLOCK, :]], axis=0)
            vband = jnp.concatenate([vp_ref[0], vc_ref[0, 0:BLOCK, :]], axis=0)
            kpos = jnp.concatenate([kposp_ref[0], kposc_ref[0, :, 0:BLOCK]], axis=1)
            mask = band & ((col >= BLOCK) | (i > 0))
        else:
            kband = kc_ref[0, r0 - BLOCK:r0 + BLOCK, :]
            vband = vc_ref[0, r0 - BLOCK:r0 + BLOCK, :]
            kpos = kposc_ref[0, :, r0 - BLOCK:r0 + BLOCK]
            mask = band
        qpos = qpos_ref[0, r0:r0 + BLOCK, :]
        dist = jnp.abs(qpos - kpos).astype(jnp.float32)
        for pair in range(N_HEADS_A // 2):
            kv = (2 * pair) // (N_HEADS_A // N_KV_A)
            qp = q_ref[0, r0:r0 + BLOCK, pair * LANES:(pair + 1) * LANES]
            kx = kband[:, kv * LANES:(kv + 1) * LANES]
            vx = vband[:, kv * LANES:(kv + 1) * LANES]
            outs = []
            for e in range(2):
                h = 2 * pair + e
                keep = low_half if e == 0 else jnp.logical_not(low_half)
                qm = jnp.where(keep, qp, jnp.zeros_like(qp))
                s = _dot_nt(qm, kx) - ALIBI_SLOPES[h] * dist
                s = jnp.where(mask, s, neg_inf)
                sink = sink_ref[h]
                m = jnp.maximum(jnp.max(s, axis=-1, keepdims=True), sink)
                ex = jnp.exp(s - m)
                denom = jnp.sum(ex, axis=-1, keepdims=True) + jnp.exp(sink - m)
                outs.append(_dot(ex.astype(bf16), vx) / denom)
            o_ref[0, r0:r0 + BLOCK, pair * LANES:(pair + 1) * LANES] = (
                jnp.where(low_half, outs[0], outs[1]).astype(bf16))


def _mla_kernel(q_ref, k_ref, vt_ref, o_ref, s_ref, bmax_ref, acc_ref, m_ref):
    bf16 = jnp.bfloat16
    bq = MLA_BQ
    ch = MLA_CHUNK
    qi = pl.program_id(2)
    krow = lax.broadcasted_iota(jnp.int32, (bq, ch), 0)
    qcol = lax.broadcasted_iota(jnp.int32, (bq, ch), 1)
    neg_inf = jnp.float32(-jnp.inf)
    m_ref[...] = jnp.full(m_ref.shape, neg_inf, jnp.float32)
    acc_ref[...] = jnp.zeros(acc_ref.shape, jnp.float32)

    def scores_head(buf, kb, e):
        start = pl.multiple_of(kb * bq, bq)
        sl = slice(e * LANES, (e + 1) * LANES)
        s = _dot_nt(k_ref[0, pl.ds(start, bq), sl], q_ref[0, :, sl])
        s_ref[buf, e] = s
        bmax_ref[buf, e] = jnp.max(s, axis=0, keepdims=True)

    def scores_into(buf, kb):
        for e in range(MLA_HEADS):
            scores_head(buf, kb, e)

    def softmax_pv_head(buf, kb, e, masked):
        vt = vt_ref[0, kb, e * LANES:(e + 1) * LANES, :]
        for c in range(bq // ch):
            cs = slice(c * ch, (c + 1) * ch)
            s = s_ref[buf, e, :, cs]
            if masked:
                s = jnp.where(krow <= qcol + c * ch, s, neg_inf)
                blockmax = jnp.max(s, axis=0, keepdims=True)
            else:
                blockmax = bmax_ref[buf, e, :, cs]
            m_prev = m_ref[e, :, cs]
            m_new = jnp.maximum(m_prev, blockmax)
            p = jnp.exp2(s - m_new).astype(bf16)
            alpha = jnp.exp2(m_prev - m_new)
            acc_ref[e, :, cs] = alpha * acc_ref[e, :, cs] + _dot(vt, p)
            m_ref[e, :, cs] = m_new

    def softmax_pv(buf, kb, masked):
        for e in range(MLA_HEADS):
            softmax_pv_head(buf, kb, e, masked)

    def half(src, dst, kb):
        for e in range(min(MLA_LEAD, MLA_HEADS)):
            scores_head(dst, kb + 1, e)
        for e in range(MLA_HEADS):
            softmax_pv_head(src, kb, e, False)
            if e + MLA_LEAD < MLA_HEADS:
                scores_head(dst, kb + 1, e + MLA_LEAD)

    def pair_body(t, carry):
        half(0, 1, 2 * t)
        half(1, 0, 2 * t + 1)
        return carry

    scores_into(0, 0)
    lax.fori_loop(0, qi // 2, pair_body, 0)

    @pl.when(qi % 2 == 1)
    def _():
        half(0, 1, qi - 1)
        softmax_pv(1, qi, True)

    @pl.when(qi % 2 == 0)
    def _():
        softmax_pv(0, qi, True)

    outs = []
    for e in range(MLA_HEADS):
        acc = acc_ref[e]
        outs.append(acc[0:V_DIM_B] / acc[V_DIM_B:V_DIM_B + 1])
    o_ref[0] = jnp.concatenate(outs, axis=0).T.astype(bf16)


def _merge_kernel(x_ref, oa_ref, ob_ref, gpre_ref, wg_ref, woa_ref, wob_ref, wout_ref, gpost_ref,
                  gpre2_ref, wup_ref, wdn_ref, gpost2_ref, o_ref):
    bf16 = jnp.bfloat16
    x = x_ref[...]
    hb = (_rms(x) * gpre_ref[...]).astype(bf16)
    gate_a = jax.nn.sigmoid(_dot(hb, wg_ref[:, 0:D_MODEL]))
    gate_b = jax.nn.sigmoid(_dot(hb, wg_ref[:, D_MODEL:2 * D_MODEL]))
    merged = gate_a * _dot(oa_ref[...], woa_ref[...]) + gate_b * _dot(ob_ref[...], wob_ref[...])
    y = _dot(merged.astype(bf16), wout_ref[...])
    x1 = x + _rms(y) * gpost_ref[...]
    h2 = (_rms(x1) * gpre2_ref[...]).astype(bf16)
    y2 = jnp.zeros(x.shape, jnp.float32)
    for c in range(D_FF // FF_CHUNK):
        cs = slice(c * FF_CHUNK, (c + 1) * FF_CHUNK)
        up = jnp.maximum(_dot(h2, wup_ref[:, cs]), 0.0)
        y2 = y2 + _dot((up * up).astype(bf16), wdn_ref[cs, :])
    o_ref[...] = x1 + _rms(y2) * gpost2_ref[...]


def _const_spec(shape):
    return pl.BlockSpec(shape, lambda *_: (0,) * len(shape), pipeline_mode=pl.Buffered(1))


def kernel(x, positions, pre_norm_mix, w_in, q_a_norm, w_q_b, kv_a_norm, w_kv_b, sinks, w_o_a,
           w_o_b, w_out, post_norm_mix, pre_norm_mlp, w_up, w_down, post_norm_mlp):
    f32, bf16 = jnp.float32, jnp.bfloat16
    B, S, D = x.shape
    T = B * S
    depth = w_in.shape[0]
    for l in range(depth):
        wi = w_in[l]
        o_ga, o_qa = 0, 2 * D_MODEL
        o_ka = o_qa + WIDTH_A
        o_va = o_ka + N_KV_A * HEAD_DIM_A
        o_cq = o_va + N_KV_A * HEAD_DIM_A
        o_ckv = o_cq + Q_LORA
        o_kr = o_ckv + KV_LORA
        ka = wi[:, o_ka:o_va]
        va = wi[:, o_va:o_cq]
        kr = wi[:, o_kr:o_kr + QK_ROPE]
        hr = QK_ROPE // 2
        z = lambda n: jnp.zeros((D_MODEL, n), f32)
        w1 = jnp.concatenate([
            wi[:, o_qa:o_ka] * SCALE_A,
            ka[:, :HEAD_DIM_A], ka[:, :HEAD_DIM_A], ka[:, HEAD_DIM_A:], ka[:, HEAD_DIM_A:],
            va[:, :HEAD_DIM_A], va[:, :HEAD_DIM_A], va[:, HEAD_DIM_A:], va[:, HEAD_DIM_A:],
            wi[:, o_cq:o_ckv], wi[:, o_ckv:o_kr],
            z(QK_NOPE), kr, z(LANES - Q_HEAD_B),
            z(QK_NOPE), kr[:, hr:], kr[:, :hr], z(LANES - Q_HEAD_B),
        ], axis=1).astype(bf16)
        w_gates = wi[:, o_ga:o_qa].astype(bf16)

        wq = w_q_b[l].reshape(Q_LORA, N_HEADS_B, Q_HEAD_B)
        q_nope, q_rope = wq[..., :QK_NOPE], wq[..., QK_NOPE:]
        zq = lambda n: jnp.zeros((Q_LORA, N_HEADS_B, n), f32)
        wq_main = jnp.concatenate([q_nope, q_rope, zq(LANES - Q_HEAD_B)], -1)
        wq_swap = jnp.concatenate([zq(QK_NOPE), q_rope[..., hr:], q_rope[..., :hr],
                                   zq(LANES - Q_HEAD_B)], -1)
        wq_main = wq_main.reshape(Q_LORA, N_HEADS_B * LANES).astype(bf16)
        wq_swap = wq_swap.reshape(Q_LORA, N_HEADS_B * LANES).astype(bf16)

        wkv = w_kv_b[l].reshape(KV_LORA, N_HEADS_B, KV_HEAD_B)
        kv_k, kv_v = wkv[..., :QK_NOPE], wkv[..., QK_NOPE:]
        zk = jnp.zeros((KV_LORA, N_HEADS_B, HALF), f32)
        wkv_k = jnp.concatenate([kv_k, zk], -1).reshape(KV_LORA, N_HEADS_B * LANES).astype(bf16)
        wkv_vt = jnp.concatenate([kv_v, zk], -1).reshape(KV_LORA, N_HEADS_B * LANES).T.astype(bf16)

        freqs = ROPE_THETA ** (-jnp.arange(0, QK_ROPE, 2, dtype=f32) / QK_ROPE)
        zf = jnp.zeros((QK_NOPE,), f32)
        zt = jnp.zeros((LANES - Q_HEAD_B,), f32)
        freq_slab = jnp.concatenate([zf, freqs, freqs, zt])[None, :]
        sgn_slab = jnp.concatenate([zf, -jnp.ones((hr,), f32), jnp.ones((hr,), f32), zt])[None, :]

        row = lambda g: g.reshape(1, -1).astype(f32)
        x2 = x.reshape(T, D)
        pos_col = positions.reshape(T, 1)

        tm = PROJ_TM
        assert tm == MLA_BQ and S % tm == 0
        nkb = S // tm
        tok = lambda w: pl.BlockSpec((tm, w), lambda i: (i, 0))
        qa, ka_x, va_x, qb, kb, vt = pl.pallas_call(
            _proj_kernel,
            grid=(T // tm,),
            in_specs=[tok(D), tok(1), _const_spec((1, D)), _const_spec((D, W1_COLS)),
                      _const_spec((1, Q_LORA)), _const_spec((Q_LORA, N_HEADS_B * LANES)),
                      _const_spec((Q_LORA, N_HEADS_B * LANES)), _const_spec((1, KV_LORA)),
                      _const_spec((KV_LORA, N_HEADS_B * LANES)),
                      _const_spec((N_HEADS_B * LANES, KV_LORA)),
                      _const_spec((1, LANES)), _const_spec((1, LANES))],
            out_specs=[tok(WIDTH_A), tok(2 * LANES), tok(2 * LANES), tok(N_HEADS_B * LANES),
                       tok(N_HEADS_B * LANES),
                       pl.BlockSpec((1, 1, N_HEADS_B * LANES, tm),
                                    lambda i: (i // nkb, i % nkb, 0, 0))],
            out_shape=[jax.ShapeDtypeStruct((T, WIDTH_A), bf16),
                       jax.ShapeDtypeStruct((T, 2 * LANES), bf16),
                       jax.ShapeDtypeStruct((T, 2 * LANES), bf16),
                       jax.ShapeDtypeStruct((T, N_HEADS_B * LANES), bf16),
                       jax.ShapeDtypeStruct((T, N_HEADS_B * LANES), bf16),
                       jax.ShapeDtypeStruct((B, nkb, N_HEADS_B * LANES, tm), bf16)],
            compiler_params=pltpu.CompilerParams(dimension_semantics=("arbitrary",),
                                                 vmem_limit_bytes=VMEM_LIMIT),
            name="proj",
        )(x2, pos_col, row(pre_norm_mix[l]), w1, row(q_a_norm[l]), wq_main, wq_swap,
          row(kv_a_norm[l]), wkv_k, wkv_vt, freq_slab, sgn_slab)

        tq = SWA_TQ
        nb_per = tq // BLOCK
        qa3 = qa.reshape(B, S, WIDTH_A)
        ka3 = ka_x.reshape(B, S, 2 * LANES)
        va3 = va_x.reshape(B, S, 2 * LANES)
        pos_c3 = positions.reshape(B, S, 1)
        pos_r3 = positions.reshape(B, 1, S)
        cur = lambda w: pl.BlockSpec((1, tq, w), lambda b, i: (b, i, 0))
        prev = lambda w: pl.BlockSpec((1, BLOCK, w),
                                      lambda b, i: (b, jnp.maximum(i * nb_per - 1, 0), 0))
        out_a = pl.pallas_call(
            _swa_kernel,
            grid=(B, S // tq),
            in_specs=[pl.BlockSpec(memory_space=pltpu.SMEM),
                      cur(WIDTH_A), cur(2 * LANES), prev(2 * LANES), cur(2 * LANES),
                      prev(2 * LANES), cur(1),
                      pl.BlockSpec((1, 1, tq), lambda b, i: (b, 0, i)),
                      pl.BlockSpec((1, 1, BLOCK),
                                   lambda b, i: (b, 0, jnp.maximum(i * nb_per - 1, 0)))],
            out_specs=cur(WIDTH_A),
            out_shape=jax.ShapeDtypeStruct((B, S, WIDTH_A), bf16),
            compiler_params=pltpu.CompilerParams(dimension_semantics=("arbitrary", "arbitrary"),
                                                 vmem_limit_bytes=VMEM_LIMIT),
            name="swa",
        )(sinks[l].astype(f32), qa3, ka3, ka3, va3, va3, pos_c3, pos_r3, pos_r3)

        bq, nh = MLA_BQ, MLA_HEADS
        qb3 = qb.reshape(B, S, N_HEADS_B * LANES)
        kb3 = kb.reshape(B, S, N_HEADS_B * LANES)
        out_b = pl.pallas_call(
            _mla_kernel,
            grid=(B, N_HEADS_B // nh, S // bq),
            in_specs=[pl.BlockSpec((1, bq, nh * LANES), lambda b, j, i: (b, i, j)),
                      pl.BlockSpec((1, S, nh * LANES), lambda b, j, i: (b, 0, j)),
                      pl.BlockSpec((1, nkb, nh * LANES, bq), lambda b, j, i: (b, 0, j, 0))],
            out_specs=pl.BlockSpec((1, bq, nh * V_DIM_B), lambda b, j, i: (b, i, j)),
            out_shape=jax.ShapeDtypeStruct((B, S, WIDTH_B), bf16),
            scratch_shapes=[pltpu.VMEM((2, nh, bq, bq), f32),
                            pltpu.VMEM((2, nh, 1, bq), f32),
                            pltpu.VMEM((nh, LANES, bq), f32), pltpu.VMEM((nh, 1, bq), f32)],
            compiler_params=pltpu.CompilerParams(
                dimension_semantics=("arbitrary", "arbitrary", "arbitrary"),
                vmem_limit_bytes=VMEM_LIMIT),
            name="mla",
        )(qb3, kb3, vt)

        tm = MERGE_TM
        tok = lambda w: pl.BlockSpec((tm, w), lambda i: (i, 0))
        x2 = pl.pallas_call(
            _merge_kernel,
            grid=(T // tm,),
            in_specs=[tok(D), tok(WIDTH_A), tok(WIDTH_B), _const_spec((1, D)),
                      _const_spec((D, 2 * D)), _const_spec((WIDTH_A, D)), _const_spec((WIDTH_B, D)),
                      _const_spec((D, D)), _const_spec((1, D)), _const_spec((1, D)),
                      _const_spec((D, D_FF)), _const_spec((D_FF, D)), _const_spec((1, D))],
            out_specs=tok(D),
            out_shape=jax.ShapeDtypeStruct((T, D), f32),
            compiler_params=pltpu.CompilerParams(dimension_semantics=("arbitrary",),
                                                 vmem_limit_bytes=VMEM_LIMIT),
            name="merge_mlp",
        )(x2, out_a.reshape(T, WIDTH_A), out_b.reshape(T, WIDTH_B), row(pre_norm_mix[l]), w_gates,
          w_o_a[l].astype(bf16), w_o_b[l].astype(bf16), w_out[l].astype(bf16),
          row(post_norm_mix[l]), row(pre_norm_mlp[l]), w_up[l].astype(bf16),
          w_down[l].astype(bf16), row(post_norm_mlp[l]))
        x = x2.reshape(B, S, D)
    return x
```

```python
import functools

import jax
import jax.numpy as jnp
from jax import lax
from jax.experimental import pallas as pl
from jax.experimental.pallas import tpu as pltpu

D_MODEL = 1024
N_HEADS_A = 8
N_KV_A = 2
HEAD_DIM_A = 64
WINDOW = 128
BLOCK = 128
N_HEADS_B = 8
QK_NOPE = 64
QK_ROPE = 32
V_DIM_B = 64
Q_LORA = 256
KV_LORA = 128
ROPE_THETA = 10000.0
D_FF = 4 * D_MODEL
EPS = 1e-6

WIDTH_A = N_HEADS_A * HEAD_DIM_A
WIDTH_B = N_HEADS_B * V_DIM_B
Q_HEAD_B = QK_NOPE + QK_ROPE
KV_HEAD_B = QK_NOPE + V_DIM_B

LANES = 128
HALF = LANES // 2
SCALE_A = HEAD_DIM_A ** -0.5
SCALE_B = Q_HEAD_B ** -0.5
LOG2_E = 1.4426950408889634
QSCALE_A = SCALE_A * LOG2_E
QSCALE_B = SCALE_B * LOG2_E
ALIBI_SLOPES = tuple(2.0 ** (-8.0 * (h + 1) / N_HEADS_A) for h in range(N_HEADS_A))

IN_GATES = 0
IN_QA = 2 * D_MODEL
IN_KA = IN_QA + WIDTH_A
IN_VA = IN_KA + N_KV_A * HEAD_DIM_A
IN_CQ = IN_VA + N_KV_A * HEAD_DIM_A
IN_CKV = IN_CQ + Q_LORA
IN_KR = IN_CKV + KV_LORA
D_IN = IN_KR + QK_ROPE
SM_KA = 0
SM_KR = SM_KA + 2 * LANES
SM_KRS = SM_KR + LANES
SM_ROWS = SM_KRS + LANES

PROJ_TM = 512
SWA_TQ = 512
SWA_LEAD = 4
MLA_BQ = 512
MLA_CHUNK = 256
MLA_HEADS = 4
MLA_LEAD = 2
MERGE_TM = 512
FF_CHUNK = 1024
VMEM_LIMIT = 60 * 1024 * 1024


def _rms(v):
    return v * lax.rsqrt(jnp.mean(v * v, axis=-1, keepdims=True) + EPS)


def _dot(a, b):
    return jnp.dot(a, b, preferred_element_type=jnp.float32)


def _dot_nt(a, b):
    return lax.dot_general(a, b, (((1,), (1,)), ((), ())), preferred_element_type=jnp.float32)


def _proj_kernel(x_ref, pos_ref, gpre_ref, wqa_ref, wcq_ref, wckv_ref, wsm_ref, wvat_ref, gq_ref,
                 wqm_ref, wqs_ref, gkv_ref, wkk_ref, wkvt_ref, freq_ref, sgn_ref,
                 qa_ref, ka_ref, vat_ref, qb_ref, kb_ref, vt_ref):
    bf16 = jnp.bfloat16
    hb = (_rms(x_ref[...]) * gpre_ref[...]).astype(bf16)
    qa_ref[...] = (_dot_nt(hb, wqa_ref[...]) * QSCALE_A).astype(bf16)
    small = _dot_nt(hb, wsm_ref[...])
    ka_ref[...] = small[:, SM_KA:SM_KR].astype(bf16)
    va_t = _dot_nt(wvat_ref[...], hb)
    varow = lax.broadcasted_iota(jnp.int32, va_t.shape, 0)
    vat_ref[0] = jnp.where(varow % LANES == HEAD_DIM_A, 1.0, va_t).astype(bf16)
    kr_main = small[:, SM_KR:SM_KRS]
    kr_swap = small[:, SM_KRS:SM_ROWS]
    cq = _dot_nt(hb, wcq_ref[...])
    ckv = _dot_nt(hb, wckv_ref[...])

    pos = pos_ref[0].astype(jnp.float32)
    ang = jnp.broadcast_to(pos, (LANES, pos.shape[1])).T * freq_ref[...]
    cos = jnp.cos(ang)
    sin = jnp.sin(ang) * sgn_ref[...]

    cqn = (_rms(cq) * gq_ref[...]).astype(bf16)
    q_main = _dot(cqn, wqm_ref[...])
    q_swap = _dot(cqn, wqs_ref[...])
    ckvn = (_rms(ckv) * gkv_ref[...]).astype(bf16)
    k_nope = _dot(ckvn, wkk_ref[...])
    v_t = _dot_nt(wkvt_ref[...], ckvn)
    vrow = lax.broadcasted_iota(jnp.int32, v_t.shape, 0)
    vt_ref[0, 0] = jnp.where(vrow % LANES == V_DIM_B, 1.0, v_t).astype(bf16)
    k_rot = kr_main * cos + kr_swap * sin
    for h in range(N_HEADS_B):
        sl = slice(h * LANES, (h + 1) * LANES)
        qb_ref[:, sl] = ((q_main[:, sl] * cos + q_swap[:, sl] * sin) * QSCALE_B).astype(bf16)
        kb_ref[:, sl] = (k_nope[:, sl] + k_rot).astype(bf16)


def _swa_kernel(sink_ref, q_ref, kc_ref, kp_ref, vtc_ref, vtp_ref, posc_ref, posp_ref, o_ref):
    bf16 = jnp.bfloat16
    i = pl.program_id(1)
    krow = lax.broadcasted_iota(jnp.int32, (2 * BLOCK, BLOCK), 0)
    qcol = lax.broadcasted_iota(jnp.int32, (2 * BLOCK, BLOCK), 1)
    ahead = krow - qcol
    band = (ahead > 0) & (ahead <= WINDOW)
    lane = lax.broadcasted_iota(jnp.int32, (BLOCK, LANES), 1)
    low_half = lane < HALF
    pos_inf = jnp.float32(jnp.inf)

    n_pairs = N_HEADS_A // 2
    chains = [(blk, pair) for blk in range(SWA_TQ // BLOCK) for pair in range(n_pairs)]
    bands = {}

    def band_of(blk):
        if blk not in bands:
            r0 = blk * BLOCK
            if blk == 0:
                kband = jnp.concatenate([kp_ref[0], kc_ref[0, 0:BLOCK, :]], axis=0)
                vtband = jnp.concatenate([vtp_ref[0], vtc_ref[0, :, 0:BLOCK]], axis=1)
                kpos = jnp.concatenate([posp_ref[0], posc_ref[0, :, 0:BLOCK]], axis=1)
                mask = band & ((krow >= BLOCK) | (i > 0))
            else:
                kband = kc_ref[0, r0 - BLOCK:r0 + BLOCK, :]
                vtband = vtc_ref[0, :, r0 - BLOCK:r0 + BLOCK]
                kpos = posc_ref[0, :, r0 - BLOCK:r0 + BLOCK]
                mask = band
            qpos = posc_ref[0, :, r0:r0 + BLOCK]
            kpos_col = jnp.broadcast_to(kpos, (BLOCK, 2 * BLOCK)).T
            dist = jnp.abs(kpos_col - qpos).astype(jnp.float32) * LOG2_E
            dist = jnp.where(mask, dist, pos_inf)
            bands[blk] = (kband, vtband, dist)
        return bands[blk]

    def scores(blk, pair):
        kv = (2 * pair) // (N_HEADS_A // N_KV_A)
        r0 = blk * BLOCK
        qp = q_ref[0, r0:r0 + BLOCK, pair * LANES:(pair + 1) * LANES]
        zero = jnp.zeros_like(qp)
        q2 = jnp.concatenate([jnp.where(low_half, qp, zero), jnp.where(low_half, zero, qp)],
                             axis=0)
        kx = band_of(blk)[0][:, kv * LANES:(kv + 1) * LANES]
        return _dot_nt(kx, q2)

    def finish(blk, pair, s2):
        kv = (2 * pair) // (N_HEADS_A // N_KV_A)
        r0 = blk * BLOCK
        _, vtband, dist = band_of(blk)
        vt = vtband[kv * LANES:(kv + 1) * LANES, :]
        ps, ms = [], []
        for e in range(2):
            h = 2 * pair + e
            s = s2[:, e * BLOCK:(e + 1) * BLOCK] - ALIBI_SLOPES[h] * dist
            m = jnp.maximum(jnp.max(s, axis=0, keepdims=True), sink_ref[h] * LOG2_E)
            ps.append(jnp.exp2(s - m).astype(bf16))
            ms.append(m)
        o2 = _dot(vt, jnp.concatenate(ps, axis=1))
        outs = []
        for e in range(2):
            h = 2 * pair + e
            o_t = o2[:, e * BLOCK:(e + 1) * BLOCK]
            denom = (o_t[HEAD_DIM_A:HEAD_DIM_A + 1]
                     + jnp.exp2(sink_ref[h] * LOG2_E - ms[e]))
            outs.append(o_t[0:HEAD_DIM_A] / denom)
        o_ref[0, r0:r0 + BLOCK, pair * LANES:(pair + 1) * LANES] = (
            jnp.concatenate(outs, axis=0).T.astype(bf16))

    pending = [scores(*c) for c in chains[:SWA_LEAD]]
    for n, c in enumerate(chains):
        s2 = pending.pop(0)
        if n + SWA_LEAD < len(chains):
            pending.append(scores(*chains[n + SWA_LEAD]))
        finish(*c, s2)


def _mla_kernel(q_ref, k_ref, vt_ref, o_ref, s_ref, bmax_ref, acc_ref, m_ref):
    bf16 = jnp.bfloat16
    bq = MLA_BQ
    ch = MLA_CHUNK
    qi = pl.program_id(2)
    krow = lax.broadcasted_iota(jnp.int32, (bq, ch), 0)
    qcol = lax.broadcasted_iota(jnp.int32, (bq, ch), 1)
    neg_inf = jnp.float32(-jnp.inf)
    m_ref[...] = jnp.full(m_ref.shape, neg_inf, jnp.float32)
    acc_ref[...] = jnp.zeros(acc_ref.shape, jnp.float32)

    def scores_head(buf, kb, e):
        start = pl.multiple_of(kb * bq, bq)
        sl = slice(e * LANES, (e + 1) * LANES)
        s = _dot_nt(k_ref[0, pl.ds(start, bq), sl], q_ref[0, :, sl])
        s_ref[buf, e] = s
        bmax_ref[buf, e] = jnp.max(s, axis=0, keepdims=True)

    def scores_into(buf, kb):
        for e in range(MLA_HEADS):
            scores_head(buf, kb, e)

    def softmax_pv_head(buf, kb, e, masked):
        vt = vt_ref[0, kb, e * LANES:(e + 1) * LANES, :]
        for c in range(bq // ch):
            cs = slice(c * ch, (c + 1) * ch)
            s = s_ref[buf, e, :, cs]
            if masked:
                s = jnp.where(krow <= qcol + c * ch, s, neg_inf)
                blockmax = jnp.max(s, axis=0, keepdims=True)
            else:
                blockmax = bmax_ref[buf, e, :, cs]
            m_prev = m_ref[e, :, cs]
            m_new = jnp.maximum(m_prev, blockmax)
            p = jnp.exp2(s - m_new).astype(bf16)
            alpha = jnp.exp2(m_prev - m_new)
            acc_ref[e, :, cs] = alpha * acc_ref[e, :, cs] + _dot(vt, p)
            m_ref[e, :, cs] = m_new

    def softmax_pv(buf, kb, masked):
        for e in range(MLA_HEADS):
            softmax_pv_head(buf, kb, e, masked)

    def half(src, dst, kb):
        for e in range(min(MLA_LEAD, MLA_HEADS)):
            scores_head(dst, kb + 1, e)
        for e in range(MLA_HEADS):
            softmax_pv_head(src, kb, e, False)
            if e + MLA_LEAD < MLA_HEADS:
                scores_head(dst, kb + 1, e + MLA_LEAD)

    def pair_body(t, carry):
        half(0, 1, 2 * t)
        half(1, 0, 2 * t + 1)
        return carry

    scores_into(0, 0)
    lax.fori_loop(0, qi // 2, pair_body, 0)

    @pl.when(qi % 2 == 1)
    def _():
        half(0, 1, qi - 1)
        softmax_pv(1, qi, True)

    @pl.when(qi % 2 == 0)
    def _():
        softmax_pv(0, qi, True)

    outs = []
    for e in range(MLA_HEADS):
        acc = acc_ref[e]
        outs.append(acc[0:V_DIM_B] / acc[V_DIM_B:V_DIM_B + 1])
    o_ref[0] = jnp.concatenate(outs, axis=0).T.astype(bf16)


def _merge_kernel(x_ref, oa_ref, ob_ref, gpre_ref, wg_ref, woa_ref, wob_ref, wout_ref, gpost_ref,
                  gpre2_ref, wup_ref, wdn_ref, gpost2_ref, o_ref):
    bf16 = jnp.bfloat16
    x = x_ref[...]
    hb = (_rms(x) * gpre_ref[...]).astype(bf16)
    gate_a = jax.nn.sigmoid(_dot_nt(hb, wg_ref[0:D_MODEL, :]))
    gate_b = jax.nn.sigmoid(_dot_nt(hb, wg_ref[D_MODEL:2 * D_MODEL, :]))
    merged = gate_a * _dot(oa_ref[...], woa_ref[...]) + gate_b * _dot(ob_ref[...], wob_ref[...])
    y = _dot(merged.astype(bf16), wout_ref[...])
    x1 = x + _rms(y) * gpost_ref[...]
    h2 = (_rms(x1) * gpre2_ref[...]).astype(bf16)
    y2 = jnp.zeros(x.shape, jnp.float32)
    for c in range(D_FF // FF_CHUNK):
        cs = slice(c * FF_CHUNK, (c + 1) * FF_CHUNK)
        up = jnp.maximum(_dot(h2, wup_ref[:, cs]), 0.0)
        y2 = y2 + _dot((up * up).astype(bf16), wdn_ref[cs, :])
    o_ref[...] = x1 + _rms(y2) * gpost2_ref[...]


def _const_spec(shape):
    return pl.BlockSpec(shape, lambda *_: (0,) * len(shape), pipeline_mode=pl.Buffered(1))


def kernel(x, positions, pre_norm_mix, w_in, q_a_norm, w_q_b, kv_a_norm, w_kv_b, sinks, w_o_a,
           w_o_b, w_out, post_norm_mix, pre_norm_mlp, w_up, w_down, post_norm_mlp):
    f32, bf16 = jnp.float32, jnp.bfloat16
    B, S, D = x.shape
    T = B * S
    depth = w_in.shape[0]
    for l in range(depth):
        assert w_in.shape[2] == D_IN
        wit = jnp.swapaxes(w_in[l], 0, 1).astype(bf16)
        ka_t = wit[IN_KA:IN_VA]
        va_t = wit[IN_VA:IN_CQ]
        kr_t = wit[IN_KR:D_IN]
        hr = QK_ROPE // 2
        hd = HEAD_DIM_A
        z = lambda n: jnp.zeros((n, D_MODEL), bf16)
        w_small_t = jnp.concatenate([
            ka_t[:hd], ka_t[:hd], ka_t[hd:], ka_t[hd:],
            z(QK_NOPE), kr_t, z(LANES - Q_HEAD_B),
            z(QK_NOPE), kr_t[hr:], kr_t[:hr], z(LANES - Q_HEAD_B)], axis=0)
        w_va_t = jnp.concatenate([va_t[:hd], z(LANES - hd), va_t[hd:], z(LANES - hd)], axis=0)

        wq = w_q_b[l].reshape(Q_LORA, N_HEADS_B, Q_HEAD_B)
        q_nope, q_rope = wq[..., :QK_NOPE], wq[..., QK_NOPE:]
        zq = lambda n: jnp.zeros((Q_LORA, N_HEADS_B, n), f32)
        wq_main = jnp.concatenate([q_nope, q_rope, zq(LANES - Q_HEAD_B)], -1)
        wq_swap = jnp.concatenate([zq(QK_NOPE), q_rope[..., hr:], q_rope[..., :hr],
                                   zq(LANES - Q_HEAD_B)], -1)
        wq_main = wq_main.reshape(Q_LORA, N_HEADS_B * LANES).astype(bf16)
        wq_swap = wq_swap.reshape(Q_LORA, N_HEADS_B * LANES).astype(bf16)

        wkv = w_kv_b[l].reshape(KV_LORA, N_HEADS_B, KV_HEAD_B)
        kv_k, kv_v = wkv[..., :QK_NOPE], wkv[..., QK_NOPE:]
        zk = jnp.zeros((KV_LORA, N_HEADS_B, HALF), f32)
        wkv_k = jnp.concatenate([kv_k, zk], -1).reshape(KV_LORA, N_HEADS_B * LANES).astype(bf16)
        wkv_vt = jnp.concatenate([kv_v, zk], -1).reshape(KV_LORA, N_HEADS_B * LANES).T.astype(bf16)

        freqs = ROPE_THETA ** (-jnp.arange(0, QK_ROPE, 2, dtype=f32) / QK_ROPE)
        zf = jnp.zeros((QK_NOPE,), f32)
        zt = jnp.zeros((LANES - Q_HEAD_B,), f32)
        freq_slab = jnp.concatenate([zf, freqs, freqs, zt])[None, :]
        sgn_slab = jnp.concatenate([zf, -jnp.ones((hr,), f32), jnp.ones((hr,), f32), zt])[None, :]

        row = lambda g: g.reshape(1, -1).astype(f32)
        x2 = x.reshape(T, D)

        tm = PROJ_TM
        assert tm == MLA_BQ and S % tm == 0
        nkb = S // tm
        tok = lambda w: pl.BlockSpec((tm, w), lambda i: (i, 0))
        w_rows = lambda n, start: pl.BlockSpec((n, D), lambda i: (start // n, 0),
                                               pipeline_mode=pl.Buffered(1))
        assert IN_QA % WIDTH_A == 0 and IN_CQ % Q_LORA == 0 and IN_CKV % KV_LORA == 0
        qa, ka_x, va_t3, qb, kb, vt = pl.pallas_call(
            _proj_kernel,
            grid=(T // tm,),
            in_specs=[tok(D), pl.BlockSpec((1, 1, tm), lambda i: (i, 0, 0)), _const_spec((1, D)),
                      w_rows(WIDTH_A, IN_QA), w_rows(Q_LORA, IN_CQ), w_rows(KV_LORA, IN_CKV),
                      _const_spec((SM_ROWS, D)), _const_spec((N_KV_A * LANES, D)),
                      _const_spec((1, Q_LORA)), _const_spec((Q_LORA, N_HEADS_B * LANES)),
                      _const_spec((Q_LORA, N_HEADS_B * LANES)), _const_spec((1, KV_LORA)),
                      _const_spec((KV_LORA, N_HEADS_B * LANES)),
                      _const_spec((N_HEADS_B * LANES, KV_LORA)),
                      _const_spec((1, LANES)), _const_spec((1, LANES))],
            out_specs=[tok(WIDTH_A), tok(2 * LANES),
                       pl.BlockSpec((1, N_KV_A * LANES, tm), lambda i: (i // nkb, 0, i % nkb)),
                       tok(N_HEADS_B * LANES), tok(N_HEADS_B * LANES),
                       pl.BlockSpec((1, 1, N_HEADS_B * LANES, tm),
                                    lambda i: (i // nkb, i % nkb, 0, 0))],
            out_shape=[jax.ShapeDtypeStruct((T, WIDTH_A), bf16),
                       jax.ShapeDtypeStruct((T, 2 * LANES), bf16),
                       jax.ShapeDtypeStruct((B, N_KV_A * LANES, S), bf16),
                       jax.ShapeDtypeStruct((T, N_HEADS_B * LANES), bf16),
                       jax.ShapeDtypeStruct((T, N_HEADS_B * LANES), bf16),
                       jax.ShapeDtypeStruct((B, nkb, N_HEADS_B * LANES, tm), bf16)],
            compiler_params=pltpu.CompilerParams(dimension_semantics=("arbitrary",),
                                                 vmem_limit_bytes=VMEM_LIMIT),
            name="proj",
        )(x2, positions.reshape(T // tm, 1, tm), row(pre_norm_mix[l]), wit, wit, wit, w_small_t,
          w_va_t, row(q_a_norm[l]), wq_main, wq_swap, row(kv_a_norm[l]), wkv_k, wkv_vt,
          freq_slab, sgn_slab)

        tq = SWA_TQ
        nb_per = tq // BLOCK
        qa3 = qa.reshape(B, S, WIDTH_A)
        ka3 = ka_x.reshape(B, S, 2 * LANES)
        pos_r3 = positions.reshape(B, 1, S)
        cur = lambda w: pl.BlockSpec((1, tq, w), lambda b, i: (b, i, 0))
        prev_blk = lambda b, i: jnp.maximum(i * nb_per - 1, 0)
        out_a = pl.pallas_call(
            _swa_kernel,
            grid=(B, S // tq),
            in_specs=[pl.BlockSpec(memory_space=pltpu.SMEM),
                      cur(WIDTH_A), cur(2 * LANES),
                      pl.BlockSpec((1, BLOCK, 2 * LANES), lambda b, i: (b, prev_blk(b, i), 0)),
                      pl.BlockSpec((1, N_KV_A * LANES, tq), lambda b, i: (b, 0, i)),
                      pl.BlockSpec((1, N_KV_A * LANES, BLOCK), lambda b, i: (b, 0, prev_blk(b, i))),
                      pl.BlockSpec((1, 1, tq), lambda b, i: (b, 0, i)),
                      pl.BlockSpec((1, 1, BLOCK), lambda b, i: (b, 0, prev_blk(b, i)))],
            out_specs=cur(WIDTH_A),
            out_shape=jax.ShapeDtypeStruct((B, S, WIDTH_A), bf16),
            compiler_params=pltpu.CompilerParams(dimension_semantics=("arbitrary", "arbitrary"),
                                                 vmem_limit_bytes=VMEM_LIMIT),
            name="swa",
        )(sinks[l].astype(f32), qa3, ka3, ka3, va_t3, va_t3, pos_r3, pos_r3)

        bq, nh = MLA_BQ, MLA_HEADS
        qb3 = qb.reshape(B, S, N_HEADS_B * LANES)
        kb3 = kb.reshape(B, S, N_HEADS_B * LANES)
        out_b = pl.pallas_call(
            _mla_kernel,
            grid=(B, N_HEADS_B // nh, S // bq),
            in_specs=[pl.BlockSpec((1, bq, nh * LANES), lambda b, j, i: (b, i, j)),
                      pl.BlockSpec((1, S, nh * LANES), lambda b, j, i: (b, 0, j)),
                      pl.BlockSpec((1, nkb, nh * LANES, bq), lambda b, j, i: (b, 0, j, 0))],
            out_specs=pl.BlockSpec((1, bq, nh * V_DIM_B), lambda b, j, i: (b, i, j)),
            out_shape=jax.ShapeDtypeStruct((B, S, WIDTH_B), bf16),
            scratch_shapes=[pltpu.VMEM((2, nh, bq, bq), f32),
                            pltpu.VMEM((2, nh, 1, bq), f32),
                            pltpu.VMEM((nh, LANES, bq), f32), pltpu.VMEM((nh, 1, bq), f32)],
            compiler_params=pltpu.CompilerParams(
                dimension_semantics=("arbitrary", "arbitrary", "arbitrary"),
                vmem_limit_bytes=VMEM_LIMIT),
            name="mla",
        )(qb3, kb3, vt)

        tm = MERGE_TM
        tok = lambda w: pl.BlockSpec((tm, w), lambda i: (i, 0))
        x2 = pl.pallas_call(
            _merge_kernel,
            grid=(T // tm,),
            in_specs=[tok(D), tok(WIDTH_A), tok(WIDTH_B), _const_spec((1, D)),
                      w_rows(2 * D, IN_GATES), _const_spec((WIDTH_A, D)), _const_spec((WIDTH_B, D)),
                      _const_spec((D, D)), _const_spec((1, D)), _const_spec((1, D)),
                      _const_spec((D, D_FF)), _const_spec((D_FF, D)), _const_spec((1, D))],
            out_specs=tok(D),
            out_shape=jax.ShapeDtypeStruct((T, D), f32),
            compiler_params=pltpu.CompilerParams(dimension_semantics=("arbitrary",),
                                                 vmem_limit_bytes=VMEM_LIMIT),
            name="merge_mlp",
        )(x2, out_a.reshape(T, WIDTH_A), out_b.reshape(T, WIDTH_B), row(pre_norm_mix[l]), wit,
          w_o_a[l].astype(bf16), w_o_b[l].astype(bf16), w_out[l].astype(bf16),
          row(post_norm_mix[l]), row(pre_norm_mlp[l]), w_up[l].astype(bf16),
          w_down[l].astype(bf16), row(post_norm_mlp[l]))
        x = x2.reshape(B, S, D)
    return x
```

```python
import functools

import jax
import jax.numpy as jnp
from jax import lax
from jax.experimental import pallas as pl
from jax.experimental.pallas import tpu as pltpu

D_MODEL = 1024
N_HEADS_A = 8
N_KV_A = 2
HEAD_DIM_A = 64
WINDOW = 128
BLOCK = 128
N_HEADS_B = 8
QK_NOPE = 64
QK_ROPE = 32
V_DIM_B = 64
Q_LORA = 256
KV_LORA = 128
ROPE_THETA = 10000.0
D_FF = 4 * D_MODEL
EPS = 1e-6

WIDTH_A = N_HEADS_A * HEAD_DIM_A
WIDTH_B = N_HEADS_B * V_DIM_B
Q_HEAD_B = QK_NOPE + QK_ROPE
KV_HEAD_B = QK_NOPE + V_DIM_B

LANES = 128
HALF = LANES // 2
SCALE_A = HEAD_DIM_A ** -0.5
SCALE_B = Q_HEAD_B ** -0.5
LOG2_E = 1.4426950408889634
QSCALE_A = SCALE_A * LOG2_E
QSCALE_B = SCALE_B * LOG2_E
ALIBI_SLOPES = tuple(2.0 ** (-8.0 * (h + 1) / N_HEADS_A) for h in range(N_HEADS_A))

IN_GATES = 0
IN_QA = 2 * D_MODEL
IN_KA = IN_QA + WIDTH_A
IN_VA = IN_KA + N_KV_A * HEAD_DIM_A
IN_CQ = IN_VA + N_KV_A * HEAD_DIM_A
IN_CKV = IN_CQ + Q_LORA
IN_KR = IN_CKV + KV_LORA
D_IN = IN_KR + QK_ROPE
SM_KA = 0
SM_KR = SM_KA + 2 * LANES
SM_KRS = SM_KR + LANES
SM_ROWS = SM_KRS + LANES

PROJ_TM = 1024
PROJ_SUB = 512
SWA_TQ = 512
SWA_LEAD = 4
MLA_BQ = 512
MLA_CHUNK = 256
MLA_HEADS = 4
MLA_LEAD = 2
MERGE_TM = 512
FF_CHUNK = 1024
VMEM_LIMIT = 60 * 1024 * 1024


def _rms(v):
    return v * lax.rsqrt(jnp.mean(v * v, axis=-1, keepdims=True) + EPS)


def _dot(a, b):
    return jnp.dot(a, b, preferred_element_type=jnp.float32)


def _dot_nt(a, b):
    return lax.dot_general(a, b, (((1,), (1,)), ((), ())), preferred_element_type=jnp.float32)


def _proj_kernel(x_ref, pos_ref, gpre_ref, wqa_ref, wcq_ref, wckv_ref, wsm_ref, wvat_ref, gq_ref,
                 wqm_ref, wqs_ref, gkv_ref, wkk_ref, wkvt_ref, freq_ref,
                 qa_ref, ka_ref, vat_ref, qb_ref, kb_ref, vt_ref):
    bf16 = jnp.bfloat16
    sub = PROJ_SUB

    def first_stage(n):
        rows = slice(n * sub, (n + 1) * sub)
        hb = (_rms(x_ref[rows, :]) * gpre_ref[...]).astype(bf16)
        qa_ref[rows, :] = (_dot_nt(hb, wqa_ref[...]) * QSCALE_A).astype(bf16)
        small = _dot_nt(hb, wsm_ref[...])
        ka_ref[rows, :] = small[:, SM_KA:SM_KR].astype(bf16)
        va_t = _dot_nt(wvat_ref[...], hb)
        varow = lax.broadcasted_iota(jnp.int32, va_t.shape, 0)
        vat_ref[0, :, rows] = jnp.where(varow % LANES == HEAD_DIM_A, 1.0, va_t).astype(bf16)
        cq = _dot_nt(hb, wcq_ref[...])
        ckv = _dot_nt(hb, wckv_ref[...])
        pos = pos_ref[0, :, rows].astype(jnp.float32)
        ang = freq_ref[...] * pos
        cos_t, sin_t = jnp.cos(ang), jnp.sin(ang)
        one = jnp.ones((QK_NOPE, sub), jnp.float32)
        zero = jnp.zeros((QK_NOPE, sub), jnp.float32)
        pad = LANES - Q_HEAD_B
        cos = jnp.concatenate([one, cos_t, cos_t, one[:pad]], axis=0).T
        sin = jnp.concatenate([zero, -sin_t, sin_t, zero[:pad]], axis=0).T
        k_rot = small[:, SM_KR:SM_KRS] * cos + small[:, SM_KRS:SM_ROWS] * sin
        return cq, ckv, cos, sin, k_rot

    def second_stage(n, cq, ckv, cos, sin, k_rot):
        rows = slice(n * sub, (n + 1) * sub)
        cqn = (_rms(cq) * gq_ref[...]).astype(bf16)
        q_main = _dot(cqn, wqm_ref[...])
        q_swap = _dot(cqn, wqs_ref[...])
        ckvn = (_rms(ckv) * gkv_ref[...]).astype(bf16)
        k_nope = _dot(ckvn, wkk_ref[...])
        v_t = _dot_nt(wkvt_ref[...], ckvn)
        vrow = lax.broadcasted_iota(jnp.int32, v_t.shape, 0)
        vt_ref[0, n] = jnp.where(vrow % LANES == V_DIM_B, 1.0, v_t).astype(bf16)
        for h in range(N_HEADS_B):
            sl = slice(h * LANES, (h + 1) * LANES)
            qb_ref[rows, sl] = ((q_main[:, sl] * cos + q_swap[:, sl] * sin) * QSCALE_B).astype(bf16)
            kb_ref[rows, sl] = (k_nope[:, sl] + k_rot).astype(bf16)

    n_sub = x_ref.shape[0] // sub
    staged = [first_stage(n) for n in range(n_sub)]
    for n in range(n_sub):
        second_stage(n, *staged[n])


def _swa_kernel(sink_ref, q_ref, kc_ref, kp_ref, vtc_ref, vtp_ref, posc_ref, posp_ref, o_ref):
    bf16 = jnp.bfloat16
    i = pl.program_id(1)
    krow = lax.broadcasted_iota(jnp.int32, (2 * BLOCK, BLOCK), 0)
    qcol = lax.broadcasted_iota(jnp.int32, (2 * BLOCK, BLOCK), 1)
    ahead = krow - qcol
    band = (ahead > 0) & (ahead <= WINDOW)
    lane = lax.broadcasted_iota(jnp.int32, (BLOCK, LANES), 1)
    low_half = lane < HALF
    pos_inf = jnp.float32(jnp.inf)

    n_pairs = N_HEADS_A // 2
    chains = [(blk, pair) for blk in range(SWA_TQ // BLOCK) for pair in range(n_pairs)]
    bands = {}

    def band_of(blk):
        if blk not in bands:
            r0 = blk * BLOCK
            if blk == 0:
                kband = jnp.concatenate([kp_ref[0], kc_ref[0, 0:BLOCK, :]], axis=0)
                vtband = jnp.concatenate([vtp_ref[0], vtc_ref[0, :, 0:BLOCK]], axis=1)
                kpos = jnp.concatenate([posp_ref[0], posc_ref[0, :, 0:BLOCK]], axis=1)
                mask = band & ((krow >= BLOCK) | (i > 0))
            else:
                kband = kc_ref[0, r0 - BLOCK:r0 + BLOCK, :]
                vtband = vtc_ref[0, :, r0 - BLOCK:r0 + BLOCK]
                kpos = posc_ref[0, :, r0 - BLOCK:r0 + BLOCK]
                mask = band
            qpos = posc_ref[0, :, r0:r0 + BLOCK]
            kpos_col = jnp.broadcast_to(kpos, (BLOCK, 2 * BLOCK)).T
            dist = jnp.abs(kpos_col - qpos).astype(jnp.float32) * LOG2_E
            dist = jnp.where(mask, dist, pos_inf)
            bands[blk] = (kband, vtband, dist)
        return bands[blk]

    def scores(blk, pair):
        kv = (2 * pair) // (N_HEADS_A // N_KV_A)
        r0 = blk * BLOCK
        qp = q_ref[0, r0:r0 + BLOCK, pair * LANES:(pair + 1) * LANES]
        zero = jnp.zeros_like(qp)
        q2 = jnp.concatenate([jnp.where(low_half, qp, zero), jnp.where(low_half, zero, qp)],
                             axis=0)
        kx = band_of(blk)[0][:, kv * LANES:(kv + 1) * LANES]
        return _dot_nt(kx, q2)

    def finish(blk, pair, s2):
        kv = (2 * pair) // (N_HEADS_A // N_KV_A)
        r0 = blk * BLOCK
        _, vtband, dist = band_of(blk)
        vt = vtband[kv * LANES:(kv + 1) * LANES, :]
        ps, ms = [], []
        for e in range(2):
            h = 2 * pair + e
            s = s2[:, e * BLOCK:(e + 1) * BLOCK] - ALIBI_SLOPES[h] * dist
            m = jnp.maximum(jnp.max(s, axis=0, keepdims=True), sink_ref[h] * LOG2_E)
            ps.append(jnp.exp2(s - m).astype(bf16))
            ms.append(m)
        o2 = _dot(vt, jnp.concatenate(ps, axis=1))
        outs = []
        for e in range(2):
            h = 2 * pair + e
            o_t = o2[:, e * BLOCK:(e + 1) * BLOCK]
            denom = (o_t[HEAD_DIM_A:HEAD_DIM_A + 1]
                     + jnp.exp2(sink_ref[h] * LOG2_E - ms[e]))
            outs.append(o_t[0:HEAD_DIM_A] / denom)
        o_ref[0, r0:r0 + BLOCK, pair * LANES:(pair + 1) * LANES] = (
            jnp.concatenate(outs, axis=0).T.astype(bf16))

    pending = [scores(*c) for c in chains[:SWA_LEAD]]
    for n, c in enumerate(chains):
        s2 = pending.pop(0)
        if n + SWA_LEAD < len(chains):
            pending.append(scores(*chains[n + SWA_LEAD]))
        finish(*c, s2)


def _mla_kernel(q_ref, k_ref, vt_ref, o_ref, s_ref, bmax_ref, acc_ref, m_ref):
    bf16 = jnp.bfloat16
    bq = MLA_BQ
    ch = MLA_CHUNK
    qi = pl.program_id(2)
    krow = lax.broadcasted_iota(jnp.int32, (bq, ch), 0)
    qcol = lax.broadcasted_iota(jnp.int32, (bq, ch), 1)
    neg_inf = jnp.float32(-jnp.inf)
    m_ref[...] = jnp.full(m_ref.shape, neg_inf, jnp.float32)
    acc_ref[...] = jnp.zeros(acc_ref.shape, jnp.float32)

    def scores_head(buf, kb, e):
        start = pl.multiple_of(kb * bq, bq)
        sl = slice(e * LANES, (e + 1) * LANES)
        s = _dot_nt(k_ref[0, pl.ds(start, bq), sl], q_ref[0, :, sl])
        s_ref[buf, e] = s
        bmax_ref[buf, e] = jnp.max(s, axis=0, keepdims=True)

    def scores_into(buf, kb):
        for e in range(MLA_HEADS):
            scores_head(buf, kb, e)

    def softmax_pv_head(buf, kb, e, masked):
        vt = vt_ref[0, kb, e * LANES:(e + 1) * LANES, :]
        for c in range(bq // ch):
            cs = slice(c * ch, (c + 1) * ch)
            s = s_ref[buf, e, :, cs]
            if masked:
                s = jnp.where(krow <= qcol + c * ch, s, neg_inf)
                blockmax = jnp.max(s, axis=0, keepdims=True)
            else:
                blockmax = bmax_ref[buf, e, :, cs]
            m_prev = m_ref[e, :, cs]
            m_new = jnp.maximum(m_prev, blockmax)
            p = jnp.exp2(s - m_new).astype(bf16)
            alpha = jnp.exp2(m_prev - m_new)
            acc_ref[e, :, cs] = alpha * acc_ref[e, :, cs] + _dot(vt, p)
            m_ref[e, :, cs] = m_new

    def softmax_pv(buf, kb, masked):
        for e in range(MLA_HEADS):
            softmax_pv_head(buf, kb, e, masked)

    def half(src, dst, kb):
        for e in range(min(MLA_LEAD, MLA_HEADS)):
            scores_head(dst, kb + 1, e)
        for e in range(MLA_HEADS):
            softmax_pv_head(src, kb, e, False)
            if e + MLA_LEAD < MLA_HEADS:
                scores_head(dst, kb + 1, e + MLA_LEAD)

    def pair_body(t, carry):
        half(0, 1, 2 * t)
        half(1, 0, 2 * t + 1)
        return carry

    scores_into(0, 0)
    lax.fori_loop(0, qi // 2, pair_body, 0)

    @pl.when(qi % 2 == 1)
    def _():
        half(0, 1, qi - 1)
        softmax_pv(1, qi, True)

    @pl.when(qi % 2 == 0)
    def _():
        softmax_pv(0, qi, True)

    outs = []
    for e in range(MLA_HEADS):
        acc = acc_ref[e]
        outs.append(acc[0:V_DIM_B] / acc[V_DIM_B:V_DIM_B + 1])
    o_ref[0] = jnp.concatenate(outs, axis=0).T.astype(bf16)


def _merge_kernel(x_ref, oa_ref, ob_ref, gpre_ref, wg_ref, woa_ref, wob_ref, wout_ref, gpost_ref,
                  gpre2_ref, wup_ref, wdn_ref, gpost2_ref, o_ref):
    bf16 = jnp.bfloat16
    x = x_ref[...]
    hb = (_rms(x) * gpre_ref[...]).astype(bf16)
    gate_a = jax.nn.sigmoid(_dot_nt(hb, wg_ref[0:D_MODEL, :]))
    gate_b = jax.nn.sigmoid(_dot_nt(hb, wg_ref[D_MODEL:2 * D_MODEL, :]))
    merged = gate_a * _dot(oa_ref[...], woa_ref[...]) + gate_b * _dot(ob_ref[...], wob_ref[...])
    y = _dot(merged.astype(bf16), wout_ref[...])
    x1 = x + _rms(y) * gpost_ref[...]
    h2 = (_rms(x1) * gpre2_ref[...]).astype(bf16)
    y2 = jnp.zeros(x.shape, jnp.float32)
    for c in range(D_FF // FF_CHUNK):
        cs = slice(c * FF_CHUNK, (c + 1) * FF_CHUNK)
        up = jnp.maximum(_dot(h2, wup_ref[:, cs]), 0.0)
        y2 = y2 + _dot((up * up).astype(bf16), wdn_ref[cs, :])
    o_ref[...] = x1 + _rms(y2) * gpost2_ref[...]


def _const_spec(shape):
    return pl.BlockSpec(shape, lambda *_: (0,) * len(shape), pipeline_mode=pl.Buffered(1))


def kernel(x, positions, pre_norm_mix, w_in, q_a_norm, w_q_b, kv_a_norm, w_kv_b, sinks, w_o_a,
           w_o_b, w_out, post_norm_mix, pre_norm_mlp, w_up, w_down, post_norm_mlp):
    f32, bf16 = jnp.float32, jnp.bfloat16
    B, S, D = x.shape
    T = B * S
    depth = w_in.shape[0]
    for l in range(depth):
        assert w_in.shape[2] == D_IN
        wit = jnp.swapaxes(w_in[l], 0, 1).astype(bf16)
        ka_t = wit[IN_KA:IN_VA]
        va_t = wit[IN_VA:IN_CQ]
        kr_t = wit[IN_KR:D_IN]
        hr = QK_ROPE // 2
        hd = HEAD_DIM_A
        z = lambda n: jnp.zeros((n, D_MODEL), bf16)
        w_small_t = jnp.concatenate([
            ka_t[:hd], ka_t[:hd], ka_t[hd:], ka_t[hd:],
            z(QK_NOPE), kr_t, z(LANES - Q_HEAD_B),
            z(QK_NOPE), kr_t[hr:], kr_t[:hr], z(LANES - Q_HEAD_B)], axis=0)
        w_va_t = jnp.concatenate([va_t[:hd], z(LANES - hd), va_t[hd:], z(LANES - hd)], axis=0)

        wq = w_q_b[l].reshape(Q_LORA, N_HEADS_B, Q_HEAD_B)
        q_nope, q_rope = wq[..., :QK_NOPE], wq[..., QK_NOPE:]
        zq = lambda n: jnp.zeros((Q_LORA, N_HEADS_B, n), f32)
        wq_main = jnp.concatenate([q_nope, q_rope, zq(LANES - Q_HEAD_B)], -1)
        wq_swap = jnp.concatenate([zq(QK_NOPE), q_rope[..., hr:], q_rope[..., :hr],
                                   zq(LANES - Q_HEAD_B)], -1)
        wq_main = wq_main.reshape(Q_LORA, N_HEADS_B * LANES).astype(bf16)
        wq_swap = wq_swap.reshape(Q_LORA, N_HEADS_B * LANES).astype(bf16)

        wkv = w_kv_b[l].reshape(KV_LORA, N_HEADS_B, KV_HEAD_B)
        kv_k, kv_v = wkv[..., :QK_NOPE], wkv[..., QK_NOPE:]
        zk = jnp.zeros((KV_LORA, N_HEADS_B, HALF), f32)
        wkv_k = jnp.concatenate([kv_k, zk], -1).reshape(KV_LORA, N_HEADS_B * LANES).astype(bf16)
        wkv_vt = jnp.concatenate([kv_v, zk], -1).reshape(KV_LORA, N_HEADS_B * LANES).T.astype(bf16)

        freq_col = (ROPE_THETA ** (-jnp.arange(0, QK_ROPE, 2, dtype=f32) / QK_ROPE))[:, None]

        row = lambda g: g.reshape(1, -1).astype(f32)
        x2 = x.reshape(T, D)

        tm = PROJ_TM
        assert PROJ_SUB == MLA_BQ and tm % PROJ_SUB == 0 and S % tm == 0
        nkb = S // MLA_BQ
        n_sub = tm // PROJ_SUB
        npb = S // tm
        tok = lambda w: pl.BlockSpec((tm, w), lambda i: (i, 0))
        w_rows = lambda n, start: pl.BlockSpec((n, D), lambda i: (start // n, 0),
                                               pipeline_mode=pl.Buffered(1))
        assert IN_QA % WIDTH_A == 0 and IN_CQ % Q_LORA == 0 and IN_CKV % KV_LORA == 0
        qa, ka_x, va_t3, qb, kb, vt = pl.pallas_call(
            _proj_kernel,
            grid=(T // tm,),
            in_specs=[tok(D), pl.BlockSpec((1, 1, tm), lambda i: (i, 0, 0)), _const_spec((1, D)),
                      w_rows(WIDTH_A, IN_QA), w_rows(Q_LORA, IN_CQ), w_rows(KV_LORA, IN_CKV),
                      _const_spec((SM_ROWS, D)), _const_spec((N_KV_A * LANES, D)),
                      _const_spec((1, Q_LORA)), _const_spec((Q_LORA, N_HEADS_B * LANES)),
                      _const_spec((Q_LORA, N_HEADS_B * LANES)), _const_spec((1, KV_LORA)),
                      _const_spec((KV_LORA, N_HEADS_B * LANES)),
                      _const_spec((N_HEADS_B * LANES, KV_LORA)),
                      _const_spec((QK_ROPE // 2, 1))],
            out_specs=[tok(WIDTH_A), tok(2 * LANES),
                       pl.BlockSpec((1, N_KV_A * LANES, tm), lambda i: (i // npb, 0, i % npb)),
                       tok(N_HEADS_B * LANES), tok(N_HEADS_B * LANES),
                       pl.BlockSpec((1, n_sub, N_HEADS_B * LANES, PROJ_SUB),
                                    lambda i: (i // npb, i % npb, 0, 0))],
            out_shape=[jax.ShapeDtypeStruct((T, WIDTH_A), bf16),
                       jax.ShapeDtypeStruct((T, 2 * LANES), bf16),
                       jax.ShapeDtypeStruct((B, N_KV_A * LANES, S), bf16),
                       jax.ShapeDtypeStruct((T, N_HEADS_B * LANES), bf16),
                       jax.ShapeDtypeStruct((T, N_HEADS_B * LANES), bf16),
                       jax.ShapeDtypeStruct((B, nkb, N_HEADS_B * LANES, PROJ_SUB), bf16)],
            compiler_params=pltpu.CompilerParams(dimension_semantics=("arbitrary",),
                                                 vmem_limit_bytes=VMEM_LIMIT),
            name="proj",
        )(x2, positions.reshape(T // tm, 1, tm), row(pre_norm_mix[l]), wit, wit, wit, w_small_t,
          w_va_t, row(q_a_norm[l]), wq_main, wq_swap, row(kv_a_norm[l]), wkv_k, wkv_vt,
          freq_col)

        tq = SWA_TQ
        nb_per = tq // BLOCK
        qa3 = qa.reshape(B, S, WIDTH_A)
        ka3 = ka_x.reshape(B, S, 2 * LANES)
        pos_r3 = positions.reshape(B, 1, S)
        cur = lambda w: pl.BlockSpec((1, tq, w), lambda b, i: (b, i, 0))
        prev_blk = lambda b, i: jnp.maximum(i * nb_per - 1, 0)
        out_a = pl.pallas_call(
            _swa_kernel,
            grid=(B, S // tq),
            in_specs=[pl.BlockSpec(memory_space=pltpu.SMEM),
                      cur(WIDTH_A), cur(2 * LANES),
                      pl.BlockSpec((1, BLOCK, 2 * LANES), lambda b, i: (b, prev_blk(b, i), 0)),
                      pl.BlockSpec((1, N_KV_A * LANES, tq), lambda b, i: (b, 0, i)),
                      pl.BlockSpec((1, N_KV_A * LANES, BLOCK), lambda b, i: (b, 0, prev_blk(b, i))),
                      pl.BlockSpec((1, 1, tq), lambda b, i: (b, 0, i)),
                      pl.BlockSpec((1, 1, BLOCK), lambda b, i: (b, 0, prev_blk(b, i)))],
            out_specs=cur(WIDTH_A),
            out_shape=jax.ShapeDtypeStruct((B, S, WIDTH_A), bf16),
            compiler_params=pltpu.CompilerParams(dimension_semantics=("arbitrary", "arbitrary"),
                                                 vmem_limit_bytes=VMEM_LIMIT),
            name="swa",
        )(sinks[l].astype(f32), qa3, ka3, ka3, va_t3, va_t3, pos_r3, pos_r3)

        bq, nh = MLA_BQ, MLA_HEADS
        qb3 = qb.reshape(B, S, N_HEADS_B * LANES)
        kb3 = kb.reshape(B, S, N_HEADS_B * LANES)
        out_b = pl.pallas_call(
            _mla_kernel,
            grid=(B, N_HEADS_B // nh, S // bq),
            in_specs=[pl.BlockSpec((1, bq, nh * LANES), lambda b, j, i: (b, i, j)),
                      pl.BlockSpec((1, S, nh * LANES), lambda b, j, i: (b, 0, j)),
                      pl.BlockSpec((1, nkb, nh * LANES, bq), lambda b, j, i: (b, 0, j, 0))],
            out_specs=pl.BlockSpec((1, bq, nh * V_DIM_B), lambda b, j, i: (b, i, j)),
            out_shape=jax.ShapeDtypeStruct((B, S, WIDTH_B), bf16),
            scratch_shapes=[pltpu.VMEM((2, nh, bq, bq), f32),
                            pltpu.VMEM((2, nh, 1, bq), f32),
                            pltpu.VMEM((nh, LANES, bq), f32), pltpu.VMEM((nh, 1, bq), f32)],
            compiler_params=pltpu.CompilerParams(
                dimension_semantics=("arbitrary", "arbitrary", "arbitrary"),
                vmem_limit_bytes=VMEM_LIMIT),
            name="mla",
        )(qb3, kb3, vt)

        tm = MERGE_TM
        tok = lambda w: pl.BlockSpec((tm, w), lambda i: (i, 0))
        x2 = pl.pallas_call(
            _merge_kernel,
            grid=(T // tm,),
            in_specs=[tok(D), tok(WIDTH_A), tok(WIDTH_B), _const_spec((1, D)),
                      w_rows(2 * D, IN_GATES), _const_spec((WIDTH_A, D)), _const_spec((WIDTH_B, D)),
                      _const_spec((D, D)), _const_spec((1, D)), _const_spec((1, D)),
                      _const_spec((D, D_FF)), _const_spec((D_FF, D)), _const_spec((1, D))],
            out_specs=tok(D),
            out_shape=jax.ShapeDtypeStruct((T, D), f32),
            compiler_params=pltpu.CompilerParams(dimension_semantics=("arbitrary",),
                                                 vmem_limit_bytes=VMEM_LIMIT),
            name="merge_mlp",
        )(x2, out_a.reshape(T, WIDTH_A), out_b.reshape(T, WIDTH_B), row(pre_norm_mix[l]), wit,
          w_o_a[l].astype(bf16), w_o_b[l].astype(bf16), w_out[l].astype(bf16),
          row(post_norm_mix[l]), row(pre_norm_mlp[l]), w_up[l].astype(bf16),
          w_down[l].astype(bf16), row(post_norm_mlp[l]))
        x = x2.reshape(B, S, D)
    return x
```

```python
import functools

import jax
import jax.numpy as jnp
from jax import lax
from jax.experimental import pallas as pl
from jax.experimental.pallas import tpu as pltpu

D_MODEL = 1024
N_HEADS_A = 8
N_KV_A = 2
HEAD_DIM_A = 64
WINDOW = 128
BLOCK = 128
N_HEADS_B = 8
QK_NOPE = 64
QK_ROPE = 32
V_DIM_B = 64
Q_LORA = 256
KV_LORA = 128
ROPE_THETA = 10000.0
D_FF = 4 * D_MODEL
EPS = 1e-6

WIDTH_A = N_HEADS_A * HEAD_DIM_A
WIDTH_B = N_HEADS_B * V_DIM_B
Q_HEAD_B = QK_NOPE + QK_ROPE
KV_HEAD_B = QK_NOPE + V_DIM_B

LANES = 128
HALF = LANES // 2
SCALE_A = HEAD_DIM_A ** -0.5
SCALE_B = Q_HEAD_B ** -0.5
LOG2_E = 1.4426950408889634
QSCALE_A = SCALE_A * LOG2_E
QSCALE_B = SCALE_B * LOG2_E
ALIBI_SLOPES = tuple(2.0 ** (-8.0 * (h + 1) / N_HEADS_A) for h in range(N_HEADS_A))

IN_GATES = 0
IN_QA = 2 * D_MODEL
IN_KA = IN_QA + WIDTH_A
IN_VA = IN_KA + N_KV_A * HEAD_DIM_A
IN_CQ = IN_VA + N_KV_A * HEAD_DIM_A
IN_CKV = IN_CQ + Q_LORA
IN_KR = IN_CKV + KV_LORA
D_IN = IN_KR + QK_ROPE
SM_KA = 0
SM_KR = SM_KA + 2 * LANES
SM_KRS = SM_KR + LANES
SM_ROWS = SM_KRS + LANES

PROJ_TM = 1024
PROJ_SUB = 512
SWA_TQ = 512
SWA_LEAD = 4
MLA_BQ = 512
MLA_CHUNK = 256
MLA_HEADS = 4
MLA_LEAD = 1
MERGE_TM = 512
FF_CHUNK = 1024
VMEM_LIMIT = 60 * 1024 * 1024


def _rms(v):
    return v * lax.rsqrt(jnp.mean(v * v, axis=-1, keepdims=True) + EPS)


def _dot(a, b):
    return jnp.dot(a, b, preferred_element_type=jnp.float32)


def _dot_nt(a, b):
    return lax.dot_general(a, b, (((1,), (1,)), ((), ())), preferred_element_type=jnp.float32)


def _proj_kernel(x_ref, pos_ref, gpre_ref, wqa_ref, wcq_ref, wckv_ref, wsm_ref, wvat_ref, gq_ref,
                 wqm_ref, wqs_ref, gkv_ref, wkk_ref, wkvt_ref, freq_ref,
                 qa_ref, ka_ref, vat_ref, qb_ref, kb_ref, vt_ref):
    bf16 = jnp.bfloat16
    sub = PROJ_SUB

    def first_stage(n):
        rows = slice(n * sub, (n + 1) * sub)
        hb = (_rms(x_ref[rows, :]) * gpre_ref[...]).astype(bf16)
        qa_ref[rows, :] = (_dot_nt(hb, wqa_ref[...]) * QSCALE_A).astype(bf16)
        small = _dot_nt(hb, wsm_ref[...])
        ka_ref[rows, :] = small[:, SM_KA:SM_KR].astype(bf16)
        va_t = _dot_nt(wvat_ref[...], hb)
        varow = lax.broadcasted_iota(jnp.int32, va_t.shape, 0)
        vat_ref[0, :, rows] = jnp.where(varow % LANES == HEAD_DIM_A, 1.0, va_t).astype(bf16)
        cq = _dot_nt(hb, wcq_ref[...])
        ckv = _dot_nt(hb, wckv_ref[...])
        pos = pos_ref[0, :, rows].astype(jnp.float32)
        ang = freq_ref[...] * pos
        cos_t, sin_t = jnp.cos(ang), jnp.sin(ang)
        one = jnp.ones((QK_NOPE, sub), jnp.float32)
        zero = jnp.zeros((QK_NOPE, sub), jnp.float32)
        pad = LANES - Q_HEAD_B
        cos = jnp.concatenate([one, cos_t, cos_t, one[:pad]], axis=0).T
        sin = jnp.concatenate([zero, -sin_t, sin_t, zero[:pad]], axis=0).T
        k_rot = small[:, SM_KR:SM_KRS] * cos + small[:, SM_KRS:SM_ROWS] * sin
        return cq, ckv, cos, sin, k_rot

    def second_stage(n, cq, ckv, cos, sin, k_rot):
        rows = slice(n * sub, (n + 1) * sub)
        cqn = (_rms(cq) * gq_ref[...]).astype(bf16)
        q_main = _dot(cqn, wqm_ref[...])
        q_swap = _dot(cqn, wqs_ref[...])
        ckvn = (_rms(ckv) * gkv_ref[...]).astype(bf16)
        k_nope = _dot(ckvn, wkk_ref[...])
        v_t = _dot_nt(wkvt_ref[...], ckvn)
        vrow = lax.broadcasted_iota(jnp.int32, v_t.shape, 0)
        vt_ref[0, n] = jnp.where(vrow % LANES == V_DIM_B, 1.0, v_t).astype(bf16)
        for h in range(N_HEADS_B):
            sl = slice(h * LANES, (h + 1) * LANES)
            qb_ref[rows, sl] = ((q_main[:, sl] * cos + q_swap[:, sl] * sin) * QSCALE_B).astype(bf16)
            kb_ref[rows, sl] = (k_nope[:, sl] + k_rot).astype(bf16)

    n_sub = x_ref.shape[0] // sub
    staged = [first_stage(n) for n in range(n_sub)]
    for n in range(n_sub):
        second_stage(n, *staged[n])


def _swa_kernel(sink_ref, q_ref, kc_ref, kp_ref, vtc_ref, vtp_ref, posc_ref, posp_ref, o_ref):
    bf16 = jnp.bfloat16
    i = pl.program_id(1)
    krow = lax.broadcasted_iota(jnp.int32, (2 * BLOCK, BLOCK), 0)
    qcol = lax.broadcasted_iota(jnp.int32, (2 * BLOCK, BLOCK), 1)
    ahead = krow - qcol
    band = (ahead > 0) & (ahead <= WINDOW)
    lane = lax.broadcasted_iota(jnp.int32, (BLOCK, LANES), 1)
    low_half = lane < HALF
    pos_inf = jnp.float32(jnp.inf)

    n_pairs = N_HEADS_A // 2
    chains = [(blk, pair) for blk in range(SWA_TQ // BLOCK) for pair in range(n_pairs)]
    bands = {}

    def band_of(blk):
        if blk not in bands:
            r0 = blk * BLOCK
            if blk == 0:
                kband = jnp.concatenate([kp_ref[0], kc_ref[0, 0:BLOCK, :]], axis=0)
                vtband = jnp.concatenate([vtp_ref[0], vtc_ref[0, :, 0:BLOCK]], axis=1)
                kpos = jnp.concatenate([posp_ref[0], posc_ref[0, :, 0:BLOCK]], axis=1)
                mask = band & ((krow >= BLOCK) | (i > 0))
            else:
                kband = kc_ref[0, r0 - BLOCK:r0 + BLOCK, :]
                vtband = vtc_ref[0, :, r0 - BLOCK:r0 + BLOCK]
                kpos = posc_ref[0, :, r0 - BLOCK:r0 + BLOCK]
                mask = band
            qpos = posc_ref[0, :, r0:r0 + BLOCK]
            kpos_col = jnp.broadcast_to(kpos, (BLOCK, 2 * BLOCK)).T
            dist = jnp.abs(kpos_col - qpos).astype(jnp.float32) * LOG2_E
            dist = jnp.where(mask, dist, pos_inf)
            bands[blk] = (kband, vtband, dist)
        return bands[blk]

    def scores(blk, pair):
        kv = (2 * pair) // (N_HEADS_A // N_KV_A)
        r0 = blk * BLOCK
        qp = q_ref[0, r0:r0 + BLOCK, pair * LANES:(pair + 1) * LANES]
        zero = jnp.zeros_like(qp)
        q2 = jnp.concatenate([jnp.where(low_half, qp, zero), jnp.where(low_half, zero, qp)],
                             axis=0)
        kx = band_of(blk)[0][:, kv * LANES:(kv + 1) * LANES]
        return _dot_nt(kx, q2)

    def finish(blk, pair, s2):
        kv = (2 * pair) // (N_HEADS_A // N_KV_A)
        r0 = blk * BLOCK
        _, vtband, dist = band_of(blk)
        vt = vtband[kv * LANES:(kv + 1) * LANES, :]
        ps, ms = [], []
        for e in range(2):
            h = 2 * pair + e
            s = s2[:, e * BLOCK:(e + 1) * BLOCK] - ALIBI_SLOPES[h] * dist
            m = jnp.maximum(jnp.max(s, axis=0, keepdims=True), sink_ref[h] * LOG2_E)
            ps.append(jnp.exp2(s - m).astype(bf16))
            ms.append(m)
        o2 = _dot(vt, jnp.concatenate(ps, axis=1))
        outs = []
        for e in range(2):
            h = 2 * pair + e
            o_t = o2[:, e * BLOCK:(e + 1) * BLOCK]
            denom = (o_t[HEAD_DIM_A:HEAD_DIM_A + 1]
                     + jnp.exp2(sink_ref[h] * LOG2_E - ms[e]))
            outs.append(o_t[0:HEAD_DIM_A] / denom)
        o_ref[0, r0:r0 + BLOCK, pair * LANES:(pair + 1) * LANES] = (
            jnp.concatenate(outs, axis=0).T.astype(bf16))

    pending = [scores(*c) for c in chains[:SWA_LEAD]]
    for n, c in enumerate(chains):
        s2 = pending.pop(0)
        if n + SWA_LEAD < len(chains):
            pending.append(scores(*chains[n + SWA_LEAD]))
        finish(*c, s2)


def _mla_kernel(q_ref, k_ref, vt_ref, o_ref, s_ref, bmax_ref, acc_ref, m_ref):
    bf16 = jnp.bfloat16
    bq = MLA_BQ
    ch = MLA_CHUNK
    qi = pl.program_id(2)
    krow = lax.broadcasted_iota(jnp.int32, (bq, ch), 0)
    qcol = lax.broadcasted_iota(jnp.int32, (bq, ch), 1)
    neg_inf = jnp.float32(-jnp.inf)
    m_ref[...] = jnp.full(m_ref.shape, neg_inf, jnp.float32)
    acc_ref[...] = jnp.zeros(acc_ref.shape, jnp.float32)

    def scores_head(buf, kb, e):
        start = pl.multiple_of(kb * bq, bq)
        sl = slice(e * LANES, (e + 1) * LANES)
        s = _dot_nt(k_ref[0, pl.ds(start, bq), sl], q_ref[0, :, sl])
        s_ref[buf, e] = s
        bmax_ref[buf, e] = jnp.max(s, axis=0, keepdims=True)

    def scores_into(buf, kb):
        for e in range(MLA_HEADS):
            scores_head(buf, kb, e)

    def softmax_pv_head(buf, kb, e, masked):
        vt = vt_ref[0, kb, e * LANES:(e + 1) * LANES, :]
        for c in range(bq // ch):
            cs = slice(c * ch, (c + 1) * ch)
            s = s_ref[buf, e, :, cs]
            if masked:
                s = jnp.where(krow <= qcol + c * ch, s, neg_inf)
                blockmax = jnp.max(s, axis=0, keepdims=True)
            else:
                blockmax = bmax_ref[buf, e, :, cs]
            m_prev = m_ref[e, :, cs]
            m_new = jnp.maximum(m_prev, blockmax)
            p = jnp.exp2(s - m_new).astype(bf16)
            alpha = jnp.exp2(m_prev - m_new)
            acc_ref[e, :, cs] = alpha * acc_ref[e, :, cs] + _dot(vt, p)
            m_ref[e, :, cs] = m_new

    def softmax_pv(buf, kb, masked):
        for e in range(MLA_HEADS):
            softmax_pv_head(buf, kb, e, masked)

    def pipelined(steps):
        units = [(src, kb, e) for (src, _, kb) in steps for e in range(MLA_HEADS)]
        prods = [(dst, kb + 1, e) for (_, dst, kb) in steps for e in range(MLA_HEADS)]
        for n in range(min(MLA_LEAD, len(prods))):
            scores_head(*prods[n])
        for n, (src, kb, e) in enumerate(units):
            softmax_pv_head(src, kb, e, False)
            if n + MLA_LEAD < len(prods):
                scores_head(*prods[n + MLA_LEAD])

    def half(src, dst, kb):
        pipelined([(src, dst, kb)])

    def quad_body(t, carry):
        pipelined([(0, 1, 4 * t), (1, 0, 4 * t + 1), (0, 1, 4 * t + 2), (1, 0, 4 * t + 3)])
        return carry

    scores_into(0, 0)
    lax.fori_loop(0, qi // 4, quad_body, 0)

    @pl.when(qi % 4 >= 2)
    def _():
        base = (qi // 4) * 4
        pipelined([(0, 1, base), (1, 0, base + 1)])

    @pl.when(qi % 2 == 1)
    def _():
        half(0, 1, qi - 1)
        softmax_pv(1, qi, True)

    @pl.when(qi % 2 == 0)
    def _():
        softmax_pv(0, qi, True)

    outs = []
    for e in range(MLA_HEADS):
        acc = acc_ref[e]
        outs.append(acc[0:V_DIM_B] / acc[V_DIM_B:V_DIM_B + 1])
    o_ref[0] = jnp.concatenate(outs, axis=0).T.astype(bf16)


def _merge_kernel(x_ref, oa_ref, ob_ref, gpre_ref, wg_ref, woa_ref, wob_ref, wout_ref, gpost_ref,
                  gpre2_ref, wup_ref, wdn_ref, gpost2_ref, o_ref):
    bf16 = jnp.bfloat16
    x = x_ref[...]
    hm = x.shape[0] // 2
    halves = (slice(0, hm), slice(hm, 2 * hm))
    hb = [(_rms(x[r]) * gpre_ref[...]).astype(bf16) for r in halves]
    gate_a = jnp.concatenate([_dot_nt(h, wg_ref[0:D_MODEL, :]) for h in hb], axis=0)
    hb = jnp.concatenate(hb, axis=0)
    gate_a = jax.nn.sigmoid(gate_a)
    gate_b = jax.nn.sigmoid(_dot_nt(hb, wg_ref[D_MODEL:2 * D_MODEL, :]))
    merged = gate_a * _dot(oa_ref[...], woa_ref[...]) + gate_b * _dot(ob_ref[...], wob_ref[...])
    mb = merged.astype(bf16)

    n_chunks = D_FF // FF_CHUNK
    chunk = lambda c: slice(c * FF_CHUNK, (c + 1) * FF_CHUNK)

    def sq_relu(v):
        v = jnp.maximum(v, 0.0)
        return (v * v).astype(bf16)

    x1, h2, up0 = [], [], []
    for r in halves:
        y = _dot(mb[r], wout_ref[...])
        x1.append(x[r] + _rms(y) * gpost_ref[...])
        h2.append((_rms(x1[-1]) * gpre2_ref[...]).astype(bf16))
    for i in range(2):
        up0.append(sq_relu(_dot(h2[i], wup_ref[:, chunk(0)])))
    h2 = jnp.concatenate(h2, axis=0)
    y2 = _dot(jnp.concatenate(up0, axis=0), wdn_ref[chunk(0), :])
    for c in range(1, n_chunks - 1):
        y2 = y2 + _dot(sq_relu(_dot(h2, wup_ref[:, chunk(c)])), wdn_ref[chunk(c), :])
    last = sq_relu(_dot(h2, wup_ref[:, chunk(n_chunks - 1)]))
    for i, r in enumerate(halves):
        y2_half = y2[r] + _dot(last[r], wdn_ref[chunk(n_chunks - 1), :])
        o_ref[r, :] = x1[i] + _rms(y2_half) * gpost2_ref[...]


def _const_spec(shape):
    return pl.BlockSpec(shape, lambda *_: (0,) * len(shape), pipeline_mode=pl.Buffered(1))


def kernel(x, positions, pre_norm_mix, w_in, q_a_norm, w_q_b, kv_a_norm, w_kv_b, sinks, w_o_a,
           w_o_b, w_out, post_norm_mix, pre_norm_mlp, w_up, w_down, post_norm_mlp):
    f32, bf16 = jnp.float32, jnp.bfloat16
    B, S, D = x.shape
    T = B * S
    depth = w_in.shape[0]
    for l in range(depth):
        assert w_in.shape[2] == D_IN
        wit = jnp.swapaxes(w_in[l], 0, 1).astype(bf16)
        ka_t = wit[IN_KA:IN_VA]
        va_t = wit[IN_VA:IN_CQ]
        kr_t = wit[IN_KR:D_IN]
        hr = QK_ROPE // 2
        hd = HEAD_DIM_A
        z = lambda n: jnp.zeros((n, D_MODEL), bf16)
        w_small_t = jnp.concatenate([
            ka_t[:hd], ka_t[:hd], ka_t[hd:], ka_t[hd:],
            z(QK_NOPE), kr_t, z(LANES - Q_HEAD_B),
            z(QK_NOPE), kr_t[hr:], kr_t[:hr], z(LANES - Q_HEAD_B)], axis=0)
        w_va_t = jnp.concatenate([va_t[:hd], z(LANES - hd), va_t[hd:], z(LANES - hd)], axis=0)

        wq = w_q_b[l].reshape(Q_LORA, N_HEADS_B, Q_HEAD_B)
        q_nope, q_rope = wq[..., :QK_NOPE], wq[..., QK_NOPE:]
        zq = lambda n: jnp.zeros((Q_LORA, N_HEADS_B, n), f32)
        wq_main = jnp.concatenate([q_nope, q_rope, zq(LANES - Q_HEAD_B)], -1)
        wq_swap = jnp.concatenate([zq(QK_NOPE), q_rope[..., hr:], q_rope[..., :hr],
                                   zq(LANES - Q_HEAD_B)], -1)
        wq_main = wq_main.reshape(Q_LORA, N_HEADS_B * LANES).astype(bf16)
        wq_swap = wq_swap.reshape(Q_LORA, N_HEADS_B * LANES).astype(bf16)

        wkv = w_kv_b[l].reshape(KV_LORA, N_HEADS_B, KV_HEAD_B)
        kv_k, kv_v = wkv[..., :QK_NOPE], wkv[..., QK_NOPE:]
        zk = jnp.zeros((KV_LORA, N_HEADS_B, HALF), f32)
        wkv_k = jnp.concatenate([kv_k, zk], -1).reshape(KV_LORA, N_HEADS_B * LANES).astype(bf16)
        wkv_vt = jnp.concatenate([kv_v, zk], -1).reshape(KV_LORA, N_HEADS_B * LANES).T.astype(bf16)

        freq_col = (ROPE_THETA ** (-jnp.arange(0, QK_ROPE, 2, dtype=f32) / QK_ROPE))[:, None]

        row = lambda g: g.reshape(1, -1).astype(f32)
        x2 = x.reshape(T, D)

        tm = PROJ_TM
        assert PROJ_SUB == MLA_BQ and tm % PROJ_SUB == 0 and S % tm == 0
        nkb = S // MLA_BQ
        n_sub = tm // PROJ_SUB
        npb = S // tm
        tok = lambda w: pl.BlockSpec((tm, w), lambda i: (i, 0))
        w_rows = lambda n, start: pl.BlockSpec((n, D), lambda i: (start // n, 0),
                                               pipeline_mode=pl.Buffered(1))
        assert IN_QA % WIDTH_A == 0 and IN_CQ % Q_LORA == 0 and IN_CKV % KV_LORA == 0
        qa, ka_x, va_t3, qb, kb, vt = pl.pallas_call(
            _proj_kernel,
            grid=(T // tm,),
            in_specs=[tok(D), pl.BlockSpec((1, 1, tm), lambda i: (i, 0, 0)), _const_spec((1, D)),
                      w_rows(WIDTH_A, IN_QA), w_rows(Q_LORA, IN_CQ), w_rows(KV_LORA, IN_CKV),
                      _const_spec((SM_ROWS, D)), _const_spec((N_KV_A * LANES, D)),
                      _const_spec((1, Q_LORA)), _const_spec((Q_LORA, N_HEADS_B * LANES)),
                      _const_spec((Q_LORA, N_HEADS_B * LANES)), _const_spec((1, KV_LORA)),
                      _const_spec((KV_LORA, N_HEADS_B * LANES)),
                      _const_spec((N_HEADS_B * LANES, KV_LORA)),
                      _const_spec((QK_ROPE // 2, 1))],
            out_specs=[tok(WIDTH_A), tok(2 * LANES),
                       pl.BlockSpec((1, N_KV_A * LANES, tm), lambda i: (i // npb, 0, i % npb)),
                       tok(N_HEADS_B * LANES), tok(N_HEADS_B * LANES),
                       pl.BlockSpec((1, n_sub, N_HEADS_B * LANES, PROJ_SUB),
                                    lambda i: (i // npb, i % npb, 0, 0))],
            out_shape=[jax.ShapeDtypeStruct((T, WIDTH_A), bf16),
                       jax.ShapeDtypeStruct((T, 2 * LANES), bf16),
                       jax.ShapeDtypeStruct((B, N_KV_A * LANES, S), bf16),
                       jax.ShapeDtypeStruct((T, N_HEADS_B * LANES), bf16),
                       jax.ShapeDtypeStruct((T, N_HEADS_B * LANES), bf16),
                       jax.ShapeDtypeStruct((B, nkb, N_HEADS_B * LANES, PROJ_SUB), bf16)],
            compiler_params=pltpu.CompilerParams(dimension_semantics=("arbitrary",),
                                                 vmem_limit_bytes=VMEM_LIMIT),
            name="proj",
        )(x2, positions.reshape(T // tm, 1, tm), row(pre_norm_mix[l]), wit, wit, wit, w_small_t,
          w_va_t, row(q_a_norm[l]), wq_main, wq_swap, row(kv_a_norm[l]), wkv_k, wkv_vt,
          freq_col)

        tq = SWA_TQ
        nb_per = tq // BLOCK
        qa3 = qa.reshape(B, S, WIDTH_A)
        ka3 = ka_x.reshape(B, S, 2 * LANES)
        pos_r3 = positions.reshape(B, 1, S)
        cur = lambda w: pl.BlockSpec((1, tq, w), lambda b, i: (b, i, 0))
        prev_blk = lambda b, i: jnp.maximum(i * nb_per - 1, 0)
        out_a = pl.pallas_call(
            _swa_kernel,
            grid=(B, S // tq),
            in_specs=[pl.BlockSpec(memory_space=pltpu.SMEM),
                      cur(WIDTH_A), cur(2 * LANES),
                      pl.BlockSpec((1, BLOCK, 2 * LANES), lambda b, i: (b, prev_blk(b, i), 0)),
                      pl.BlockSpec((1, N_KV_A * LANES, tq), lambda b, i: (b, 0, i)),
                      pl.BlockSpec((1, N_KV_A * LANES, BLOCK), lambda b, i: (b, 0, prev_blk(b, i))),
                      pl.BlockSpec((1, 1, tq), lambda b, i: (b, 0, i)),
                      pl.BlockSpec((1, 1, BLOCK), lambda b, i: (b, 0, prev_blk(b, i)))],
            out_specs=cur(WIDTH_A),
            out_shape=jax.ShapeDtypeStruct((B, S, WIDTH_A), bf16),
            compiler_params=pltpu.CompilerParams(dimension_semantics=("arbitrary", "arbitrary"),
                                                 vmem_limit_bytes=VMEM_LIMIT),
            name="swa",
        )(sinks[l].astype(f32), qa3, ka3, ka3, va_t3, va_t3, pos_r3, pos_r3)

        bq, nh = MLA_BQ, MLA_HEADS
        qb3 = qb.reshape(B, S, N_HEADS_B * LANES)
        kb3 = kb.reshape(B, S, N_HEADS_B * LANES)
        out_b = pl.pallas_call(
            _mla_kernel,
            grid=(B, N_HEADS_B // nh, S // bq),
            in_specs=[pl.BlockSpec((1, bq, nh * LANES), lambda b, j, i: (b, i, j)),
                      pl.BlockSpec((1, S, nh * LANES), lambda b, j, i: (b, 0, j)),
                      pl.BlockSpec((1, nkb, nh * LANES, bq), lambda b, j, i: (b, 0, j, 0))],
            out_specs=pl.BlockSpec((1, bq, nh * V_DIM_B), lambda b, j, i: (b, i, j)),
            out_shape=jax.ShapeDtypeStruct((B, S, WIDTH_B), bf16),
            scratch_shapes=[pltpu.VMEM((2, nh, bq, bq), f32),
                            pltpu.VMEM((2, nh, 1, bq), f32),
                            pltpu.VMEM((nh, LANES, bq), f32), pltpu.VMEM((nh, 1, bq), f32)],
            compiler_params=pltpu.CompilerParams(
                dimension_semantics=("arbitrary", "arbitrary", "arbitrary"),
                vmem_limit_bytes=VMEM_LIMIT),
            name="mla",
        )(qb3, kb3, vt)

        tm = MERGE_TM
        tok = lambda w: pl.BlockSpec((tm, w), lambda i: (i, 0))
        x2 = pl.pallas_call(
            _merge_kernel,
            grid=(T // tm,),
            in_specs=[tok(D), tok(WIDTH_A), tok(WIDTH_B), _const_spec((1, D)),
                      w_rows(2 * D, IN_GATES), _const_spec((WIDTH_A, D)), _const_spec((WIDTH_B, D)),
                      _const_spec((D, D)), _const_spec((1, D)), _const_spec((1, D)),
                      _const_spec((D, D_FF)), _const_spec((D_FF, D)), _const_spec((1, D))],
            out_specs=tok(D),
            out_shape=jax.ShapeDtypeStruct((T, D), f32),
            compiler_params=pltpu.CompilerParams(dimension_semantics=("arbitrary",),
                                                 vmem_limit_bytes=VMEM_LIMIT),
            name="merge_mlp",
        )(x2, out_a.reshape(T, WIDTH_A), out_b.reshape(T, WIDTH_B), row(pre_norm_mix[l]), wit,
          w_o_a[l].astype(bf16), w_o_b[l].astype(bf16), w_out[l].astype(bf16),
          row(post_norm_mix[l]), row(pre_norm_mlp[l]), w_up[l].astype(bf16),
          w_down[l].astype(bf16), row(post_norm_mlp[l]))
        x = x2.reshape(B, S, D)
    return x
```

```python
import functools

import jax
import jax.numpy as jnp
from jax import lax
from jax.experimental import pallas as pl
from jax.experimental.pallas import tpu as pltpu

D_MODEL = 1024
N_HEADS_A = 8
N_KV_A = 2
HEAD_DIM_A = 64
WINDOW = 128
BLOCK = 128
N_HEADS_B = 8
QK_NOPE = 64
QK_ROPE = 32
V_DIM_B = 64
Q_LORA = 256
KV_LORA = 128
ROPE_THETA = 10000.0
D_FF = 4 * D_MODEL
EPS = 1e-6

WIDTH_A = N_HEADS_A * HEAD_DIM_A
WIDTH_B = N_HEADS_B * V_DIM_B
Q_HEAD_B = QK_NOPE + QK_ROPE
KV_HEAD_B = QK_NOPE + V_DIM_B

LANES = 128
HALF = LANES // 2
BF16_ROWS = 16
VT_ROWS = V_DIM_B + BF16_ROWS
SCALE_A = HEAD_DIM_A ** -0.5
SCALE_B = Q_HEAD_B ** -0.5
LOG2_E = 1.4426950408889634
QSCALE_A = SCALE_A * LOG2_E
QSCALE_B = SCALE_B * LOG2_E
ALIBI_SLOPES = tuple(2.0 ** (-8.0 * (h + 1) / N_HEADS_A) for h in range(N_HEADS_A))

IN_GATES = 0
IN_QA = 2 * D_MODEL
IN_KA = IN_QA + WIDTH_A
IN_VA = IN_KA + N_KV_A * HEAD_DIM_A
IN_CQ = IN_VA + N_KV_A * HEAD_DIM_A
IN_CKV = IN_CQ + Q_LORA
IN_KR = IN_CKV + KV_LORA
D_IN = IN_KR + QK_ROPE
SM_KA = 0
SM_KR = SM_KA + 2 * LANES
SM_KRS = SM_KR + LANES
SM_ROWS = SM_KRS + LANES

PROJ_TM = 1024
PROJ_SUB = 512
SWA_TQ = 512
SWA_LEAD = 4
MLA_BQ = 512
MLA_CHUNK = 256
MLA_HEADS = 4
MLA_LEAD = 1
MERGE_TM = 512
FF_CHUNK = 1024
VMEM_LIMIT = 60 * 1024 * 1024


def _rms(v):
    return v * lax.rsqrt(jnp.mean(v * v, axis=-1, keepdims=True) + EPS)


def _dot(a, b):
    return jnp.dot(a, b, preferred_element_type=jnp.float32)


def _dot_nt(a, b):
    return lax.dot_general(a, b, (((1,), (1,)), ((), ())), preferred_element_type=jnp.float32)


def _proj_kernel(x_ref, pos_ref, gpre_ref, wqa_ref, wcq_ref, wckv_ref, wsm_ref, wvat_ref, gq_ref,
                 wqm_ref, wqs_ref, gkv_ref, wkk_ref, wkvt_ref, freq_ref,
                 qa_ref, ka_ref, vat_ref, qb_ref, kb_ref, vt_ref):
    bf16 = jnp.bfloat16
    sub = PROJ_SUB

    def first_stage(n):
        rows = slice(n * sub, (n + 1) * sub)
        hb = (_rms(x_ref[rows, :]) * gpre_ref[...]).astype(bf16)
        qa_ref[rows, :] = (_dot_nt(hb, wqa_ref[...]) * QSCALE_A).astype(bf16)
        small = _dot_nt(hb, wsm_ref[...])
        ka_ref[rows, :] = small[:, SM_KA:SM_KR].astype(bf16)
        va_t = _dot_nt(wvat_ref[...], hb)
        varow = lax.broadcasted_iota(jnp.int32, va_t.shape, 0)
        vat_ref[0, :, rows] = jnp.where(varow % VT_ROWS == HEAD_DIM_A, 1.0, va_t).astype(bf16)
        cq = _dot_nt(hb, wcq_ref[...])
        ckv = _dot_nt(hb, wckv_ref[...])
        pos = pos_ref[0, :, rows].astype(jnp.float32)
        ang = freq_ref[...] * pos
        cos_t, sin_t = jnp.cos(ang), jnp.sin(ang)
        one = jnp.ones((QK_NOPE, sub), jnp.float32)
        zero = jnp.zeros((QK_NOPE, sub), jnp.float32)
        pad = LANES - Q_HEAD_B
        cos = jnp.concatenate([one, cos_t, cos_t, one[:pad]], axis=0).T
        sin = jnp.concatenate([zero, -sin_t, sin_t, zero[:pad]], axis=0).T
        k_rot = small[:, SM_KR:SM_KRS] * cos + small[:, SM_KRS:SM_ROWS] * sin
        return cq, ckv, cos, sin, k_rot

    def second_stage(n, cq, ckv, cos, sin, k_rot):
        rows = slice(n * sub, (n + 1) * sub)
        cqn = (_rms(cq) * gq_ref[...]).astype(bf16)
        q_main = _dot(cqn, wqm_ref[...])
        q_swap = _dot(cqn, wqs_ref[...])
        ckvn = (_rms(ckv) * gkv_ref[...]).astype(bf16)
        k_nope = _dot(ckvn, wkk_ref[...])
        v_t = _dot_nt(wkvt_ref[...], ckvn)
        vrow = lax.broadcasted_iota(jnp.int32, v_t.shape, 0)
        vt_ref[0, n] = jnp.where(vrow % VT_ROWS == V_DIM_B, 1.0, v_t).astype(bf16)
        for h in range(N_HEADS_B):
            sl = slice(h * LANES, (h + 1) * LANES)
            qb_ref[rows, sl] = ((q_main[:, sl] * cos + q_swap[:, sl] * sin) * QSCALE_B).astype(bf16)
            kb_ref[rows, sl] = (k_nope[:, sl] + k_rot).astype(bf16)

    n_sub = x_ref.shape[0] // sub
    staged = [first_stage(n) for n in range(n_sub)]
    for n in range(n_sub):
        second_stage(n, *staged[n])


def _swa_kernel(sink_ref, q_ref, kc_ref, kp_ref, vtc_ref, vtp_ref, posc_ref, posp_ref, o_ref):
    bf16 = jnp.bfloat16
    i = pl.program_id(1)
    krow = lax.broadcasted_iota(jnp.int32, (2 * BLOCK, BLOCK), 0)
    qcol = lax.broadcasted_iota(jnp.int32, (2 * BLOCK, BLOCK), 1)
    ahead = krow - qcol
    band = (ahead > 0) & (ahead <= WINDOW)
    lane = lax.broadcasted_iota(jnp.int32, (BLOCK, LANES), 1)
    low_half = lane < HALF
    pos_inf = jnp.float32(jnp.inf)

    n_pairs = N_HEADS_A // 2
    chains = [(blk, pair) for blk in range(SWA_TQ // BLOCK) for pair in range(n_pairs)]
    bands = {}

    def band_of(blk):
        if blk not in bands:
            r0 = blk * BLOCK
            if blk == 0:
                kband = jnp.concatenate([kp_ref[0], kc_ref[0, 0:BLOCK, :]], axis=0)
                vtband = jnp.concatenate([vtp_ref[0], vtc_ref[0, :, 0:BLOCK]], axis=1)
                kpos = jnp.concatenate([posp_ref[0], posc_ref[0, :, 0:BLOCK]], axis=1)
                mask = band & ((krow >= BLOCK) | (i > 0))
            else:
                kband = kc_ref[0, r0 - BLOCK:r0 + BLOCK, :]
                vtband = vtc_ref[0, :, r0 - BLOCK:r0 + BLOCK]
                kpos = posc_ref[0, :, r0 - BLOCK:r0 + BLOCK]
                mask = band
            qpos = posc_ref[0, :, r0:r0 + BLOCK]
            kpos_col = jnp.broadcast_to(kpos, (BLOCK, 2 * BLOCK)).T
            dist = jnp.abs(kpos_col - qpos).astype(jnp.float32) * LOG2_E
            dist = jnp.where(mask, dist, pos_inf)
            bands[blk] = (kband, vtband, dist)
        return bands[blk]

    def scores(blk, pair):
        kv = (2 * pair) // (N_HEADS_A // N_KV_A)
        r0 = blk * BLOCK
        qp = q_ref[0, r0:r0 + BLOCK, pair * LANES:(pair + 1) * LANES]
        zero = jnp.zeros_like(qp)
        q2 = jnp.concatenate([jnp.where(low_half, qp, zero), jnp.where(low_half, zero, qp)],
                             axis=0)
        kx = band_of(blk)[0][:, kv * LANES:(kv + 1) * LANES]
        return _dot_nt(kx, q2)

    def finish(blk, pair, s2):
        kv = (2 * pair) // (N_HEADS_A // N_KV_A)
        r0 = blk * BLOCK
        _, vtband, dist = band_of(blk)
        vt = vtband[kv * VT_ROWS:(kv + 1) * VT_ROWS, :]
        ps, ms = [], []
        for e in range(2):
            h = 2 * pair + e
            s = s2[:, e * BLOCK:(e + 1) * BLOCK] - ALIBI_SLOPES[h] * dist
            m = jnp.maximum(jnp.max(s, axis=0, keepdims=True), sink_ref[h] * LOG2_E)
            ps.append(jnp.exp2(s - m).astype(bf16))
            ms.append(m)
        o2 = _dot(vt, jnp.concatenate(ps, axis=1))
        outs = []
        for e in range(2):
            h = 2 * pair + e
            o_t = o2[:, e * BLOCK:(e + 1) * BLOCK]
            denom = (o_t[HEAD_DIM_A:HEAD_DIM_A + 1]
                     + jnp.exp2(sink_ref[h] * LOG2_E - ms[e]))
            outs.append(o_t[0:HEAD_DIM_A] / denom)
        o_ref[0, r0:r0 + BLOCK, pair * LANES:(pair + 1) * LANES] = (
            jnp.concatenate(outs, axis=0).T.astype(bf16))

    pending = [scores(*c) for c in chains[:SWA_LEAD]]
    for n, c in enumerate(chains):
        s2 = pending.pop(0)
        if n + SWA_LEAD < len(chains):
            pending.append(scores(*chains[n + SWA_LEAD]))
        finish(*c, s2)


def _mla_kernel(q_ref, k_ref, vt_ref, o_ref, s_ref, bmax_ref, acc_ref, m_ref):
    bf16 = jnp.bfloat16
    bq = MLA_BQ
    ch = MLA_CHUNK
    qi = pl.program_id(2)
    krow = lax.broadcasted_iota(jnp.int32, (bq, ch), 0)
    qcol = lax.broadcasted_iota(jnp.int32, (bq, ch), 1)
    neg_inf = jnp.float32(-jnp.inf)
    m_ref[...] = jnp.full(m_ref.shape, neg_inf, jnp.float32)
    acc_ref[...] = jnp.zeros(acc_ref.shape, jnp.float32)

    def scores_head(buf, kb, e):
        start = pl.multiple_of(kb * bq, bq)
        sl = slice(e * LANES, (e + 1) * LANES)
        s = _dot_nt(k_ref[0, pl.ds(start, bq), sl], q_ref[0, :, sl])
        s_ref[buf, e] = s
        bmax_ref[buf, e] = jnp.max(s, axis=0, keepdims=True)

    def scores_into(buf, kb):
        for e in range(MLA_HEADS):
            scores_head(buf, kb, e)

    def softmax_head(buf, e, masked):
        out = []
        for c in range(bq // ch):
            cs = slice(c * ch, (c + 1) * ch)
            s = s_ref[buf, e, :, cs]
            if masked:
                s = jnp.where(krow <= qcol + c * ch, s, neg_inf)
                blockmax = jnp.max(s, axis=0, keepdims=True)
            else:
                blockmax = bmax_ref[buf, e, :, cs]
            m_prev = m_ref[e, :, cs]
            m_new = jnp.maximum(m_prev, blockmax)
            m_ref[e, :, cs] = m_new
            out.append((jnp.exp2(s - m_new).astype(bf16), jnp.exp2(m_prev - m_new)))
        return out

    def pv_head(kb, e, weights):
        vt = vt_ref[0, kb, e * VT_ROWS:(e + 1) * VT_ROWS, :]
        for c, (p, alpha) in enumerate(weights):
            cs = slice(c * ch, (c + 1) * ch)
            acc_ref[e, :, cs] = alpha * acc_ref[e, :, cs] + _dot(vt, p)

    def softmax_pv(buf, kb, masked):
        for e in range(MLA_HEADS):
            pv_head(kb, e, softmax_head(buf, e, masked))

    def pipelined(steps):
        units = [(src, kb, e) for (src, _, kb) in steps for e in range(MLA_HEADS)]
        prods = [(dst, kb + 1, e) for (_, dst, kb) in steps for e in range(MLA_HEADS)]
        for n in range(min(MLA_LEAD, len(prods))):
            scores_head(*prods[n])
        for n, (src, kb, e) in enumerate(units):
            pv_head(kb, e, softmax_head(src, e, False))
            if n + MLA_LEAD < len(prods):
                scores_head(*prods[n + MLA_LEAD])

    def half(src, dst, kb):
        pipelined([(src, dst, kb)])

    def quad_body(t, carry):
        pipelined([(0, 1, 4 * t), (1, 0, 4 * t + 1), (0, 1, 4 * t + 2), (1, 0, 4 * t + 3)])
        return carry

    scores_into(0, 0)
    lax.fori_loop(0, qi // 4, quad_body, 0)

    @pl.when(qi % 4 >= 2)
    def _():
        base = (qi // 4) * 4
        pipelined([(0, 1, base), (1, 0, base + 1)])

    @pl.when(qi % 2 == 1)
    def _():
        half(0, 1, qi - 1)
        softmax_pv(1, qi, True)

    @pl.when(qi % 2 == 0)
    def _():
        softmax_pv(0, qi, True)

    outs = []
    for e in range(MLA_HEADS):
        acc = acc_ref[e]
        outs.append(acc[0:V_DIM_B] / acc[V_DIM_B:V_DIM_B + 1])
    o_ref[0] = jnp.concatenate(outs, axis=0).T.astype(bf16)


def _merge_kernel(x_ref, oa_ref, ob_ref, gpre_ref, wg_ref, woa_ref, wob_ref, wout_ref, gpost_ref,
                  gpre2_ref, wup_ref, wdn_ref, gpost2_ref, o_ref):
    bf16 = jnp.bfloat16
    x = x_ref[...]
    hm = x.shape[0] // 2
    halves = (slice(0, hm), slice(hm, 2 * hm))
    hb = [(_rms(x[r]) * gpre_ref[...]).astype(bf16) for r in halves]
    gate_a = jnp.concatenate([_dot_nt(h, wg_ref[0:D_MODEL, :]) for h in hb], axis=0)
    hb = jnp.concatenate(hb, axis=0)
    gate_a = jax.nn.sigmoid(gate_a)
    gate_b = jax.nn.sigmoid(_dot_nt(hb, wg_ref[D_MODEL:2 * D_MODEL, :]))
    merged = gate_a * _dot(oa_ref[...], woa_ref[...]) + gate_b * _dot(ob_ref[...], wob_ref[...])
    mb = merged.astype(bf16)

    n_chunks = D_FF // FF_CHUNK
    chunk = lambda c: slice(c * FF_CHUNK, (c + 1) * FF_CHUNK)

    def sq_relu(v):
        v = jnp.maximum(v, 0.0)
        return (v * v).astype(bf16)

    x1, h2, up0 = [], [], []
    for r in halves:
        y = _dot(mb[r], wout_ref[...])
        x1.append(x[r] + _rms(y) * gpost_ref[...])
        h2.append((_rms(x1[-1]) * gpre2_ref[...]).astype(bf16))
    for i in range(2):
        up0.append(sq_relu(_dot(h2[i], wup_ref[:, chunk(0)])))
    h2 = jnp.concatenate(h2, axis=0)
    y2 = _dot(jnp.concatenate(up0, axis=0), wdn_ref[chunk(0), :])
    for c in range(1, n_chunks - 1):
        y2 = y2 + _dot(sq_relu(_dot(h2, wup_ref[:, chunk(c)])), wdn_ref[chunk(c), :])
    last = sq_relu(_dot(h2, wup_ref[:, chunk(n_chunks - 1)]))
    for i, r in enumerate(halves):
        y2_half = y2[r] + _dot(last[r], wdn_ref[chunk(n_chunks - 1), :])
        o_ref[r, :] = x1[i] + _rms(y2_half) * gpost2_ref[...]


def _const_spec(shape):
    return pl.BlockSpec(shape, lambda *_: (0,) * len(shape), pipeline_mode=pl.Buffered(1))


def kernel(x, positions, pre_norm_mix, w_in, q_a_norm, w_q_b, kv_a_norm, w_kv_b, sinks, w_o_a,
           w_o_b, w_out, post_norm_mix, pre_norm_mlp, w_up, w_down, post_norm_mlp):
    f32, bf16 = jnp.float32, jnp.bfloat16
    B, S, D = x.shape
    T = B * S
    depth = w_in.shape[0]
    for l in range(depth):
        assert w_in.shape[2] == D_IN
        wit = jnp.swapaxes(w_in[l], 0, 1).astype(bf16)
        ka_t = wit[IN_KA:IN_VA]
        va_t = wit[IN_VA:IN_CQ]
        kr_t = wit[IN_KR:D_IN]
        hr = QK_ROPE // 2
        hd = HEAD_DIM_A
        z = lambda n: jnp.zeros((n, D_MODEL), bf16)
        w_small_t = jnp.concatenate([
            ka_t[:hd], ka_t[:hd], ka_t[hd:], ka_t[hd:],
            z(QK_NOPE), kr_t, z(LANES - Q_HEAD_B),
            z(QK_NOPE), kr_t[hr:], kr_t[:hr], z(LANES - Q_HEAD_B)], axis=0)
        w_va_t = jnp.concatenate([va_t[:hd], z(BF16_ROWS), va_t[hd:], z(BF16_ROWS)], axis=0)

        wq = w_q_b[l].reshape(Q_LORA, N_HEADS_B, Q_HEAD_B)
        q_nope, q_rope = wq[..., :QK_NOPE], wq[..., QK_NOPE:]
        zq = lambda n: jnp.zeros((Q_LORA, N_HEADS_B, n), f32)
        wq_main = jnp.concatenate([q_nope, q_rope, zq(LANES - Q_HEAD_B)], -1)
        wq_swap = jnp.concatenate([zq(QK_NOPE), q_rope[..., hr:], q_rope[..., :hr],
                                   zq(LANES - Q_HEAD_B)], -1)
        wq_main = wq_main.reshape(Q_LORA, N_HEADS_B * LANES).astype(bf16)
        wq_swap = wq_swap.reshape(Q_LORA, N_HEADS_B * LANES).astype(bf16)

        wkv = w_kv_b[l].reshape(KV_LORA, N_HEADS_B, KV_HEAD_B)
        kv_k, kv_v = wkv[..., :QK_NOPE], wkv[..., QK_NOPE:]
        zk = jnp.zeros((KV_LORA, N_HEADS_B, HALF), f32)
        wkv_k = jnp.concatenate([kv_k, zk], -1).reshape(KV_LORA, N_HEADS_B * LANES).astype(bf16)
        wkv_vt = jnp.concatenate([kv_v, zk[..., :BF16_ROWS]], -1)
        wkv_vt = wkv_vt.reshape(KV_LORA, N_HEADS_B * VT_ROWS).T.astype(bf16)

        freq_col = (ROPE_THETA ** (-jnp.arange(0, QK_ROPE, 2, dtype=f32) / QK_ROPE))[:, None]

        row = lambda g: g.reshape(1, -1).astype(f32)
        x2 = x.reshape(T, D)

        tm = PROJ_TM
        assert PROJ_SUB == MLA_BQ and tm % PROJ_SUB == 0 and S % tm == 0
        nkb = S // MLA_BQ
        n_sub = tm // PROJ_SUB
        npb = S // tm
        tok = lambda w: pl.BlockSpec((tm, w), lambda i: (i, 0))
        w_rows = lambda n, start: pl.BlockSpec((n, D), lambda i: (start // n, 0),
                                               pipeline_mode=pl.Buffered(1))
        assert IN_QA % WIDTH_A == 0 and IN_CQ % Q_LORA == 0 and IN_CKV % KV_LORA == 0
        qa, ka_x, va_t3, qb, kb, vt = pl.pallas_call(
            _proj_kernel,
            grid=(T // tm,),
            in_specs=[tok(D), pl.BlockSpec((1, 1, tm), lambda i: (i, 0, 0)), _const_spec((1, D)),
                      w_rows(WIDTH_A, IN_QA), w_rows(Q_LORA, IN_CQ), w_rows(KV_LORA, IN_CKV),
                      _const_spec((SM_ROWS, D)), _const_spec((N_KV_A * VT_ROWS, D)),
                      _const_spec((1, Q_LORA)), _const_spec((Q_LORA, N_HEADS_B * LANES)),
                      _const_spec((Q_LORA, N_HEADS_B * LANES)), _const_spec((1, KV_LORA)),
                      _const_spec((KV_LORA, N_HEADS_B * LANES)),
                      _const_spec((N_HEADS_B * VT_ROWS, KV_LORA)),
                      _const_spec((QK_ROPE // 2, 1))],
            out_specs=[tok(WIDTH_A), tok(2 * LANES),
                       pl.BlockSpec((1, N_KV_A * VT_ROWS, tm), lambda i: (i // npb, 0, i % npb)),
                       tok(N_HEADS_B * LANES), tok(N_HEADS_B * LANES),
                       pl.BlockSpec((1, n_sub, N_HEADS_B * VT_ROWS, PROJ_SUB),
                                    lambda i: (i // npb, i % npb, 0, 0))],
            out_shape=[jax.ShapeDtypeStruct((T, WIDTH_A), bf16),
                       jax.ShapeDtypeStruct((T, 2 * LANES), bf16),
                       jax.ShapeDtypeStruct((B, N_KV_A * VT_ROWS, S), bf16),
                       jax.ShapeDtypeStruct((T, N_HEADS_B * LANES), bf16),
                       jax.ShapeDtypeStruct((T, N_HEADS_B * LANES), bf16),
                       jax.ShapeDtypeStruct((B, nkb, N_HEADS_B * VT_ROWS, PROJ_SUB), bf16)],
            compiler_params=pltpu.CompilerParams(dimension_semantics=("arbitrary",),
                                                 vmem_limit_bytes=VMEM_LIMIT),
            name="proj",
        )(x2, positions.reshape(T // tm, 1, tm), row(pre_norm_mix[l]), wit, wit, wit, w_small_t,
          w_va_t, row(q_a_norm[l]), wq_main, wq_swap, row(kv_a_norm[l]), wkv_k, wkv_vt,
          freq_col)

        tq = SWA_TQ
        nb_per = tq // BLOCK
        qa3 = qa.reshape(B, S, WIDTH_A)
        ka3 = ka_x.reshape(B, S, 2 * LANES)
        pos_r3 = positions.reshape(B, 1, S)
        cur = lambda w: pl.BlockSpec((1, tq, w), lambda b, i: (b, i, 0))
        prev_blk = lambda b, i: jnp.maximum(i * nb_per - 1, 0)
        out_a = pl.pallas_call(
            _swa_kernel,
            grid=(B, S // tq),
            in_specs=[pl.BlockSpec(memory_space=pltpu.SMEM),
                      cur(WIDTH_A), cur(2 * LANES),
                      pl.BlockSpec((1, BLOCK, 2 * LANES), lambda b, i: (b, prev_blk(b, i), 0)),
                      pl.BlockSpec((1, N_KV_A * VT_ROWS, tq), lambda b, i: (b, 0, i)),
                      pl.BlockSpec((1, N_KV_A * VT_ROWS, BLOCK), lambda b, i: (b, 0, prev_blk(b, i))),
                      pl.BlockSpec((1, 1, tq), lambda b, i: (b, 0, i)),
                      pl.BlockSpec((1, 1, BLOCK), lambda b, i: (b, 0, prev_blk(b, i)))],
            out_specs=cur(WIDTH_A),
            out_shape=jax.ShapeDtypeStruct((B, S, WIDTH_A), bf16),
            compiler_params=pltpu.CompilerParams(dimension_semantics=("arbitrary", "arbitrary"),
                                                 vmem_limit_bytes=VMEM_LIMIT),
            name="swa",
        )(sinks[l].astype(f32), qa3, ka3, ka3, va_t3, va_t3, pos_r3, pos_r3)

        bq, nh = MLA_BQ, MLA_HEADS
        qb3 = qb.reshape(B, S, N_HEADS_B * LANES)
        kb3 = kb.reshape(B, S, N_HEADS_B * LANES)
        out_b = pl.pallas_call(
            _mla_kernel,
            grid=(B, N_HEADS_B // nh, S // bq),
            in_specs=[pl.BlockSpec((1, bq, nh * LANES), lambda b, j, i: (b, i, j)),
                      pl.BlockSpec((1, S, nh * LANES), lambda b, j, i: (b, 0, j)),
                      pl.BlockSpec((1, nkb, nh * VT_ROWS, bq), lambda b, j, i: (b, 0, j, 0))],
            out_specs=pl.BlockSpec((1, bq, nh * V_DIM_B), lambda b, j, i: (b, i, j)),
            out_shape=jax.ShapeDtypeStruct((B, S, WIDTH_B), bf16),
            scratch_shapes=[pltpu.VMEM((2, nh, bq, bq), f32),
                            pltpu.VMEM((2, nh, 1, bq), f32),
                            pltpu.VMEM((nh, VT_ROWS, bq), f32), pltpu.VMEM((nh, 1, bq), f32)],
            compiler_params=pltpu.CompilerParams(
                dimension_semantics=("arbitrary", "arbitrary", "arbitrary"),
                vmem_limit_bytes=VMEM_LIMIT),
            name="mla",
        )(qb3, kb3, vt)

        tm = MERGE_TM
        tok = lambda w: pl.BlockSpec((tm, w), lambda i: (i, 0))
        x2 = pl.pallas_call(
            _merge_kernel,
            grid=(T // tm,),
            in_specs=[tok(D), tok(WIDTH_A), tok(WIDTH_B), _const_spec((1, D)),
                      w_rows(2 * D, IN_GATES), _const_spec((WIDTH_A, D)), _const_spec((WIDTH_B, D)),
                      _const_spec((D, D)), _const_spec((1, D)), _const_spec((1, D)),
                      _const_spec((D, D_FF)), _const_spec((D_FF, D)), _const_spec((1, D))],
            out_specs=tok(D),
            out_shape=jax.ShapeDtypeStruct((T, D), f32),
            compiler_params=pltpu.CompilerParams(dimension_semantics=("arbitrary",),
                                                 vmem_limit_bytes=VMEM_LIMIT),
            name="merge_mlp",
        )(x2, out_a.reshape(T, WIDTH_A), out_b.reshape(T, WIDTH_B), row(pre_norm_mix[l]), wit,
          w_o_a[l].astype(bf16), w_o_b[l].astype(bf16), w_out[l].astype(bf16),
          row(post_norm_mix[l]), row(pre_norm_mlp[l]), w_up[l].astype(bf16),
          w_down[l].astype(bf16), row(post_norm_mlp[l]))
        x = x2.reshape(B, S, D)
    return x
```

```python
import functools

import jax
import jax.numpy as jnp
from jax import lax
from jax.experimental import pallas as pl
from jax.experimental.pallas import tpu as pltpu

D_MODEL = 1024
N_HEADS_A = 8
N_KV_A = 2
HEAD_DIM_A = 64
WINDOW = 128
BLOCK = 128
N_HEADS_B = 8
QK_NOPE = 64
QK_ROPE = 32
V_DIM_B = 64
Q_LORA = 256
KV_LORA = 128
ROPE_THETA = 10000.0
D_FF = 4 * D_MODEL
EPS = 1e-6

WIDTH_A = N_HEADS_A * HEAD_DIM_A
WIDTH_B = N_HEADS_B * V_DIM_B
Q_HEAD_B = QK_NOPE + QK_ROPE
KV_HEAD_B = QK_NOPE + V_DIM_B

LANES = 128
HALF = LANES // 2
VT_ROWS = LANES
VT_PAD = VT_ROWS - V_DIM_B
SCALE_A = HEAD_DIM_A ** -0.5
SCALE_B = Q_HEAD_B ** -0.5
LOG2_E = 1.4426950408889634
QSCALE_A = SCALE_A * LOG2_E
QSCALE_B = SCALE_B * LOG2_E
ALIBI_SLOPES = tuple(2.0 ** (-8.0 * (h + 1) / N_HEADS_A) for h in range(N_HEADS_A))

IN_GATES = 0
IN_QA = 2 * D_MODEL
IN_KA = IN_QA + WIDTH_A
IN_VA = IN_KA + N_KV_A * HEAD_DIM_A
IN_CQ = IN_VA + N_KV_A * HEAD_DIM_A
IN_CKV = IN_CQ + Q_LORA
IN_KR = IN_CKV + KV_LORA
D_IN = IN_KR + QK_ROPE
SM_KA = 0
SM_KR = SM_KA + 2 * LANES
SM_KRS = SM_KR + LANES
SM_ROWS = SM_KRS + LANES

PROJ_TM = 1024
PROJ_SUB = 512
SWA_TQ = 1024
SWA_LEAD = 4
MLA_BQ = 512
MLA_CHUNK = 256
MLA_HEADS = 4
MLA_LEAD = 1
MERGE_TM = 512
FF_CHUNK = 1024
VMEM_LIMIT = 60 * 1024 * 1024


def _rms(v):
    return v * lax.rsqrt(jnp.mean(v * v, axis=-1, keepdims=True) + EPS)


def _dot(a, b):
    return jnp.dot(a, b, preferred_element_type=jnp.float32)


def _dot_nt(a, b):
    return lax.dot_general(a, b, (((1,), (1,)), ((), ())), preferred_element_type=jnp.float32)


def _proj_kernel(x_ref, pos_ref, gpre_ref, wqa_ref, wcq_ref, wckv_ref, wsm_ref, wvat_ref, gq_ref,
                 wqm_ref, wqs_ref, gkv_ref, wkk_ref, wkvt_ref, freq_ref,
                 qa_ref, ka_ref, vat_ref, qb_ref, kb_ref, vt_ref):
    bf16 = jnp.bfloat16
    sub = PROJ_SUB

    def first_stage(n):
        rows = slice(n * sub, (n + 1) * sub)
        hb = (_rms(x_ref[rows, :]) * gpre_ref[...]).astype(bf16)
        qa_ref[rows, :] = (_dot_nt(hb, wqa_ref[...]) * QSCALE_A).astype(bf16)
        small = _dot_nt(hb, wsm_ref[...])
        ka_ref[rows, :] = small[:, SM_KA:SM_KR].astype(bf16)
        va_t = _dot_nt(wvat_ref[...], hb)
        varow = lax.broadcasted_iota(jnp.int32, va_t.shape, 0)
        vat_ref[0, :, rows] = jnp.where(varow % VT_ROWS == HEAD_DIM_A, 1.0, va_t).astype(bf16)
        cq = _dot_nt(hb, wcq_ref[...])
        ckv = _dot_nt(hb, wckv_ref[...])
        pos = pos_ref[0, :, rows].astype(jnp.float32)
        ang = freq_ref[...] * pos
        cos_t, sin_t = jnp.cos(ang), jnp.sin(ang)
        one = jnp.ones((QK_NOPE, sub), jnp.float32)
        zero = jnp.zeros((QK_NOPE, sub), jnp.float32)
        pad = LANES - Q_HEAD_B
        cos = jnp.concatenate([one, cos_t, cos_t, one[:pad]], axis=0).T
        sin = jnp.concatenate([zero, -sin_t, sin_t, zero[:pad]], axis=0).T
        k_rot = small[:, SM_KR:SM_KRS] * cos + small[:, SM_KRS:SM_ROWS] * sin
        return cq, ckv, cos, sin, k_rot

    def second_stage(n, cq, ckv, cos, sin, k_rot):
        rows = slice(n * sub, (n + 1) * sub)
        cqn = (_rms(cq) * gq_ref[...]).astype(bf16)
        q_main = _dot(cqn, wqm_ref[...])
        q_swap = _dot(cqn, wqs_ref[...])
        ckvn = (_rms(ckv) * gkv_ref[...]).astype(bf16)
        k_nope = _dot(ckvn, wkk_ref[...])
        v_t = _dot_nt(wkvt_ref[...], ckvn)
        vrow = lax.broadcasted_iota(jnp.int32, v_t.shape, 0)
        vt_ref[0, n] = jnp.where(vrow % VT_ROWS == V_DIM_B, 1.0, v_t).astype(bf16)
        for h in range(N_HEADS_B):
            sl = slice(h * LANES, (h + 1) * LANES)
            qb_ref[rows, sl] = ((q_main[:, sl] * cos + q_swap[:, sl] * sin) * QSCALE_B).astype(bf16)
            kb_ref[rows, sl] = (k_nope[:, sl] + k_rot).astype(bf16)

    n_sub = x_ref.shape[0] // sub
    staged = [first_stage(n) for n in range(n_sub)]
    for n in range(n_sub):
        second_stage(n, *staged[n])


def _swa_kernel(sink_ref, q_ref, kc_ref, kp_ref, vtc_ref, vtp_ref, posc_ref, posp_ref, o_ref):
    bf16 = jnp.bfloat16
    i = pl.program_id(1)
    krow = lax.broadcasted_iota(jnp.int32, (2 * BLOCK, BLOCK), 0)
    qcol = lax.broadcasted_iota(jnp.int32, (2 * BLOCK, BLOCK), 1)
    ahead = krow - qcol
    band = (ahead > 0) & (ahead <= WINDOW)
    lane = lax.broadcasted_iota(jnp.int32, (BLOCK, LANES), 1)
    low_half = lane < HALF
    pos_inf = jnp.float32(jnp.inf)

    n_pairs = N_HEADS_A // 2
    chains = [(blk, pair) for blk in range(SWA_TQ // BLOCK) for pair in range(n_pairs)]
    bands = {}

    def band_of(blk):
        if blk not in bands:
            r0 = blk * BLOCK
            if blk == 0:
                kband = jnp.concatenate([kp_ref[0], kc_ref[0, 0:BLOCK, :]], axis=0)
                vtband = jnp.concatenate([vtp_ref[0], vtc_ref[0, :, 0:BLOCK]], axis=1)
                kpos = jnp.concatenate([posp_ref[0], posc_ref[0, :, 0:BLOCK]], axis=1)
                mask = band & ((krow >= BLOCK) | (i > 0))
            else:
                kband = kc_ref[0, r0 - BLOCK:r0 + BLOCK, :]
                vtband = vtc_ref[0, :, r0 - BLOCK:r0 + BLOCK]
                kpos = posc_ref[0, :, r0 - BLOCK:r0 + BLOCK]
                mask = band
            qpos = posc_ref[0, :, r0:r0 + BLOCK]
            kpos_col = jnp.broadcast_to(kpos, (BLOCK, 2 * BLOCK)).T
            dist = jnp.abs(kpos_col - qpos).astype(jnp.float32) * LOG2_E
            dist = jnp.where(mask, dist, pos_inf)
            bands[blk] = (kband, vtband, dist)
        return bands[blk]

    def scores(blk, pair):
        kv = (2 * pair) // (N_HEADS_A // N_KV_A)
        r0 = blk * BLOCK
        qp = q_ref[0, r0:r0 + BLOCK, pair * LANES:(pair + 1) * LANES]
        zero = jnp.zeros_like(qp)
        q2 = jnp.concatenate([jnp.where(low_half, qp, zero), jnp.where(low_half, zero, qp)],
                             axis=0)
        kx = band_of(blk)[0][:, kv * LANES:(kv + 1) * LANES]
        return _dot_nt(kx, q2)

    def finish(blk, pair, s2):
        kv = (2 * pair) // (N_HEADS_A // N_KV_A)
        r0 = blk * BLOCK
        _, vtband, dist = band_of(blk)
        vt = vtband[kv * VT_ROWS:(kv + 1) * VT_ROWS, :]
        ps, ms = [], []
        for e in range(2):
            h = 2 * pair + e
            s = s2[:, e * BLOCK:(e + 1) * BLOCK] - ALIBI_SLOPES[h] * dist
            m = jnp.maximum(jnp.max(s, axis=0, keepdims=True), sink_ref[h] * LOG2_E)
            ps.append(jnp.exp2(s - m).astype(bf16))
            ms.append(m)
        o2 = _dot(vt, jnp.concatenate(ps, axis=1))
        outs = []
        for e in range(2):
            h = 2 * pair + e
            o_t = o2[:, e * BLOCK:(e + 1) * BLOCK]
            denom = (o_t[HEAD_DIM_A:HEAD_DIM_A + 1]
                     + jnp.exp2(sink_ref[h] * LOG2_E - ms[e]))
            outs.append(o_t[0:HEAD_DIM_A] / denom)
        o_ref[0, r0:r0 + BLOCK, pair * LANES:(pair + 1) * LANES] = (
            jnp.concatenate(outs, axis=0).T.astype(bf16))

    pending = [scores(*c) for c in chains[:SWA_LEAD]]
    for n, c in enumerate(chains):
        s2 = pending.pop(0)
        if n + SWA_LEAD < len(chains):
            pending.append(scores(*chains[n + SWA_LEAD]))
        finish(*c, s2)


def _mla_kernel(q_ref, k_ref, vt_ref, o_ref, s_ref, bmax_ref, acc_ref, m_ref):
    bf16 = jnp.bfloat16
    bq = MLA_BQ
    ch = MLA_CHUNK
    qi = pl.program_id(2)
    krow = lax.broadcasted_iota(jnp.int32, (bq, ch), 0)
    qcol = lax.broadcasted_iota(jnp.int32, (bq, ch), 1)
    neg_inf = jnp.float32(-jnp.inf)
    m_ref[...] = jnp.full(m_ref.shape, neg_inf, jnp.float32)
    acc_ref[...] = jnp.zeros(acc_ref.shape, jnp.float32)

    def scores_head(buf, kb, e):
        start = pl.multiple_of(kb * bq, bq)
        sl = slice(e * LANES, (e + 1) * LANES)
        s = _dot_nt(k_ref[0, pl.ds(start, bq), sl], q_ref[0, :, sl])
        s_ref[buf, e] = s
        bmax_ref[buf, e] = jnp.max(s, axis=0, keepdims=True)

    def scores_into(buf, kb):
        for e in range(MLA_HEADS):
            scores_head(buf, kb, e)

    def softmax_head(buf, e, masked):
        out = []
        for c in range(bq // ch):
            cs = slice(c * ch, (c + 1) * ch)
            s = s_ref[buf, e, :, cs]
            if masked:
                s = jnp.where(krow <= qcol + c * ch, s, neg_inf)
                blockmax = jnp.max(s, axis=0, keepdims=True)
            else:
                blockmax = bmax_ref[buf, e, :, cs]
            m_prev = m_ref[e, :, cs]
            m_new = jnp.maximum(m_prev, blockmax)
            m_ref[e, :, cs] = m_new
            out.append((jnp.exp2(s - m_new).astype(bf16), jnp.exp2(m_prev - m_new)))
        return out

    def pv_head(kb, e, weights):
        vt = vt_ref[0, kb, e * VT_ROWS:(e + 1) * VT_ROWS, :]
        for c, (p, alpha) in enumerate(weights):
            cs = slice(c * ch, (c + 1) * ch)
            acc_ref[e, :, cs] = alpha * acc_ref[e, :, cs] + _dot(vt, p)

    def softmax_pv(buf, kb, masked):
        for e in range(MLA_HEADS):
            pv_head(kb, e, softmax_head(buf, e, masked))

    def pipelined(first_kb, n_blocks, opening=False, diagonal=False):
        blocks = [(j % 2, first_kb + j) for j in range(n_blocks)]
        units = [(buf, kb, e, False) for buf, kb in blocks for e in range(MLA_HEADS)]
        prods = [(1 - buf, kb + 1, e) for buf, kb in blocks for e in range(MLA_HEADS)]
        lead = MLA_LEAD
        issued = 0
        if opening:
            prods = [(0, first_kb, e) for e in range(MLA_HEADS)] + prods
            lead += MLA_HEADS
        if diagonal:
            units += [(n_blocks % 2, first_kb + n_blocks, e, True) for e in range(MLA_HEADS)]
        for _ in range(min(2 if opening else lead, len(prods))):
            scores_head(*prods[issued])
            issued += 1
        for n, (buf, kb, e, masked) in enumerate(units):
            pv_head(kb, e, softmax_head(buf, e, masked))
            for _ in range(2):
                if issued < min(n + 1 + lead, len(prods)):
                    scores_head(*prods[issued])
                    issued += 1

    def quad_body(t, carry):
        pipelined(4 * t, 4)
        return carry

    n_quads = qi // 4
    rest = qi % 4

    @pl.when(n_quads == 0)
    def _():
        scores_into(0, 0)

    @pl.when(n_quads > 0)
    def _():
        pipelined(0, 4, opening=True)

    lax.fori_loop(1, n_quads, quad_body, 0)

    for r in range(4):
        @pl.when(rest == r)
        def _(r=r):
            pipelined(n_quads * 4, r, diagonal=True)

    outs = []
    for e in range(MLA_HEADS):
        acc = acc_ref[e]
        outs.append(acc[0:V_DIM_B] / acc[V_DIM_B:V_DIM_B + 1])
    o_ref[0] = jnp.concatenate(outs, axis=0).T.astype(bf16)


def _merge_kernel(x_ref, oa_ref, ob_ref, gpre_ref, wg_ref, woa_ref, wob_ref, wout_ref, gpost_ref,
                  gpre2_ref, wup_ref, wdn_ref, gpost2_ref, o_ref):
    bf16 = jnp.bfloat16
    x = x_ref[...]
    hm = x.shape[0] // 2
    halves = (slice(0, hm), slice(hm, 2 * hm))
    hb = [(_rms(x[r]) * gpre_ref[...]).astype(bf16) for r in halves]
    gate_a = jnp.concatenate([_dot_nt(h, wg_ref[0:D_MODEL, :]) for h in hb], axis=0)
    hb = jnp.concatenate(hb, axis=0)
    gate_a = jax.nn.sigmoid(gate_a)
    gate_b = jax.nn.sigmoid(_dot_nt(hb, wg_ref[D_MODEL:2 * D_MODEL, :]))
    merged = gate_a * _dot(oa_ref[...], woa_ref[...]) + gate_b * _dot(ob_ref[...], wob_ref[...])
    mb = merged.astype(bf16)

    n_chunks = D_FF // FF_CHUNK
    chunk = lambda c: slice(c * FF_CHUNK, (c + 1) * FF_CHUNK)

    def sq_relu(v):
        v = jnp.maximum(v, 0.0)
        return (v * v).astype(bf16)

    x1, h2, up0 = [], [], []
    for r in halves:
        y = _dot(mb[r], wout_ref[...])
        x1.append(x[r] + _rms(y) * gpost_ref[...])
        h2.append((_rms(x1[-1]) * gpre2_ref[...]).astype(bf16))
    for i in range(2):
        up0.append(sq_relu(_dot(h2[i], wup_ref[:, chunk(0)])))
    h2 = jnp.concatenate(h2, axis=0)
    y2 = _dot(jnp.concatenate(up0, axis=0), wdn_ref[chunk(0), :])
    for c in range(1, n_chunks - 1):
        y2 = y2 + _dot(sq_relu(_dot(h2, wup_ref[:, chunk(c)])), wdn_ref[chunk(c), :])
    last = sq_relu(_dot(h2, wup_ref[:, chunk(n_chunks - 1)]))
    for i, r in enumerate(halves):
        y2_half = y2[r] + _dot(last[r], wdn_ref[chunk(n_chunks - 1), :])
        o_ref[r, :] = x1[i] + _rms(y2_half) * gpost2_ref[...]


def _const_spec(shape):
    return pl.BlockSpec(shape, lambda *_: (0,) * len(shape), pipeline_mode=pl.Buffered(1))


def kernel(x, positions, pre_norm_mix, w_in, q_a_norm, w_q_b, kv_a_norm, w_kv_b, sinks, w_o_a,
           w_o_b, w_out, post_norm_mix, pre_norm_mlp, w_up, w_down, post_norm_mlp):
    f32, bf16 = jnp.float32, jnp.bfloat16
    B, S, D = x.shape
    T = B * S
    depth = w_in.shape[0]
    for l in range(depth):
        assert w_in.shape[2] == D_IN
        wit = jnp.swapaxes(w_in[l], 0, 1).astype(bf16)
        ka_t = wit[IN_KA:IN_VA]
        va_t = wit[IN_VA:IN_CQ]
        kr_t = wit[IN_KR:D_IN]
        hr = QK_ROPE // 2
        hd = HEAD_DIM_A
        z = lambda n: jnp.zeros((n, D_MODEL), bf16)
        w_small_t = jnp.concatenate([
            ka_t[:hd], ka_t[:hd], ka_t[hd:], ka_t[hd:],
            z(QK_NOPE), kr_t, z(LANES - Q_HEAD_B),
            z(QK_NOPE), kr_t[hr:], kr_t[:hr], z(LANES - Q_HEAD_B)], axis=0)
        w_va_t = jnp.concatenate([va_t[:hd], z(VT_PAD), va_t[hd:], z(VT_PAD)], axis=0)

        wq = w_q_b[l].reshape(Q_LORA, N_HEADS_B, Q_HEAD_B)
        q_nope, q_rope = wq[..., :QK_NOPE], wq[..., QK_NOPE:]
        zq = lambda n: jnp.zeros((Q_LORA, N_HEADS_B, n), f32)
        wq_main = jnp.concatenate([q_nope, q_rope, zq(LANES - Q_HEAD_B)], -1)
        wq_swap = jnp.concatenate([zq(QK_NOPE), q_rope[..., hr:], q_rope[..., :hr],
                                   zq(LANES - Q_HEAD_B)], -1)
        wq_main = wq_main.reshape(Q_LORA, N_HEADS_B * LANES).astype(bf16)
        wq_swap = wq_swap.reshape(Q_LORA, N_HEADS_B * LANES).astype(bf16)

        wkv = w_kv_b[l].reshape(KV_LORA, N_HEADS_B, KV_HEAD_B)
        kv_k, kv_v = wkv[..., :QK_NOPE], wkv[..., QK_NOPE:]
        zk = jnp.zeros((KV_LORA, N_HEADS_B, HALF), f32)
        wkv_k = jnp.concatenate([kv_k, zk], -1).reshape(KV_LORA, N_HEADS_B * LANES).astype(bf16)
        wkv_vt = jnp.concatenate([kv_v, zk[..., :VT_PAD]], -1)
        wkv_vt = wkv_vt.reshape(KV_LORA, N_HEADS_B * VT_ROWS).T.astype(bf16)

        freq_col = (ROPE_THETA ** (-jnp.arange(0, QK_ROPE, 2, dtype=f32) / QK_ROPE))[:, None]

        row = lambda g: g.reshape(1, -1).astype(f32)
        x2 = x.reshape(T, D)

        tm = PROJ_TM
        assert PROJ_SUB == MLA_BQ and tm % PROJ_SUB == 0 and S % tm == 0
        nkb = S // MLA_BQ
        n_sub = tm // PROJ_SUB
        npb = S // tm
        tok = lambda w: pl.BlockSpec((tm, w), lambda i: (i, 0))
        w_rows = lambda n, start: pl.BlockSpec((n, D), lambda i: (start // n, 0),
                                               pipeline_mode=pl.Buffered(1))
        assert IN_QA % WIDTH_A == 0 and IN_CQ % Q_LORA == 0 and IN_CKV % KV_LORA == 0
        qa, ka_x, va_t3, qb, kb, vt = pl.pallas_call(
            _proj_kernel,
            grid=(T // tm,),
            in_specs=[tok(D), pl.BlockSpec((1, 1, tm), lambda i: (i, 0, 0)), _const_spec((1, D)),
                      w_rows(WIDTH_A, IN_QA), w_rows(Q_LORA, IN_CQ), w_rows(KV_LORA, IN_CKV),
                      _const_spec((SM_ROWS, D)), _const_spec((N_KV_A * VT_ROWS, D)),
                      _const_spec((1, Q_LORA)), _const_spec((Q_LORA, N_HEADS_B * LANES)),
                      _const_spec((Q_LORA, N_HEADS_B * LANES)), _const_spec((1, KV_LORA)),
                      _const_spec((KV_LORA, N_HEADS_B * LANES)),
                      _const_spec((N_HEADS_B * VT_ROWS, KV_LORA)),
                      _const_spec((QK_ROPE // 2, 1))],
            out_specs=[tok(WIDTH_A), tok(2 * LANES),
                       pl.BlockSpec((1, N_KV_A * VT_ROWS, tm), lambda i: (i // npb, 0, i % npb)),
                       tok(N_HEADS_B * LANES), tok(N_HEADS_B * LANES),
                       pl.BlockSpec((1, n_sub, N_HEADS_B * VT_ROWS, PROJ_SUB),
                                    lambda i: (i // npb, i % npb, 0, 0))],
            out_shape=[jax.ShapeDtypeStruct((T, WIDTH_A), bf16),
                       jax.ShapeDtypeStruct((T, 2 * LANES), bf16),
                       jax.ShapeDtypeStruct((B, N_KV_A * VT_ROWS, S), bf16),
                       jax.ShapeDtypeStruct((T, N_HEADS_B * LANES), bf16),
                       jax.ShapeDtypeStruct((T, N_HEADS_B * LANES), bf16),
                       jax.ShapeDtypeStruct((B, nkb, N_HEADS_B * VT_ROWS, PROJ_SUB), bf16)],
            compiler_params=pltpu.CompilerParams(dimension_semantics=("arbitrary",),
                                                 vmem_limit_bytes=VMEM_LIMIT),
            name="proj",
        )(x2, positions.reshape(T // tm, 1, tm), row(pre_norm_mix[l]), wit, wit, wit, w_small_t,
          w_va_t, row(q_a_norm[l]), wq_main, wq_swap, row(kv_a_norm[l]), wkv_k, wkv_vt,
          freq_col)

        tq = SWA_TQ
        nb_per = tq // BLOCK
        qa3 = qa.reshape(B, S, WIDTH_A)
        ka3 = ka_x.reshape(B, S, 2 * LANES)
        pos_r3 = positions.reshape(B, 1, S)
        cur = lambda w: pl.BlockSpec((1, tq, w), lambda b, i: (b, i, 0))
        prev_blk = lambda b, i: jnp.maximum(i * nb_per - 1, 0)
        out_a = pl.pallas_call(
            _swa_kernel,
            grid=(B, S // tq),
            in_specs=[pl.BlockSpec(memory_space=pltpu.SMEM),
                      cur(WIDTH_A), cur(2 * LANES),
                      pl.BlockSpec((1, BLOCK, 2 * LANES), lambda b, i: (b, prev_blk(b, i), 0)),
                      pl.BlockSpec((1, N_KV_A * VT_ROWS, tq), lambda b, i: (b, 0, i)),
                      pl.BlockSpec((1, N_KV_A * VT_ROWS, BLOCK), lambda b, i: (b, 0, prev_blk(b, i))),
                      pl.BlockSpec((1, 1, tq), lambda b, i: (b, 0, i)),
                      pl.BlockSpec((1, 1, BLOCK), lambda b, i: (b, 0, prev_blk(b, i)))],
            out_specs=cur(WIDTH_A),
            out_shape=jax.ShapeDtypeStruct((B, S, WIDTH_A), bf16),
            compiler_params=pltpu.CompilerParams(dimension_semantics=("arbitrary", "arbitrary"),
                                                 vmem_limit_bytes=VMEM_LIMIT),
            name="swa",
        )(sinks[l].astype(f32), qa3, ka3, ka3, va_t3, va_t3, pos_r3, pos_r3)

        bq, nh = MLA_BQ, MLA_HEADS
        qb3 = qb.reshape(B, S, N_HEADS_B * LANES)
        kb3 = kb.reshape(B, S, N_HEADS_B * LANES)
        out_b = pl.pallas_call(
            _mla_kernel,
            grid=(B, N_HEADS_B // nh, S // bq),
            in_specs=[pl.BlockSpec((1, bq, nh * LANES), lambda b, j, i: (b, i, j)),
                      pl.BlockSpec((1, S, nh * LANES), lambda b, j, i: (b, 0, j)),
                      pl.BlockSpec((1, nkb, nh * VT_ROWS, bq), lambda b, j, i: (b, 0, j, 0))],
            out_specs=pl.BlockSpec((1, bq, nh * V_DIM_B), lambda b, j, i: (b, i, j)),
            out_shape=jax.ShapeDtypeStruct((B, S, WIDTH_B), bf16),
            scratch_shapes=[pltpu.VMEM((2, nh, bq, bq), f32),
                            pltpu.VMEM((2, nh, 1, bq), f32),
                            pltpu.VMEM((nh, VT_ROWS, bq), f32), pltpu.VMEM((nh, 1, bq), f32)],
            compiler_params=pltpu.CompilerParams(
                dimension_semantics=("arbitrary", "arbitrary", "arbitrary"),
                vmem_limit_bytes=VMEM_LIMIT),
            name="mla",
        )(qb3, kb3, vt)

        tm = MERGE_TM
        tok = lambda w: pl.BlockSpec((tm, w), lambda i: (i, 0))
        x2 = pl.pallas_call(
            _merge_kernel,
            grid=(T // tm,),
            in_specs=[tok(D), tok(WIDTH_A), tok(WIDTH_B), _const_spec((1, D)),
                      w_rows(2 * D, IN_GATES), _const_spec((WIDTH_A, D)), _const_spec((WIDTH_B, D)),
                      _const_spec((D, D)), _const_spec((1, D)), _const_spec((1, D)),
                      _const_spec((D, D_FF)), _const_spec((D_FF, D)), _const_spec((1, D))],
            out_specs=tok(D),
            out_shape=jax.ShapeDtypeStruct((T, D), f32),
            compiler_params=pltpu.CompilerParams(dimension_semantics=("arbitrary",),
                                                 vmem_limit_bytes=VMEM_LIMIT),
            name="merge_mlp",
        )(x2, out_a.reshape(T, WIDTH_A), out_b.reshape(T, WIDTH_B), row(pre_norm_mix[l]), wit,
          w_o_a[l].astype(bf16), w_o_b[l].astype(bf16), w_out[l].astype(bf16),
          row(post_norm_mix[l]), row(pre_norm_mlp[l]), w_up[l].astype(bf16),
          w_down[l].astype(bf16), row(post_norm_mlp[l]))
        x = x2.reshape(B, S, D)
    return x
```

```python
import functools

import jax
import jax.numpy as jnp
from jax import lax
from jax.experimental import pallas as pl
from jax.experimental.pallas import tpu as pltpu

D_MODEL = 1024
N_HEADS_A = 8
N_KV_A = 2
HEAD_DIM_A = 64
WINDOW = 128
BLOCK = 128
N_HEADS_B = 8
QK_NOPE = 64
QK_ROPE = 32
V_DIM_B = 64
Q_LORA = 256
KV_LORA = 128
ROPE_THETA = 10000.0
D_FF = 4 * D_MODEL
EPS = 1e-6

WIDTH_A = N_HEADS_A * HEAD_DIM_A
WIDTH_B = N_HEADS_B * V_DIM_B
Q_HEAD_B = QK_NOPE + QK_ROPE
KV_HEAD_B = QK_NOPE + V_DIM_B

LANES = 128
HALF = LANES // 2
VT_ROWS = LANES
VT_PAD = VT_ROWS - V_DIM_B
SCALE_A = HEAD_DIM_A ** -0.5
SCALE_B = Q_HEAD_B ** -0.5
LOG2_E = 1.4426950408889634
QSCALE_A = SCALE_A * LOG2_E
QSCALE_B = SCALE_B * LOG2_E
ALIBI_SLOPES = tuple(2.0 ** (-8.0 * (h + 1) / N_HEADS_A) for h in range(N_HEADS_A))

IN_GATES = 0
IN_QA = 2 * D_MODEL
IN_KA = IN_QA + WIDTH_A
IN_VA = IN_KA + N_KV_A * HEAD_DIM_A
IN_CQ = IN_VA + N_KV_A * HEAD_DIM_A
IN_CKV = IN_CQ + Q_LORA
IN_KR = IN_CKV + KV_LORA
D_IN = IN_KR + QK_ROPE
SM_KA = 0
SM_KR = SM_KA + 2 * LANES
SM_KRS = SM_KR + LANES
SM_ROWS = SM_KRS + LANES

PROJ_TM = 1024
PROJ_SUB = 512
SWA_TQ = 1024
SWA_LEAD = 4
MLA_BQ = 512
MLA_CHUNK = 512
MLA_HEADS = 4
MLA_LEAD = 1
MERGE_TM = 512
FF_CHUNK = 1024
VMEM_LIMIT = 60 * 1024 * 1024


def _rms(v):
    return v * lax.rsqrt(jnp.mean(v * v, axis=-1, keepdims=True) + EPS)


def _dot(a, b):
    return jnp.dot(a, b, preferred_element_type=jnp.float32)


def _dot_nt(a, b):
    return lax.dot_general(a, b, (((1,), (1,)), ((), ())), preferred_element_type=jnp.float32)


def _proj_kernel(x_ref, pos_ref, gpre_ref, wqa_ref, wcq_ref, wckv_ref, wsm_ref, wvat_ref, gq_ref,
                 wqm_ref, wqs_ref, gkv_ref, wkk_ref, wkvt_ref, freq_ref,
                 qa_ref, ka_ref, vat_ref, qb_ref, kb_ref, vt_ref):
    bf16 = jnp.bfloat16
    sub = PROJ_SUB

    def first_stage(n):
        rows = slice(n * sub, (n + 1) * sub)
        hb = (_rms(x_ref[rows, :]) * gpre_ref[...]).astype(bf16)
        qa_ref[rows, :] = (_dot_nt(hb, wqa_ref[...]) * QSCALE_A).astype(bf16)
        small = _dot_nt(hb, wsm_ref[...])
        ka_ref[rows, :] = small[:, SM_KA:SM_KR].astype(bf16)
        va_t = _dot_nt(wvat_ref[...], hb)
        varow = lax.broadcasted_iota(jnp.int32, va_t.shape, 0)
        vat_ref[0, :, rows] = jnp.where(varow % VT_ROWS == HEAD_DIM_A, 1.0, va_t).astype(bf16)
        cq = _dot_nt(hb, wcq_ref[...])
        ckv = _dot_nt(hb, wckv_ref[...])
        pos = pos_ref[0, :, rows].astype(jnp.float32)
        ang = freq_ref[...] * pos
        cos_t, sin_t = jnp.cos(ang), jnp.sin(ang)
        one = jnp.ones((QK_NOPE, sub), jnp.float32)
        zero = jnp.zeros((QK_NOPE, sub), jnp.float32)
        pad = LANES - Q_HEAD_B
        cos = jnp.concatenate([one, cos_t, cos_t, one[:pad]], axis=0).T
        sin = jnp.concatenate([zero, -sin_t, sin_t, zero[:pad]], axis=0).T
        k_rot = small[:, SM_KR:SM_KRS] * cos + small[:, SM_KRS:SM_ROWS] * sin
        return cq, ckv, cos, sin, k_rot

    def second_stage(n, cq, ckv, cos, sin, k_rot):
        rows = slice(n * sub, (n + 1) * sub)
        cqn = (_rms(cq) * gq_ref[...]).astype(bf16)
        q_main = _dot(cqn, wqm_ref[...])
        q_swap = _dot(cqn, wqs_ref[...])
        ckvn = (_rms(ckv) * gkv_ref[...]).astype(bf16)
        k_nope = _dot(ckvn, wkk_ref[...])
        v_t = _dot_nt(wkvt_ref[...], ckvn)
        vrow = lax.broadcasted_iota(jnp.int32, v_t.shape, 0)
        vt_ref[0, n] = jnp.where(vrow % VT_ROWS == V_DIM_B, 1.0, v_t).astype(bf16)
        for h in range(N_HEADS_B):
            sl = slice(h * LANES, (h + 1) * LANES)
            qb_ref[rows, sl] = ((q_main[:, sl] * cos + q_swap[:, sl] * sin) * QSCALE_B).astype(bf16)
            kb_ref[rows, sl] = (k_nope[:, sl] + k_rot).astype(bf16)

    n_sub = x_ref.shape[0] // sub
    staged = [first_stage(n) for n in range(n_sub)]
    for n in range(n_sub):
        second_stage(n, *staged[n])


def _swa_kernel(sink_ref, q_ref, kc_ref, kp_ref, vtc_ref, vtp_ref, posc_ref, posp_ref, o_ref):
    bf16 = jnp.bfloat16
    i = pl.program_id(1)
    krow = lax.broadcasted_iota(jnp.int32, (2 * BLOCK, BLOCK), 0)
    qcol = lax.broadcasted_iota(jnp.int32, (2 * BLOCK, BLOCK), 1)
    ahead = krow - qcol
    band = (ahead > 0) & (ahead <= WINDOW)
    lane = lax.broadcasted_iota(jnp.int32, (BLOCK, LANES), 1)
    low_half = lane < HALF
    pos_inf = jnp.float32(jnp.inf)

    n_pairs = N_HEADS_A // 2
    chains = [(blk, pair) for blk in range(SWA_TQ // BLOCK) for pair in range(n_pairs)]
    bands = {}

    def band_of(blk):
        if blk not in bands:
            r0 = blk * BLOCK
            if blk == 0:
                kband = jnp.concatenate([kp_ref[0], kc_ref[0, 0:BLOCK, :]], axis=0)
                vtband = jnp.concatenate([vtp_ref[0], vtc_ref[0, :, 0:BLOCK]], axis=1)
                kpos = jnp.concatenate([posp_ref[0], posc_ref[0, :, 0:BLOCK]], axis=1)
                mask = band & ((krow >= BLOCK) | (i > 0))
            else:
                kband = kc_ref[0, r0 - BLOCK:r0 + BLOCK, :]
                vtband = vtc_ref[0, :, r0 - BLOCK:r0 + BLOCK]
                kpos = posc_ref[0, :, r0 - BLOCK:r0 + BLOCK]
                mask = band
            qpos = posc_ref[0, :, r0:r0 + BLOCK]
            kpos_col = jnp.broadcast_to(kpos, (BLOCK, 2 * BLOCK)).T
            dist = jnp.abs(kpos_col - qpos).astype(jnp.float32) * LOG2_E
            dist = jnp.where(mask, dist, pos_inf)
            bands[blk] = (kband, vtband, dist)
        return bands[blk]

    def scores(blk, pair):
        kv = (2 * pair) // (N_HEADS_A // N_KV_A)
        r0 = blk * BLOCK
        qp = q_ref[0, r0:r0 + BLOCK, pair * LANES:(pair + 1) * LANES]
        zero = jnp.zeros_like(qp)
        q2 = jnp.concatenate([jnp.where(low_half, qp, zero), jnp.where(low_half, zero, qp)],
                             axis=0)
        kx = band_of(blk)[0][:, kv * LANES:(kv + 1) * LANES]
        return _dot_nt(kx, q2)

    def finish(blk, pair, s2):
        kv = (2 * pair) // (N_HEADS_A // N_KV_A)
        r0 = blk * BLOCK
        _, vtband, dist = band_of(blk)
        vt = vtband[kv * VT_ROWS:(kv + 1) * VT_ROWS, :]
        ps, ms = [], []
        for e in range(2):
            h = 2 * pair + e
            s = s2[:, e * BLOCK:(e + 1) * BLOCK] - ALIBI_SLOPES[h] * dist
            m = jnp.maximum(jnp.max(s, axis=0, keepdims=True), sink_ref[h] * LOG2_E)
            ps.append(jnp.exp2(s - m).astype(bf16))
            ms.append(m)
        o2 = _dot(vt, jnp.concatenate(ps, axis=1))
        outs = []
        for e in range(2):
            h = 2 * pair + e
            o_t = o2[:, e * BLOCK:(e + 1) * BLOCK]
            denom = (o_t[HEAD_DIM_A:HEAD_DIM_A + 1]
                     + jnp.exp2(sink_ref[h] * LOG2_E - ms[e]))
            outs.append(o_t[0:HEAD_DIM_A] / denom)
        o_ref[0, r0:r0 + BLOCK, pair * LANES:(pair + 1) * LANES] = (
            jnp.concatenate(outs, axis=0).T.astype(bf16))

    pending = [scores(*c) for c in chains[:SWA_LEAD]]
    for n, c in enumerate(chains):
        s2 = pending.pop(0)
        if n + SWA_LEAD < len(chains):
            pending.append(scores(*chains[n + SWA_LEAD]))
        finish(*c, s2)


def _mla_kernel(q_ref, k_ref, vt_ref, o_ref, s_ref, bmax_ref, acc_ref, m_ref):
    bf16 = jnp.bfloat16
    bq = MLA_BQ
    ch = MLA_CHUNK
    qi = pl.program_id(2)
    krow = lax.broadcasted_iota(jnp.int32, (bq, ch), 0)
    qcol = lax.broadcasted_iota(jnp.int32, (bq, ch), 1)
    neg_inf = jnp.float32(-jnp.inf)
    m_ref[...] = jnp.full(m_ref.shape, neg_inf, jnp.float32)
    acc_ref[...] = jnp.zeros(acc_ref.shape, jnp.float32)

    def scores_head(buf, kb, e):
        start = pl.multiple_of(kb * bq, bq)
        sl = slice(e * LANES, (e + 1) * LANES)
        s = _dot_nt(k_ref[0, pl.ds(start, bq), sl], q_ref[0, :, sl])
        s_ref[buf, e] = s
        bmax_ref[buf, e] = jnp.max(s, axis=0, keepdims=True)

    def scores_into(buf, kb):
        for e in range(MLA_HEADS):
            scores_head(buf, kb, e)

    def softmax_head(buf, e, masked):
        out = []
        for c in range(bq // ch):
            cs = slice(c * ch, (c + 1) * ch)
            s = s_ref[buf, e, :, cs]
            if masked:
                s = jnp.where(krow <= qcol + c * ch, s, neg_inf)
                blockmax = jnp.max(s, axis=0, keepdims=True)
            else:
                blockmax = bmax_ref[buf, e, :, cs]
            m_prev = m_ref[e, :, cs]
            m_new = jnp.maximum(m_prev, blockmax)
            m_ref[e, :, cs] = m_new
            out.append((jnp.exp2(s - m_new).astype(bf16), jnp.exp2(m_prev - m_new)))
        return out

    def pv_head(kb, e, weights):
        vt = vt_ref[0, kb, e * VT_ROWS:(e + 1) * VT_ROWS, :]
        for c, (p, alpha) in enumerate(weights):
            cs = slice(c * ch, (c + 1) * ch)
            acc_ref[e, :, cs] = alpha * acc_ref[e, :, cs] + _dot(vt, p)

    def softmax_pv(buf, kb, masked):
        for e in range(MLA_HEADS):
            pv_head(kb, e, softmax_head(buf, e, masked))

    def pipelined(first_kb, n_blocks):
        blocks = [(j % 2, first_kb + j) for j in range(n_blocks)]
        units = [(buf, kb, e) for buf, kb in blocks for e in range(MLA_HEADS)]
        prods = [(1 - buf, kb + 1, e) for buf, kb in blocks for e in range(MLA_HEADS)]
        for n in range(min(MLA_LEAD, len(prods))):
            scores_head(*prods[n])
        for n, (buf, kb, e) in enumerate(units):
            pv_head(kb, e, softmax_head(buf, e, False))
            if n + MLA_LEAD < len(prods):
                scores_head(*prods[n + MLA_LEAD])

    def quad_body(t, carry):
        pipelined(4 * t, 4)
        return carry

    scores_into(0, 0)
    lax.fori_loop(0, qi // 4, quad_body, 0)

    @pl.when(qi % 4 >= 2)
    def _():
        pipelined((qi // 4) * 4, 2)

    @pl.when(qi % 2 == 1)
    def _():
        pipelined(qi - 1, 1)
        softmax_pv(1, qi, True)

    @pl.when(qi % 2 == 0)
    def _():
        softmax_pv(0, qi, True)

    outs = []
    for e in range(MLA_HEADS):
        acc = acc_ref[e]
        outs.append(acc[0:V_DIM_B] / acc[V_DIM_B:V_DIM_B + 1])
    o_ref[0] = jnp.concatenate(outs, axis=0).T.astype(bf16)


def _merge_kernel(x_ref, oa_ref, ob_ref, gpre_ref, wg_ref, woa_ref, wob_ref, wout_ref, gpost_ref,
                  gpre2_ref, wup_ref, wdn_ref, gpost2_ref, o_ref):
    bf16 = jnp.bfloat16
    x = x_ref[...]
    hm = x.shape[0] // 2
    halves = (slice(0, hm), slice(hm, 2 * hm))
    hb = [(_rms(x[r]) * gpre_ref[...]).astype(bf16) for r in halves]
    gate_a = jnp.concatenate([_dot_nt(h, wg_ref[0:D_MODEL, :]) for h in hb], axis=0)
    hb = jnp.concatenate(hb, axis=0)
    gate_a = jax.nn.sigmoid(gate_a)
    gate_b = jax.nn.sigmoid(_dot_nt(hb, wg_ref[D_MODEL:2 * D_MODEL, :]))
    merged = gate_a * _dot(oa_ref[...], woa_ref[...]) + gate_b * _dot(ob_ref[...], wob_ref[...])
    mb = merged.astype(bf16)

    n_chunks = D_FF // FF_CHUNK
    chunk = lambda c: slice(c * FF_CHUNK, (c + 1) * FF_CHUNK)

    def sq_relu(v):
        v = jnp.maximum(v, 0.0)
        return (v * v).astype(bf16)

    x1, h2, up0 = [], [], []
    for r in halves:
        y = _dot(mb[r], wout_ref[...])
        x1.append(x[r] + _rms(y) * gpost_ref[...])
        h2.append((_rms(x1[-1]) * gpre2_ref[...]).astype(bf16))
    for i in range(2):
        up0.append(sq_relu(_dot(h2[i], wup_ref[:, chunk(0)])))
    h2 = jnp.concatenate(h2, axis=0)
    y2 = _dot(jnp.concatenate(up0, axis=0), wdn_ref[chunk(0), :])
    for c in range(1, n_chunks - 1):
        y2 = y2 + _dot(sq_relu(_dot(h2, wup_ref[:, chunk(c)])), wdn_ref[chunk(c), :])
    last = sq_relu(_dot(h2, wup_ref[:, chunk(n_chunks - 1)]))
    for i, r in enumerate(halves):
        y2_half = y2[r] + _dot(last[r], wdn_ref[chunk(n_chunks - 1), :])
        o_ref[r, :] = x1[i] + _rms(y2_half) * gpost2_ref[...]


def _const_spec(shape):
    return pl.BlockSpec(shape, lambda *_: (0,) * len(shape), pipeline_mode=pl.Buffered(1))


def kernel(x, positions, pre_norm_mix, w_in, q_a_norm, w_q_b, kv_a_norm, w_kv_b, sinks, w_o_a,
           w_o_b, w_out, post_norm_mix, pre_norm_mlp, w_up, w_down, post_norm_mlp):
    f32, bf16 = jnp.float32, jnp.bfloat16
    B, S, D = x.shape
    T = B * S
    depth = w_in.shape[0]
    for l in range(depth):
        assert w_in.shape[2] == D_IN
        wit = jnp.swapaxes(w_in[l], 0, 1).astype(bf16)
        ka_t = wit[IN_KA:IN_VA]
        va_t = wit[IN_VA:IN_CQ]
        kr_t = wit[IN_KR:D_IN]
        hr = QK_ROPE // 2
        hd = HEAD_DIM_A
        z = lambda n: jnp.zeros((n, D_MODEL), bf16)
        w_small_t = jnp.concatenate([
            ka_t[:hd], ka_t[:hd], ka_t[hd:], ka_t[hd:],
            z(QK_NOPE), kr_t, z(LANES - Q_HEAD_B),
            z(QK_NOPE), kr_t[hr:], kr_t[:hr], z(LANES - Q_HEAD_B)], axis=0)
        w_va_t = jnp.concatenate([va_t[:hd], z(VT_PAD), va_t[hd:], z(VT_PAD)], axis=0)

        wq = w_q_b[l].reshape(Q_LORA, N_HEADS_B, Q_HEAD_B)
        q_nope, q_rope = wq[..., :QK_NOPE], wq[..., QK_NOPE:]
        zq = lambda n: jnp.zeros((Q_LORA, N_HEADS_B, n), f32)
        wq_main = jnp.concatenate([q_nope, q_rope, zq(LANES - Q_HEAD_B)], -1)
        wq_swap = jnp.concatenate([zq(QK_NOPE), q_rope[..., hr:], q_rope[..., :hr],
                                   zq(LANES - Q_HEAD_B)], -1)
        wq_main = wq_main.reshape(Q_LORA, N_HEADS_B * LANES).astype(bf16)
        wq_swap = wq_swap.reshape(Q_LORA, N_HEADS_B * LANES).astype(bf16)

        wkv = w_kv_b[l].reshape(KV_LORA, N_HEADS_B, KV_HEAD_B)
        kv_k, kv_v = wkv[..., :QK_NOPE], wkv[..., QK_NOPE:]
        zk = jnp.zeros((KV_LORA, N_HEADS_B, HALF), f32)
        wkv_k = jnp.concatenate([kv_k, zk], -1).reshape(KV_LORA, N_HEADS_B * LANES).astype(bf16)
        wkv_vt = jnp.concatenate([kv_v, zk[..., :VT_PAD]], -1)
        wkv_vt = wkv_vt.reshape(KV_LORA, N_HEADS_B * VT_ROWS).T.astype(bf16)

        freq_col = (ROPE_THETA ** (-jnp.arange(0, QK_ROPE, 2, dtype=f32) / QK_ROPE))[:, None]

        row = lambda g: g.reshape(1, -1).astype(f32)
        x2 = x.reshape(T, D)

        tm = PROJ_TM
        assert PROJ_SUB == MLA_BQ and tm % PROJ_SUB == 0 and S % tm == 0
        nkb = S // MLA_BQ
        n_sub = tm // PROJ_SUB
        npb = S // tm
        tok = lambda w: pl.BlockSpec((tm, w), lambda i: (i, 0))
        w_rows = lambda n, start: pl.BlockSpec((n, D), lambda i: (start // n, 0),
                                               pipeline_mode=pl.Buffered(1))
        assert IN_QA % WIDTH_A == 0 and IN_CQ % Q_LORA == 0 and IN_CKV % KV_LORA == 0
        qa, ka_x, va_t3, qb, kb, vt = pl.pallas_call(
            _proj_kernel,
            grid=(T // tm,),
            in_specs=[tok(D), pl.BlockSpec((1, 1, tm), lambda i: (i, 0, 0)), _const_spec((1, D)),
                      w_rows(WIDTH_A, IN_QA), w_rows(Q_LORA, IN_CQ), w_rows(KV_LORA, IN_CKV),
                      _const_spec((SM_ROWS, D)), _const_spec((N_KV_A * VT_ROWS, D)),
                      _const_spec((1, Q_LORA)), _const_spec((Q_LORA, N_HEADS_B * LANES)),
                      _const_spec((Q_LORA, N_HEADS_B * LANES)), _const_spec((1, KV_LORA)),
                      _const_spec((KV_LORA, N_HEADS_B * LANES)),
                      _const_spec((N_HEADS_B * VT_ROWS, KV_LORA)),
                      _const_spec((QK_ROPE // 2, 1))],
            out_specs=[tok(WIDTH_A), tok(2 * LANES),
                       pl.BlockSpec((1, N_KV_A * VT_ROWS, tm), lambda i: (i // npb, 0, i % npb)),
                       tok(N_HEADS_B * LANES), tok(N_HEADS_B * LANES),
                       pl.BlockSpec((1, n_sub, N_HEADS_B * VT_ROWS, PROJ_SUB),
                                    lambda i: (i // npb, i % npb, 0, 0))],
            out_shape=[jax.ShapeDtypeStruct((T, WIDTH_A), bf16),
                       jax.ShapeDtypeStruct((T, 2 * LANES), bf16),
                       jax.ShapeDtypeStruct((B, N_KV_A * VT_ROWS, S), bf16),
                       jax.ShapeDtypeStruct((T, N_HEADS_B * LANES), bf16),
                       jax.ShapeDtypeStruct((T, N_HEADS_B * LANES), bf16),
                       jax.ShapeDtypeStruct((B, nkb, N_HEADS_B * VT_ROWS, PROJ_SUB), bf16)],
            compiler_params=pltpu.CompilerParams(dimension_semantics=("arbitrary",),
                                                 vmem_limit_bytes=VMEM_LIMIT),
            name="proj",
        )(x2, positions.reshape(T // tm, 1, tm), row(pre_norm_mix[l]), wit, wit, wit, w_small_t,
          w_va_t, row(q_a_norm[l]), wq_main, wq_swap, row(kv_a_norm[l]), wkv_k, wkv_vt,
          freq_col)

        tq = SWA_TQ
        nb_per = tq // BLOCK
        qa3 = qa.reshape(B, S, WIDTH_A)
        ka3 = ka_x.reshape(B, S, 2 * LANES)
        pos_r3 = positions.reshape(B, 1, S)
        cur = lambda w: pl.BlockSpec((1, tq, w), lambda b, i: (b, i, 0))
        prev_blk = lambda b, i: jnp.maximum(i * nb_per - 1, 0)
        out_a = pl.pallas_call(
            _swa_kernel,
            grid=(B, S // tq),
            in_specs=[pl.BlockSpec(memory_space=pltpu.SMEM),
                      cur(WIDTH_A), cur(2 * LANES),
                      pl.BlockSpec((1, BLOCK, 2 * LANES), lambda b, i: (b, prev_blk(b, i), 0)),
                      pl.BlockSpec((1, N_KV_A * VT_ROWS, tq), lambda b, i: (b, 0, i)),
                      pl.BlockSpec((1, N_KV_A * VT_ROWS, BLOCK), lambda b, i: (b, 0, prev_blk(b, i))),
                      pl.BlockSpec((1, 1, tq), lambda b, i: (b, 0, i)),
                      pl.BlockSpec((1, 1, BLOCK), lambda b, i: (b, 0, prev_blk(b, i)))],
            out_specs=cur(WIDTH_A),
            out_shape=jax.ShapeDtypeStruct((B, S, WIDTH_A), bf16),
            compiler_params=pltpu.CompilerParams(dimension_semantics=("arbitrary", "arbitrary"),
                                                 vmem_limit_bytes=VMEM_LIMIT),
            name="swa",
        )(sinks[l].astype(f32), qa3, ka3, ka3, va_t3, va_t3, pos_r3, pos_r3)

        bq, nh = MLA_BQ, MLA_HEADS
        qb3 = qb.reshape(B, S, N_HEADS_B * LANES)
        kb3 = kb.reshape(B, S, N_HEADS_B * LANES)
        out_b = pl.pallas_call(
            _mla_kernel,
            grid=(B, N_HEADS_B // nh, S // bq),
            in_specs=[pl.BlockSpec((1, bq, nh * LANES), lambda b, j, i: (b, i, j)),
                      pl.BlockSpec((1, S, nh * LANES), lambda b, j, i: (b, 0, j)),
                      pl.BlockSpec((1, nkb, nh * VT_ROWS, bq), lambda b, j, i: (b, 0, j, 0))],
            out_specs=pl.BlockSpec((1, bq, nh * V_DIM_B), lambda b, j, i: (b, i, j)),
            out_shape=jax.ShapeDtypeStruct((B, S, WIDTH_B), bf16),
            scratch_shapes=[pltpu.VMEM((2, nh, bq, bq), f32),
                            pltpu.VMEM((2, nh, 1, bq), f32),
                            pltpu.VMEM((nh, VT_ROWS, bq), f32), pltpu.VMEM((nh, 1, bq), f32)],
            compiler_params=pltpu.CompilerParams(
                dimension_semantics=("arbitrary", "arbitrary", "arbitrary"),
                vmem_limit_bytes=VMEM_LIMIT),
            name="mla",
        )(qb3, kb3, vt)

        tm = MERGE_TM
        tok = lambda w: pl.BlockSpec((tm, w), lambda i: (i, 0))
        x2 = pl.pallas_call(
            _merge_kernel,
            grid=(T // tm,),
            in_specs=[tok(D), tok(WIDTH_A), tok(WIDTH_B), _const_spec((1, D)),
                      w_rows(2 * D, IN_GATES), _const_spec((WIDTH_A, D)), _const_spec((WIDTH_B, D)),
                      _const_spec((D, D)), _const_spec((1, D)), _const_spec((1, D)),
                      _const_spec((D, D_FF)), _const_spec((D_FF, D)), _const_spec((1, D))],
            out_specs=tok(D),
            out_shape=jax.ShapeDtypeStruct((T, D), f32),
            compiler_params=pltpu.CompilerParams(dimension_semantics=("arbitrary",),
                                                 vmem_limit_bytes=VMEM_LIMIT),
            name="merge_mlp",
        )(x2, out_a.reshape(T, WIDTH_A), out_b.reshape(T, WIDTH_B), row(pre_norm_mix[l]), wit,
          w_o_a[l].astype(bf16), w_o_b[l].astype(bf16), w_out[l].astype(bf16),
          row(post_norm_mix[l]), row(pre_norm_mlp[l]), w_up[l].astype(bf16),
          w_down[l].astype(bf16), row(post_norm_mlp[l]))
        x = x2.reshape(B, S, D)
    return x
```

```python
import functools

import jax
import jax.numpy as jnp
from jax import lax
from jax.experimental import pallas as pl
from jax.experimental.pallas import tpu as pltpu

D_MODEL = 1024
N_HEADS_A = 8
N_KV_A = 2
HEAD_DIM_A = 64
WINDOW = 128
BLOCK = 128
N_HEADS_B = 8
QK_NOPE = 64
QK_ROPE = 32
V_DIM_B = 64
Q_LORA = 256
KV_LORA = 128
ROPE_THETA = 10000.0
D_FF = 4 * D_MODEL
EPS = 1e-6

WIDTH_A = N_HEADS_A * HEAD_DIM_A
WIDTH_B = N_HEADS_B * V_DIM_B
Q_HEAD_B = QK_NOPE + QK_ROPE
KV_HEAD_B = QK_NOPE + V_DIM_B

LANES = 128
HALF = LANES // 2
VT_ROWS = LANES
VT_PAD = VT_ROWS - V_DIM_B
SCALE_A = HEAD_DIM_A ** -0.5
SCALE_B = Q_HEAD_B ** -0.5
LOG2_E = 1.4426950408889634
QSCALE_A = SCALE_A * LOG2_E
QSCALE_B = SCALE_B * LOG2_E
ALIBI_SLOPES = tuple(2.0 ** (-8.0 * (h + 1) / N_HEADS_A) for h in range(N_HEADS_A))

IN_GATES = 0
IN_QA = 2 * D_MODEL
IN_KA = IN_QA + WIDTH_A
IN_VA = IN_KA + N_KV_A * HEAD_DIM_A
IN_CQ = IN_VA + N_KV_A * HEAD_DIM_A
IN_CKV = IN_CQ + Q_LORA
IN_KR = IN_CKV + KV_LORA
D_IN = IN_KR + QK_ROPE
SM_KA = 0
SM_KR = SM_KA + 2 * LANES
SM_KRS = SM_KR + LANES
SM_ROWS = SM_KRS + LANES

PROJ_TM = 1024
PROJ_SUB = 512
SWA_TQ = 1024
SWA_LEAD = 4
MLA_BQ = 512
MLA_CHUNK = 512
MLA_HEADS = 4
MLA_LEAD = 1
MLA_QPS = 4
MERGE_TM = 1024
FF_CHUNK = 1024
VMEM_LIMIT = 60 * 1024 * 1024


def _rms(v):
    return v * lax.rsqrt(jnp.mean(v * v, axis=-1, keepdims=True) + EPS)


def _dot(a, b):
    return jnp.dot(a, b, preferred_element_type=jnp.float32)


def _dot_nt(a, b):
    return lax.dot_general(a, b, (((1,), (1,)), ((), ())), preferred_element_type=jnp.float32)


def _proj_kernel(x_ref, pos_ref, gpre_ref, wqa_ref, wcq_ref, wckv_ref, wsm_ref, wvat_ref, gq_ref,
                 wqm_ref, wqs_ref, gkv_ref, wkk_ref, wkvt_ref, freq_ref,
                 qa_ref, ka_ref, vat_ref, qb_ref, kb_ref, vt_ref):
    bf16 = jnp.bfloat16
    sub = PROJ_SUB

    def first_stage(n):
        rows = slice(n * sub, (n + 1) * sub)
        hb = (_rms(x_ref[rows, :]) * gpre_ref[...]).astype(bf16)
        qa_ref[rows, :] = (_dot_nt(hb, wqa_ref[...]) * QSCALE_A).astype(bf16)
        small = _dot_nt(hb, wsm_ref[...])
        ka_ref[rows, :] = small[:, SM_KA:SM_KR].astype(bf16)
        va_t = _dot_nt(wvat_ref[...], hb)
        varow = lax.broadcasted_iota(jnp.int32, va_t.shape, 0)
        vat_ref[0, :, rows] = jnp.where(varow % VT_ROWS == HEAD_DIM_A, 1.0, va_t).astype(bf16)
        cq = _dot_nt(hb, wcq_ref[...])
        ckv = _dot_nt(hb, wckv_ref[...])
        pos = pos_ref[0, :, rows].astype(jnp.float32)
        ang = freq_ref[...] * pos
        cos_t, sin_t = jnp.cos(ang), jnp.sin(ang)
        one = jnp.ones((QK_NOPE, sub), jnp.float32)
        zero = jnp.zeros((QK_NOPE, sub), jnp.float32)
        pad = LANES - Q_HEAD_B
        cos = jnp.concatenate([one, cos_t, cos_t, one[:pad]], axis=0).T
        sin = jnp.concatenate([zero, -sin_t, sin_t, zero[:pad]], axis=0).T
        k_rot = small[:, SM_KR:SM_KRS] * cos + small[:, SM_KRS:SM_ROWS] * sin
        return cq, ckv, cos, sin, k_rot

    def second_stage(n, cq, ckv, cos, sin, k_rot):
        rows = slice(n * sub, (n + 1) * sub)
        cqn = (_rms(cq) * gq_ref[...]).astype(bf16)
        q_main = _dot(cqn, wqm_ref[...])
        q_swap = _dot(cqn, wqs_ref[...])
        ckvn = (_rms(ckv) * gkv_ref[...]).astype(bf16)
        k_nope = _dot(ckvn, wkk_ref[...])
        v_t = _dot_nt(wkvt_ref[...], ckvn)
        vrow = lax.broadcasted_iota(jnp.int32, v_t.shape, 0)
        vt_ref[0, n] = jnp.where(vrow % VT_ROWS == V_DIM_B, 1.0, v_t).astype(bf16)
        for h in range(N_HEADS_B):
            sl = slice(h * LANES, (h + 1) * LANES)
            qb_ref[rows, sl] = ((q_main[:, sl] * cos + q_swap[:, sl] * sin) * QSCALE_B).astype(bf16)
            kb_ref[rows, sl] = (k_nope[:, sl] + k_rot).astype(bf16)

    n_sub = x_ref.shape[0] // sub
    staged = [first_stage(n) for n in range(n_sub)]
    for n in range(n_sub):
        second_stage(n, *staged[n])


def _swa_kernel(sink_ref, q_ref, kc_ref, kp_ref, vtc_ref, vtp_ref, posc_ref, posp_ref, o_ref):
    bf16 = jnp.bfloat16
    i = pl.program_id(1)
    krow = lax.broadcasted_iota(jnp.int32, (2 * BLOCK, BLOCK), 0)
    qcol = lax.broadcasted_iota(jnp.int32, (2 * BLOCK, BLOCK), 1)
    ahead = krow - qcol
    band = (ahead > 0) & (ahead <= WINDOW)
    lane = lax.broadcasted_iota(jnp.int32, (BLOCK, LANES), 1)
    low_half = lane < HALF
    pos_inf = jnp.float32(jnp.inf)

    n_pairs = N_HEADS_A // 2
    chains = [(blk, pair) for blk in range(SWA_TQ // BLOCK) for pair in range(n_pairs)]
    bands = {}

    def band_of(blk):
        if blk not in bands:
            r0 = blk * BLOCK
            if blk == 0:
                kband = jnp.concatenate([kp_ref[0], kc_ref[0, 0:BLOCK, :]], axis=0)
                vtband = jnp.concatenate([vtp_ref[0], vtc_ref[0, :, 0:BLOCK]], axis=1)
                kpos = jnp.concatenate([posp_ref[0], posc_ref[0, :, 0:BLOCK]], axis=1)
                mask = band & ((krow >= BLOCK) | (i > 0))
            else:
                kband = kc_ref[0, r0 - BLOCK:r0 + BLOCK, :]
                vtband = vtc_ref[0, :, r0 - BLOCK:r0 + BLOCK]
                kpos = posc_ref[0, :, r0 - BLOCK:r0 + BLOCK]
                mask = band
            qpos = posc_ref[0, :, r0:r0 + BLOCK]
            kpos_col = jnp.broadcast_to(kpos, (BLOCK, 2 * BLOCK)).T
            dist = jnp.abs(kpos_col - qpos).astype(jnp.float32) * LOG2_E
            dist = jnp.where(mask, dist, pos_inf)
            bands[blk] = (kband, vtband, dist)
        return bands[blk]

    def scores(blk, pair):
        kv = (2 * pair) // (N_HEADS_A // N_KV_A)
        r0 = blk * BLOCK
        qp = q_ref[0, r0:r0 + BLOCK, pair * LANES:(pair + 1) * LANES]
        zero = jnp.zeros_like(qp)
        q2 = jnp.concatenate([jnp.where(low_half, qp, zero), jnp.where(low_half, zero, qp)],
                             axis=0)
        kx = band_of(blk)[0][:, kv * LANES:(kv + 1) * LANES]
        return _dot_nt(kx, q2)

    def finish(blk, pair, s2):
        kv = (2 * pair) // (N_HEADS_A // N_KV_A)
        r0 = blk * BLOCK
        _, vtband, dist = band_of(blk)
        vt = vtband[kv * VT_ROWS:(kv + 1) * VT_ROWS, :]
        ps, ms = [], []
        for e in range(2):
            h = 2 * pair + e
            s = s2[:, e * BLOCK:(e + 1) * BLOCK] - ALIBI_SLOPES[h] * dist
            m = jnp.maximum(jnp.max(s, axis=0, keepdims=True), sink_ref[h] * LOG2_E)
            ps.append(jnp.exp2(s - m).astype(bf16))
            ms.append(m)
        o2 = _dot(vt, jnp.concatenate(ps, axis=1))
        outs = []
        for e in range(2):
            h = 2 * pair + e
            o_t = o2[:, e * BLOCK:(e + 1) * BLOCK]
            denom = (o_t[HEAD_DIM_A:HEAD_DIM_A + 1]
                     + jnp.exp2(sink_ref[h] * LOG2_E - ms[e]))
            outs.append(o_t[0:HEAD_DIM_A] / denom)
        o_ref[0, r0:r0 + BLOCK, pair * LANES:(pair + 1) * LANES] = (
            jnp.concatenate(outs, axis=0).T.astype(bf16))

    pending = [scores(*c) for c in chains[:SWA_LEAD]]
    for n, c in enumerate(chains):
        s2 = pending.pop(0)
        if n + SWA_LEAD < len(chains):
            pending.append(scores(*chains[n + SWA_LEAD]))
        finish(*c, s2)


def _mla_kernel(q_ref, k_ref, vt_ref, o_ref, s_ref, bmax_ref, acc_ref, m_ref):
    bf16 = jnp.bfloat16
    bq = MLA_BQ
    ch = MLA_CHUNK
    krow = lax.broadcasted_iota(jnp.int32, (bq, ch), 0)
    qcol = lax.broadcasted_iota(jnp.int32, (bq, ch), 1)
    neg_inf = jnp.float32(-jnp.inf)
    cur = {}

    def scores_head(buf, kb, e):
        start = pl.multiple_of(kb * bq, bq)
        sl = slice(e * LANES, (e + 1) * LANES)
        s = _dot_nt(k_ref[0, pl.ds(start, bq), sl], q_ref[0, cur["q_rows"], sl])
        s_ref[buf, e] = s
        bmax_ref[buf, e] = jnp.max(s, axis=0, keepdims=True)

    def scores_into(buf, kb):
        for e in range(MLA_HEADS):
            scores_head(buf, kb, e)

    def softmax_head(buf, e, masked):
        out = []
        for c in range(bq // ch):
            cs = slice(c * ch, (c + 1) * ch)
            s = s_ref[buf, e, :, cs]
            if masked:
                s = jnp.where(krow <= qcol + c * ch, s, neg_inf)
                blockmax = jnp.max(s, axis=0, keepdims=True)
            else:
                blockmax = bmax_ref[buf, e, :, cs]
            m_prev = m_ref[e, :, cs]
            m_new = jnp.maximum(m_prev, blockmax)
            m_ref[e, :, cs] = m_new
            out.append((jnp.exp2(s - m_new).astype(bf16), jnp.exp2(m_prev - m_new)))
        return out

    def pv_head(kb, e, weights):
        vt = vt_ref[0, kb, e * VT_ROWS:(e + 1) * VT_ROWS, :]
        for c, (p, alpha) in enumerate(weights):
            cs = slice(c * ch, (c + 1) * ch)
            acc_ref[e, :, cs] = alpha * acc_ref[e, :, cs] + _dot(vt, p)

    def softmax_pv(buf, kb, masked):
        for e in range(MLA_HEADS):
            pv_head(kb, e, softmax_head(buf, e, masked))

    def pipelined(first_kb, n_blocks):
        blocks = [(j % 2, first_kb + j) for j in range(n_blocks)]
        units = [(buf, kb, e) for buf, kb in blocks for e in range(MLA_HEADS)]
        prods = [(1 - buf, kb + 1, e) for buf, kb in blocks for e in range(MLA_HEADS)]
        for n in range(min(MLA_LEAD, len(prods))):
            scores_head(*prods[n])
        for n, (buf, kb, e) in enumerate(units):
            pv_head(kb, e, softmax_head(buf, e, False))
            if n + MLA_LEAD < len(prods):
                scores_head(*prods[n + MLA_LEAD])

    def quad_body(t, carry):
        pipelined(4 * t, 4)
        return carry

    def query_block(r, carry):
        qi = pl.program_id(2) * MLA_QPS + r
        cur["q_rows"] = pl.ds(pl.multiple_of(r * bq, bq), bq)
        m_ref[...] = jnp.full(m_ref.shape, neg_inf, jnp.float32)
        acc_ref[...] = jnp.zeros(acc_ref.shape, jnp.float32)

        scores_into(0, 0)
        lax.fori_loop(0, qi // 4, quad_body, 0)

        @pl.when(qi % 4 >= 2)
        def _():
            pipelined((qi // 4) * 4, 2)

        @pl.when(qi % 2 == 1)
        def _():
            pipelined(qi - 1, 1)
            softmax_pv(1, qi, True)

        @pl.when(qi % 2 == 0)
        def _():
            softmax_pv(0, qi, True)

        outs = []
        for e in range(MLA_HEADS):
            acc = acc_ref[e]
            outs.append(acc[0:V_DIM_B] / acc[V_DIM_B:V_DIM_B + 1])
        o_ref[0, cur["q_rows"], :] = jnp.concatenate(outs, axis=0).T.astype(bf16)
        return carry

    lax.fori_loop(0, MLA_QPS, query_block, 0)


def _merge_kernel(x_ref, oa_ref, ob_ref, gpre_ref, wg_ref, woa_ref, wob_ref, wout_ref, gpost_ref,
                  gpre2_ref, wup_ref, wdn_ref, gpost2_ref, o_ref):
    bf16 = jnp.bfloat16
    x = x_ref[...]
    hm = x.shape[0] // 2
    halves = (slice(0, hm), slice(hm, 2 * hm))
    hb = [(_rms(x[r]) * gpre_ref[...]).astype(bf16) for r in halves]
    gate_a = jnp.concatenate([_dot_nt(h, wg_ref[0:D_MODEL, :]) for h in hb], axis=0)
    hb = jnp.concatenate(hb, axis=0)
    gate_a = jax.nn.sigmoid(gate_a)
    gate_b = jax.nn.sigmoid(_dot_nt(hb, wg_ref[D_MODEL:2 * D_MODEL, :]))
    merged = gate_a * _dot(oa_ref[...], woa_ref[...]) + gate_b * _dot(ob_ref[...], wob_ref[...])
    mb = merged.astype(bf16)

    n_chunks = D_FF // FF_CHUNK
    chunk = lambda c: slice(c * FF_CHUNK, (c + 1) * FF_CHUNK)

    def sq_relu(v):
        v = jnp.maximum(v, 0.0)
        return (v * v).astype(bf16)

    x1, h2, up0 = [], [], []
    for r in halves:
        y = _dot(mb[r], wout_ref[...])
        x1.append(x[r] + _rms(y) * gpost_ref[...])
        h2.append((_rms(x1[-1]) * gpre2_ref[...]).astype(bf16))
    for i in range(2):
        up0.append(sq_relu(_dot(h2[i], wup_ref[:, chunk(0)])))
    h2 = jnp.concatenate(h2, axis=0)
    y2 = _dot(jnp.concatenate(up0, axis=0), wdn_ref[chunk(0), :])
    for c in range(1, n_chunks - 1):
        y2 = y2 + _dot(sq_relu(_dot(h2, wup_ref[:, chunk(c)])), wdn_ref[chunk(c), :])
    last = sq_relu(_dot(h2, wup_ref[:, chunk(n_chunks - 1)]))
    for i, r in enumerate(halves):
        y2_half = y2[r] + _dot(last[r], wdn_ref[chunk(n_chunks - 1), :])
        o_ref[r, :] = x1[i] + _rms(y2_half) * gpost2_ref[...]


def _const_spec(shape):
    return pl.BlockSpec(shape, lambda *_: (0,) * len(shape), pipeline_mode=pl.Buffered(1))


def kernel(x, positions, pre_norm_mix, w_in, q_a_norm, w_q_b, kv_a_norm, w_kv_b, sinks, w_o_a,
           w_o_b, w_out, post_norm_mix, pre_norm_mlp, w_up, w_down, post_norm_mlp):
    f32, bf16 = jnp.float32, jnp.bfloat16
    B, S, D = x.shape
    T = B * S
    depth = w_in.shape[0]
    for l in range(depth):
        assert w_in.shape[2] == D_IN
        wit = jnp.swapaxes(w_in[l], 0, 1).astype(bf16)
        ka_t = wit[IN_KA:IN_VA]
        va_t = wit[IN_VA:IN_CQ]
        kr_t = wit[IN_KR:D_IN]
        hr = QK_ROPE // 2
        hd = HEAD_DIM_A
        z = lambda n: jnp.zeros((n, D_MODEL), bf16)
        w_small_t = jnp.concatenate([
            ka_t[:hd], ka_t[:hd], ka_t[hd:], ka_t[hd:],
            z(QK_NOPE), kr_t, z(LANES - Q_HEAD_B),
            z(QK_NOPE), kr_t[hr:], kr_t[:hr], z(LANES - Q_HEAD_B)], axis=0)
        w_va_t = jnp.concatenate([va_t[:hd], z(VT_PAD), va_t[hd:], z(VT_PAD)], axis=0)

        wq = w_q_b[l].reshape(Q_LORA, N_HEADS_B, Q_HEAD_B)
        q_nope, q_rope = wq[..., :QK_NOPE], wq[..., QK_NOPE:]
        zq = lambda n: jnp.zeros((Q_LORA, N_HEADS_B, n), f32)
        wq_main = jnp.concatenate([q_nope, q_rope, zq(LANES - Q_HEAD_B)], -1)
        wq_swap = jnp.concatenate([zq(QK_NOPE), q_rope[..., hr:], q_rope[..., :hr],
                                   zq(LANES - Q_HEAD_B)], -1)
        wq_main = wq_main.reshape(Q_LORA, N_HEADS_B * LANES).astype(bf16)
        wq_swap = wq_swap.reshape(Q_LORA, N_HEADS_B * LANES).astype(bf16)

        wkv = w_kv_b[l].reshape(KV_LORA, N_HEADS_B, KV_HEAD_B)
        kv_k, kv_v = wkv[..., :QK_NOPE], wkv[..., QK_NOPE:]
        zk = jnp.zeros((KV_LORA, N_HEADS_B, HALF), f32)
        wkv_k = jnp.concatenate([kv_k, zk], -1).reshape(KV_LORA, N_HEADS_B * LANES).astype(bf16)
        wkv_vt = jnp.concatenate([kv_v, zk[..., :VT_PAD]], -1)
        wkv_vt = wkv_vt.reshape(KV_LORA, N_HEADS_B * VT_ROWS).T.astype(bf16)

        freq_col = (ROPE_THETA ** (-jnp.arange(0, QK_ROPE, 2, dtype=f32) / QK_ROPE))[:, None]

        row = lambda g: g.reshape(1, -1).astype(f32)
        x2 = x.reshape(T, D)

        tm = PROJ_TM
        assert PROJ_SUB == MLA_BQ and tm % PROJ_SUB == 0 and S % tm == 0
        nkb = S // MLA_BQ
        n_sub = tm // PROJ_SUB
        npb = S // tm
        tok = lambda w: pl.BlockSpec((tm, w), lambda i: (i, 0))
        w_rows = lambda n, start: pl.BlockSpec((n, D), lambda i: (start // n, 0),
                                               pipeline_mode=pl.Buffered(1))
        assert IN_QA % WIDTH_A == 0 and IN_CQ % Q_LORA == 0 and IN_CKV % KV_LORA == 0
        qa, ka_x, va_t3, qb, kb, vt = pl.pallas_call(
            _proj_kernel,
            grid=(T // tm,),
            in_specs=[tok(D), pl.BlockSpec((1, 1, tm), lambda i: (i, 0, 0)), _const_spec((1, D)),
                      w_rows(WIDTH_A, IN_QA), w_rows(Q_LORA, IN_CQ), w_rows(KV_LORA, IN_CKV),
                      _const_spec((SM_ROWS, D)), _const_spec((N_KV_A * VT_ROWS, D)),
                      _const_spec((1, Q_LORA)), _const_spec((Q_LORA, N_HEADS_B * LANES)),
                      _const_spec((Q_LORA, N_HEADS_B * LANES)), _const_spec((1, KV_LORA)),
                      _const_spec((KV_LORA, N_HEADS_B * LANES)),
                      _const_spec((N_HEADS_B * VT_ROWS, KV_LORA)),
                      _const_spec((QK_ROPE // 2, 1))],
            out_specs=[tok(WIDTH_A), tok(2 * LANES),
                       pl.BlockSpec((1, N_KV_A * VT_ROWS, tm), lambda i: (i // npb, 0, i % npb)),
                       tok(N_HEADS_B * LANES), tok(N_HEADS_B * LANES),
                       pl.BlockSpec((1, n_sub, N_HEADS_B * VT_ROWS, PROJ_SUB),
                                    lambda i: (i // npb, i % npb, 0, 0))],
            out_shape=[jax.ShapeDtypeStruct((T, WIDTH_A), bf16),
                       jax.ShapeDtypeStruct((T, 2 * LANES), bf16),
                       jax.ShapeDtypeStruct((B, N_KV_A * VT_ROWS, S), bf16),
                       jax.ShapeDtypeStruct((T, N_HEADS_B * LANES), bf16),
                       jax.ShapeDtypeStruct((T, N_HEADS_B * LANES), bf16),
                       jax.ShapeDtypeStruct((B, nkb, N_HEADS_B * VT_ROWS, PROJ_SUB), bf16)],
            compiler_params=pltpu.CompilerParams(dimension_semantics=("arbitrary",),
                                                 vmem_limit_bytes=VMEM_LIMIT),
            name="proj",
        )(x2, positions.reshape(T // tm, 1, tm), row(pre_norm_mix[l]), wit, wit, wit, w_small_t,
          w_va_t, row(q_a_norm[l]), wq_main, wq_swap, row(kv_a_norm[l]), wkv_k, wkv_vt,
          freq_col)

        tq = SWA_TQ
        nb_per = tq // BLOCK
        qa3 = qa.reshape(B, S, WIDTH_A)
        ka3 = ka_x.reshape(B, S, 2 * LANES)
        pos_r3 = positions.reshape(B, 1, S)
        cur = lambda w: pl.BlockSpec((1, tq, w), lambda b, i: (b, i, 0))
        prev_blk = lambda b, i: jnp.maximum(i * nb_per - 1, 0)
        out_a = pl.pallas_call(
            _swa_kernel,
            grid=(B, S // tq),
            in_specs=[pl.BlockSpec(memory_space=pltpu.SMEM),
                      cur(WIDTH_A), cur(2 * LANES),
                      pl.BlockSpec((1, BLOCK, 2 * LANES), lambda b, i: (b, prev_blk(b, i), 0)),
                      pl.BlockSpec((1, N_KV_A * VT_ROWS, tq), lambda b, i: (b, 0, i)),
                      pl.BlockSpec((1, N_KV_A * VT_ROWS, BLOCK), lambda b, i: (b, 0, prev_blk(b, i))),
                      pl.BlockSpec((1, 1, tq), lambda b, i: (b, 0, i)),
                      pl.BlockSpec((1, 1, BLOCK), lambda b, i: (b, 0, prev_blk(b, i)))],
            out_specs=cur(WIDTH_A),
            out_shape=jax.ShapeDtypeStruct((B, S, WIDTH_A), bf16),
            compiler_params=pltpu.CompilerParams(dimension_semantics=("arbitrary", "arbitrary"),
                                                 vmem_limit_bytes=VMEM_LIMIT),
            name="swa",
        )(sinks[l].astype(f32), qa3, ka3, ka3, va_t3, va_t3, pos_r3, pos_r3)

        bq, nh = MLA_BQ, MLA_HEADS
        qb3 = qb.reshape(B, S, N_HEADS_B * LANES)
        kb3 = kb.reshape(B, S, N_HEADS_B * LANES)
        out_b = pl.pallas_call(
            _mla_kernel,
            grid=(B, N_HEADS_B // nh, S // (bq * MLA_QPS)),
            in_specs=[pl.BlockSpec((1, bq * MLA_QPS, nh * LANES), lambda b, j, i: (b, i, j)),
                      pl.BlockSpec((1, S, nh * LANES), lambda b, j, i: (b, 0, j)),
                      pl.BlockSpec((1, nkb, nh * VT_ROWS, bq), lambda b, j, i: (b, 0, j, 0))],
            out_specs=pl.BlockSpec((1, bq * MLA_QPS, nh * V_DIM_B), lambda b, j, i: (b, i, j)),
            out_shape=jax.ShapeDtypeStruct((B, S, WIDTH_B), bf16),
            scratch_shapes=[pltpu.VMEM((2, nh, bq, bq), f32),
                            pltpu.VMEM((2, nh, 1, bq), f32),
                            pltpu.VMEM((nh, VT_ROWS, bq), f32), pltpu.VMEM((nh, 1, bq), f32)],
            compiler_params=pltpu.CompilerParams(
                dimension_semantics=("arbitrary", "arbitrary", "arbitrary"),
                vmem_limit_bytes=VMEM_LIMIT),
            name="mla",
        )(qb3, kb3, vt)

        tm = MERGE_TM
        tok = lambda w: pl.BlockSpec((tm, w), lambda i: (i, 0))
        x2 = pl.pallas_call(
            _merge_kernel,
            grid=(T // tm,),
            in_specs=[tok(D), tok(WIDTH_A), tok(WIDTH_B), _const_spec((1, D)),
                      w_rows(2 * D, IN_GATES), _const_spec((WIDTH_A, D)), _const_spec((WIDTH_B, D)),
                      _const_spec((D, D)), _const_spec((1, D)), _const_spec((1, D)),
                      _const_spec((D, D_FF)), _const_spec((D_FF, D)), _const_spec((1, D))],
            out_specs=tok(D),
            out_shape=jax.ShapeDtypeStruct((T, D), f32),
            compiler_params=pltpu.CompilerParams(dimension_semantics=("arbitrary",),
                                                 vmem_limit_bytes=VMEM_LIMIT),
            name="merge_mlp",
        )(x2, out_a.reshape(T, WIDTH_A), out_b.reshape(T, WIDTH_B), row(pre_norm_mix[l]), wit,
          w_o_a[l].astype(bf16), w_o_b[l].astype(bf16), w_out[l].astype(bf16),
          row(post_norm_mix[l]), row(pre_norm_mlp[l]), w_up[l].astype(bf16),
          w_down[l].astype(bf16), row(post_norm_mlp[l]))
        x = x2.reshape(B, S, D)
    return x
```

```python
import jax
import jax.numpy as jnp
from jax import lax
from jax.experimental import pallas as pl
from jax.experimental.pallas import tpu as pltpu

D_MODEL = 1024
N_HEADS_A = 8
N_KV_A = 2
HEAD_DIM_A = 64
WINDOW = 128
BLOCK = 128
N_HEADS_B = 8
QK_NOPE = 64
QK_ROPE = 32
V_DIM_B = 64
Q_LORA = 256
KV_LORA = 128
ROPE_THETA = 10000.0
D_FF = 4 * D_MODEL
EPS = 1e-6

WIDTH_A = N_HEADS_A * HEAD_DIM_A
WIDTH_B = N_HEADS_B * V_DIM_B
Q_HEAD_B = QK_NOPE + QK_ROPE
KV_HEAD_B = QK_NOPE + V_DIM_B

LANES = 128
HALF = LANES // 2
VT_ROWS = LANES
VT_PAD = VT_ROWS - V_DIM_B
SCALE_A = HEAD_DIM_A ** -0.5
SCALE_B = Q_HEAD_B ** -0.5
LOG2_E = 1.4426950408889634
QSCALE_A = SCALE_A * LOG2_E
QSCALE_B = SCALE_B * LOG2_E
ALIBI_SLOPES = tuple(2.0 ** (-8.0 * (h + 1) / N_HEADS_A) for h in range(N_HEADS_A))

IN_GATES = 0
IN_QA = 2 * D_MODEL
IN_KA = IN_QA + WIDTH_A
IN_VA = IN_KA + N_KV_A * HEAD_DIM_A
IN_CQ = IN_VA + N_KV_A * HEAD_DIM_A
IN_CKV = IN_CQ + Q_LORA
IN_KR = IN_CKV + KV_LORA
D_IN = IN_KR + QK_ROPE
SM_KA = 0
SM_KR = SM_KA + 2 * LANES
SM_KRS = SM_KR + LANES
SM_ROWS = SM_KRS + LANES

PROJ_TM = 1024
PROJ_SUB = 512
SWA_TQ = 2048
SWA_LEAD = 4
MLA_BQ = 512
MLA_CHUNK = 512
MLA_HEADS = 4
MLA_LEAD = 1
MERGE_TM = 512
FF_CHUNK = 1024
V7X_VMEM_BYTES = 64 * 1024 * 1024
VMEM_LIMIT = V7X_VMEM_BYTES - 4 * 1024 * 1024


def _rms(v):
    return v * lax.rsqrt(jnp.mean(v * v, axis=-1, keepdims=True) + EPS)


def _dot(a, b):
    return jnp.dot(a, b, preferred_element_type=jnp.float32)


def _dot_nt(a, b):
    return lax.dot_general(a, b, (((1,), (1,)), ((), ())), preferred_element_type=jnp.float32)


def _proj_kernel(x_ref, pos_ref, gpre_ref, wqa_ref, wcq_ref, wckv_ref, wsm_ref, wvat_ref, gq_ref,
                 wqm_ref, wqs_ref, gkv_ref, wkk_ref, wkvt_ref, freq_ref,
                 qa_ref, ka_ref, vat_ref, qb_ref, kb_ref, vt_ref):
    bf16 = jnp.bfloat16
    sub = PROJ_SUB

    def first_stage(n):
        rows = slice(n * sub, (n + 1) * sub)
        hb = (_rms(x_ref[rows, :]) * gpre_ref[...]).astype(bf16)
        qa_ref[rows, :] = (_dot_nt(hb, wqa_ref[...]) * QSCALE_A).astype(bf16)
        small = _dot_nt(hb, wsm_ref[...])
        ka_ref[rows, :] = small[:, SM_KA:SM_KR].astype(bf16)
        va_t = _dot_nt(wvat_ref[...], hb)
        varow = lax.broadcasted_iota(jnp.int32, va_t.shape, 0)
        vat_ref[0, :, rows] = jnp.where(varow % VT_ROWS == HEAD_DIM_A, 1.0, va_t).astype(bf16)
        cq = _dot_nt(hb, wcq_ref[...])
        ckv = _dot_nt(hb, wckv_ref[...])
        pos = pos_ref[0, :, rows].astype(jnp.float32)
        ang = freq_ref[...] * pos
        cos_t, sin_t = jnp.cos(ang), jnp.sin(ang)
        one = jnp.ones((QK_NOPE, sub), jnp.float32)
        zero = jnp.zeros((QK_NOPE, sub), jnp.float32)
        pad = LANES - Q_HEAD_B
        cos = jnp.concatenate([one, cos_t, cos_t, one[:pad]], axis=0).T
        sin = jnp.concatenate([zero, -sin_t, sin_t, zero[:pad]], axis=0).T
        k_rot = small[:, SM_KR:SM_KRS] * cos + small[:, SM_KRS:SM_ROWS] * sin
        return cq, ckv, cos, sin, k_rot

    def second_stage(n, cq, ckv, cos, sin, k_rot):
        rows = slice(n * sub, (n + 1) * sub)
        cqn = (_rms(cq) * gq_ref[...]).astype(bf16)
        q_main = _dot(cqn, wqm_ref[...])
        q_swap = _dot(cqn, wqs_ref[...])
        ckvn = (_rms(ckv) * gkv_ref[...]).astype(bf16)
        k_nope = _dot(ckvn, wkk_ref[...])
        v_t = _dot_nt(wkvt_ref[...], ckvn)
        vrow = lax.broadcasted_iota(jnp.int32, v_t.shape, 0)
        vt_ref[0, n] = jnp.where(vrow % VT_ROWS == V_DIM_B, 1.0, v_t).astype(bf16)
        for h in range(N_HEADS_B):
            sl = slice(h * LANES, (h + 1) * LANES)
            qb_ref[rows, sl] = ((q_main[:, sl] * cos + q_swap[:, sl] * sin) * QSCALE_B).astype(bf16)
            kb_ref[rows, sl] = (k_nope[:, sl] + k_rot).astype(bf16)

    n_sub = x_ref.shape[0] // sub
    staged = [first_stage(n) for n in range(n_sub)]
    for n in range(n_sub):
        second_stage(n, *staged[n])


def _swa_kernel(sink_ref, q_ref, kc_ref, kp_ref, vtc_ref, vtp_ref, posc_ref, posp_ref, o_ref):
    bf16 = jnp.bfloat16
    i = pl.program_id(1)
    krow = lax.broadcasted_iota(jnp.int32, (2 * BLOCK, BLOCK), 0)
    qcol = lax.broadcasted_iota(jnp.int32, (2 * BLOCK, BLOCK), 1)
    ahead = krow - qcol
    band = (ahead > 0) & (ahead <= WINDOW)
    lane = lax.broadcasted_iota(jnp.int32, (BLOCK, LANES), 1)
    low_half = lane < HALF
    pos_inf = jnp.float32(jnp.inf)

    n_pairs = N_HEADS_A // 2
    chains = [(blk, pair) for blk in range(SWA_TQ // BLOCK) for pair in range(n_pairs)]
    bands = {}

    def band_of(blk):
        if blk not in bands:
            r0 = blk * BLOCK
            if blk == 0:
                kband = jnp.concatenate([kp_ref[0], kc_ref[0, 0:BLOCK, :]], axis=0)
                vtband = jnp.concatenate([vtp_ref[0], vtc_ref[0, :, 0:BLOCK]], axis=1)
                kpos = jnp.concatenate([posp_ref[0], posc_ref[0, :, 0:BLOCK]], axis=1)
                mask = band & ((krow >= BLOCK) | (i > 0))
            else:
                kband = kc_ref[0, r0 - BLOCK:r0 + BLOCK, :]
                vtband = vtc_ref[0, :, r0 - BLOCK:r0 + BLOCK]
                kpos = posc_ref[0, :, r0 - BLOCK:r0 + BLOCK]
                mask = band
            qpos = posc_ref[0, :, r0:r0 + BLOCK]
            kpos_col = jnp.broadcast_to(kpos, (BLOCK, 2 * BLOCK)).T
            dist = jnp.abs(kpos_col - qpos).astype(jnp.float32) * LOG2_E
            dist = jnp.where(mask, dist, pos_inf)
            bands[blk] = (kband, vtband, dist)
        return bands[blk]

    def scores(blk, pair):
        kv = (2 * pair) // (N_HEADS_A // N_KV_A)
        r0 = blk * BLOCK
        qp = q_ref[0, r0:r0 + BLOCK, pair * LANES:(pair + 1) * LANES]
        zero = jnp.zeros_like(qp)
        q2 = jnp.concatenate([jnp.where(low_half, qp, zero), jnp.where(low_half, zero, qp)],
                             axis=0)
        kx = band_of(blk)[0][:, kv * LANES:(kv + 1) * LANES]
        return _dot_nt(kx, q2)

    def finish(blk, pair, s2):
        kv = (2 * pair) // (N_HEADS_A // N_KV_A)
        r0 = blk * BLOCK
        _, vtband, dist = band_of(blk)
        vt = vtband[kv * VT_ROWS:(kv + 1) * VT_ROWS, :]
        ps, ms = [], []
        for e in range(2):
            h = 2 * pair + e
            s = s2[:, e * BLOCK:(e + 1) * BLOCK] - ALIBI_SLOPES[h] * dist
            m = jnp.maximum(jnp.max(s, axis=0, keepdims=True), sink_ref[h] * LOG2_E)
            ps.append(jnp.exp2(s - m).astype(bf16))
            ms.append(m)
        o2 = _dot(vt, jnp.concatenate(ps, axis=1))
        outs = []
        for e in range(2):
            h = 2 * pair + e
            o_t = o2[:, e * BLOCK:(e + 1) * BLOCK]
            denom = (o_t[HEAD_DIM_A:HEAD_DIM_A + 1]
                     + jnp.exp2(sink_ref[h] * LOG2_E - ms[e]))
            outs.append(o_t[0:HEAD_DIM_A] / denom)
        o_ref[0, r0:r0 + BLOCK, pair * LANES:(pair + 1) * LANES] = (
            jnp.concatenate(outs, axis=0).T.astype(bf16))

    pending = [scores(*c) for c in chains[:SWA_LEAD]]
    for n, c in enumerate(chains):
        s2 = pending.pop(0)
        if n + SWA_LEAD < len(chains):
            pending.append(scores(*chains[n + SWA_LEAD]))
        finish(*c, s2)


def _mla_kernel(q_ref, k_ref, vt_ref, o_ref, s_ref, bmax_ref, acc_ref, m_ref):
    bf16 = jnp.bfloat16
    bq = MLA_BQ
    ch = MLA_CHUNK
    qi = pl.program_id(2)
    krow = lax.broadcasted_iota(jnp.int32, (bq, ch), 0)
    qcol = lax.broadcasted_iota(jnp.int32, (bq, ch), 1)
    neg_inf = jnp.float32(-jnp.inf)
    m_ref[...] = jnp.full(m_ref.shape, neg_inf, jnp.float32)
    acc_ref[...] = jnp.zeros(acc_ref.shape, jnp.float32)

    def scores_head(buf, kb, e):
        start = pl.multiple_of(kb * bq, bq)
        sl = slice(e * LANES, (e + 1) * LANES)
        s = _dot_nt(k_ref[0, pl.ds(start, bq), sl], q_ref[0, :, sl])
        s_ref[buf, e] = s
        bmax_ref[buf, e] = jnp.max(s, axis=0, keepdims=True)

    def scores_into(buf, kb):
        for e in range(MLA_HEADS):
            scores_head(buf, kb, e)

    def softmax_head(buf, e, masked):
        out = []
        for c in range(bq // ch):
            cs = slice(c * ch, (c + 1) * ch)
            s = s_ref[buf, e, :, cs]
            if masked:
                s = jnp.where(krow <= qcol + c * ch, s, neg_inf)
                blockmax = jnp.max(s, axis=0, keepdims=True)
            else:
                blockmax = bmax_ref[buf, e, :, cs]
            m_prev = m_ref[e, :, cs]
            m_new = jnp.maximum(m_prev, blockmax)
            m_ref[e, :, cs] = m_new
            out.append((jnp.exp2(s - m_new).astype(bf16), jnp.exp2(m_prev - m_new)))
        return out

    def pv_head(kb, e, weights):
        vt = vt_ref[0, kb, e * VT_ROWS:(e + 1) * VT_ROWS, :]
        for c, (p, alpha) in enumerate(weights):
            cs = slice(c * ch, (c + 1) * ch)
            acc_ref[e, :, cs] = alpha * acc_ref[e, :, cs] + _dot(vt, p)

    def softmax_pv(buf, kb, masked):
        for e in range(MLA_HEADS):
            pv_head(kb, e, softmax_head(buf, e, masked))

    def pipelined(first_kb, n_blocks):
        blocks = [(j % 2, first_kb + j) for j in range(n_blocks)]
        units = [(buf, kb, e) for buf, kb in blocks for e in range(MLA_HEADS)]
        prods = [(1 - buf, kb + 1, e) for buf, kb in blocks for e in range(MLA_HEADS)]
        for n in range(min(MLA_LEAD, len(prods))):
            scores_head(*prods[n])
        for n, (buf, kb, e) in enumerate(units):
            pv_head(kb, e, softmax_head(buf, e, False))
            if n + MLA_LEAD < len(prods):
                scores_head(*prods[n + MLA_LEAD])

    def quad_body(t, carry):
        pipelined(4 * t, 4)
        return carry

    scores_into(0, 0)
    lax.fori_loop(0, qi // 4, quad_body, 0)

    @pl.when(qi % 4 >= 2)
    def _():
        pipelined((qi // 4) * 4, 2)

    @pl.when(qi % 2 == 1)
    def _():
        pipelined(qi - 1, 1)
        softmax_pv(1, qi, True)

    @pl.when(qi % 2 == 0)
    def _():
        softmax_pv(0, qi, True)

    outs = []
    for e in range(MLA_HEADS):
        acc = acc_ref[e]
        outs.append(acc[0:V_DIM_B] / acc[V_DIM_B:V_DIM_B + 1])
    o_ref[0] = jnp.concatenate(outs, axis=0).T.astype(bf16)


def _merge_kernel(x_ref, oa_ref, ob_ref, gpre_ref, wg_ref, woa_ref, wob_ref, wout_ref, gpost_ref,
                  gpre2_ref, wup_ref, wdn_ref, gpost2_ref, o_ref):
    bf16 = jnp.bfloat16
    x = x_ref[...]
    hm = x.shape[0] // 2
    halves = (slice(0, hm), slice(hm, 2 * hm))
    hb = [(_rms(x[r]) * gpre_ref[...]).astype(bf16) for r in halves]
    gate_a = jnp.concatenate([_dot_nt(h, wg_ref[0:D_MODEL, :]) for h in hb], axis=0)
    hb = jnp.concatenate(hb, axis=0)
    gate_a = jax.nn.sigmoid(gate_a)
    gate_b = jax.nn.sigmoid(_dot_nt(hb, wg_ref[D_MODEL:2 * D_MODEL, :]))
    merged = gate_a * _dot(oa_ref[...], woa_ref[...]) + gate_b * _dot(ob_ref[...], wob_ref[...])
    mb = merged.astype(bf16)

    n_chunks = D_FF // FF_CHUNK
    chunk = lambda c: slice(c * FF_CHUNK, (c + 1) * FF_CHUNK)

    def sq_relu(v):
        v = jnp.maximum(v, 0.0)
        return (v * v).astype(bf16)

    x1, h2, up0 = [], [], []
    for r in halves:
        y = _dot(mb[r], wout_ref[...])
        x1.append(x[r] + _rms(y) * gpost_ref[...])
        h2.append((_rms(x1[-1]) * gpre2_ref[...]).astype(bf16))
    for i in range(2):
        up0.append(sq_relu(_dot(h2[i], wup_ref[:, chunk(0)])))
    h2 = jnp.concatenate(h2, axis=0)
    y2 = _dot(jnp.concatenate(up0, axis=0), wdn_ref[chunk(0), :])
    for c in range(1, n_chunks - 1):
        y2 = y2 + _dot(sq_relu(_dot(h2, wup_ref[:, chunk(c)])), wdn_ref[chunk(c), :])
    last = sq_relu(_dot(h2, wup_ref[:, chunk(n_chunks - 1)]))
    for i, r in enumerate(halves):
        y2_half = y2[r] + _dot(last[r], wdn_ref[chunk(n_chunks - 1), :])
        o_ref[r, :] = x1[i] + _rms(y2_half) * gpost2_ref[...]


def _const_spec(shape):
    return pl.BlockSpec(shape, lambda *_: (0,) * len(shape), pipeline_mode=pl.Buffered(1))


def kernel(x, positions, pre_norm_mix, w_in, q_a_norm, w_q_b, kv_a_norm, w_kv_b, sinks, w_o_a,
           w_o_b, w_out, post_norm_mix, pre_norm_mlp, w_up, w_down, post_norm_mlp):
    f32, bf16 = jnp.float32, jnp.bfloat16
    B, S, D = x.shape
    T = B * S
    depth = w_in.shape[0]
    for l in range(depth):
        assert w_in.shape[2] == D_IN
        wit = jnp.swapaxes(w_in[l], 0, 1).astype(bf16)
        ka_t = wit[IN_KA:IN_VA]
        va_t = wit[IN_VA:IN_CQ]
        kr_t = wit[IN_KR:D_IN]
        hr = QK_ROPE // 2
        hd = HEAD_DIM_A
        z = lambda n: jnp.zeros((n, D_MODEL), bf16)
        w_small_t = jnp.concatenate([
            ka_t[:hd], ka_t[:hd], ka_t[hd:], ka_t[hd:],
            z(QK_NOPE), kr_t, z(LANES - Q_HEAD_B),
            z(QK_NOPE), kr_t[hr:], kr_t[:hr], z(LANES - Q_HEAD_B)], axis=0)
        w_va_t = jnp.concatenate([va_t[:hd], z(VT_PAD), va_t[hd:], z(VT_PAD)], axis=0)

        wq = w_q_b[l].reshape(Q_LORA, N_HEADS_B, Q_HEAD_B)
        q_nope, q_rope = wq[..., :QK_NOPE], wq[..., QK_NOPE:]
        zq = lambda n: jnp.zeros((Q_LORA, N_HEADS_B, n), f32)
        wq_main = jnp.concatenate([q_nope, q_rope, zq(LANES - Q_HEAD_B)], -1)
        wq_swap = jnp.concatenate([zq(QK_NOPE), q_rope[..., hr:], q_rope[..., :hr],
                                   zq(LANES - Q_HEAD_B)], -1)
        wq_main = wq_main.reshape(Q_LORA, N_HEADS_B * LANES).astype(bf16)
        wq_swap = wq_swap.reshape(Q_LORA, N_HEADS_B * LANES).astype(bf16)

        wkv = w_kv_b[l].reshape(KV_LORA, N_HEADS_B, KV_HEAD_B)
        kv_k, kv_v = wkv[..., :QK_NOPE], wkv[..., QK_NOPE:]
        zk = jnp.zeros((KV_LORA, N_HEADS_B, HALF), f32)
        wkv_k = jnp.concatenate([kv_k, zk], -1).reshape(KV_LORA, N_HEADS_B * LANES).astype(bf16)
        wkv_vt = jnp.concatenate([kv_v, zk[..., :VT_PAD]], -1)
        wkv_vt = wkv_vt.reshape(KV_LORA, N_HEADS_B * VT_ROWS).T.astype(bf16)

        freq_col = (ROPE_THETA ** (-jnp.arange(0, QK_ROPE, 2, dtype=f32) / QK_ROPE))[:, None]

        row = lambda g: g.reshape(1, -1).astype(f32)
        x2 = x.reshape(T, D)

        tm = PROJ_TM
        assert PROJ_SUB == MLA_BQ and tm % PROJ_SUB == 0 and S % tm == 0
        nkb = S // MLA_BQ
        n_sub = tm // PROJ_SUB
        npb = S // tm
        tok = lambda w: pl.BlockSpec((tm, w), lambda i: (i, 0))
        w_rows = lambda n, start: pl.BlockSpec((n, D), lambda i: (start // n, 0),
                                               pipeline_mode=pl.Buffered(1))
        assert IN_QA % WIDTH_A == 0 and IN_CQ % Q_LORA == 0 and IN_CKV % KV_LORA == 0
        qa, ka_x, va_t3, qb, kb, vt = pl.pallas_call(
            _proj_kernel,
            grid=(T // tm,),
            in_specs=[tok(D), pl.BlockSpec((1, 1, tm), lambda i: (i, 0, 0)), _const_spec((1, D)),
                      w_rows(WIDTH_A, IN_QA), w_rows(Q_LORA, IN_CQ), w_rows(KV_LORA, IN_CKV),
                      _const_spec((SM_ROWS, D)), _const_spec((N_KV_A * VT_ROWS, D)),
                      _const_spec((1, Q_LORA)), _const_spec((Q_LORA, N_HEADS_B * LANES)),
                      _const_spec((Q_LORA, N_HEADS_B * LANES)), _const_spec((1, KV_LORA)),
                      _const_spec((KV_LORA, N_HEADS_B * LANES)),
                      _const_spec((N_HEADS_B * VT_ROWS, KV_LORA)),
                      _const_spec((QK_ROPE // 2, 1))],
            out_specs=[tok(WIDTH_A), tok(2 * LANES),
                       pl.BlockSpec((1, N_KV_A * VT_ROWS, tm), lambda i: (i // npb, 0, i % npb)),
                       tok(N_HEADS_B * LANES), tok(N_HEADS_B * LANES),
                       pl.BlockSpec((1, n_sub, N_HEADS_B * VT_ROWS, PROJ_SUB),
                                    lambda i: (i // npb, i % npb, 0, 0))],
            out_shape=[jax.ShapeDtypeStruct((T, WIDTH_A), bf16),
                       jax.ShapeDtypeStruct((T, 2 * LANES), bf16),
                       jax.ShapeDtypeStruct((B, N_KV_A * VT_ROWS, S), bf16),
                       jax.ShapeDtypeStruct((T, N_HEADS_B * LANES), bf16),
                       jax.ShapeDtypeStruct((T, N_HEADS_B * LANES), bf16),
                       jax.ShapeDtypeStruct((B, nkb, N_HEADS_B * VT_ROWS, PROJ_SUB), bf16)],
            compiler_params=pltpu.CompilerParams(dimension_semantics=("arbitrary",),
                                                 vmem_limit_bytes=VMEM_LIMIT),
            name="proj",
        )(x2, positions.reshape(T // tm, 1, tm), row(pre_norm_mix[l]), wit, wit, wit, w_small_t,
          w_va_t, row(q_a_norm[l]), wq_main, wq_swap, row(kv_a_norm[l]), wkv_k, wkv_vt,
          freq_col)

        tq = SWA_TQ
        nb_per = tq // BLOCK
        qa3 = qa.reshape(B, S, WIDTH_A)
        ka3 = ka_x.reshape(B, S, 2 * LANES)
        pos_r3 = positions.reshape(B, 1, S)
        cur = lambda w: pl.BlockSpec((1, tq, w), lambda b, i: (b, i, 0))
        prev_blk = lambda b, i: jnp.maximum(i * nb_per - 1, 0)
        out_a = pl.pallas_call(
            _swa_kernel,
            grid=(B, S // tq),
            in_specs=[pl.BlockSpec(memory_space=pltpu.SMEM),
                      cur(WIDTH_A), cur(2 * LANES),
                      pl.BlockSpec((1, BLOCK, 2 * LANES), lambda b, i: (b, prev_blk(b, i), 0)),
                      pl.BlockSpec((1, N_KV_A * VT_ROWS, tq), lambda b, i: (b, 0, i)),
                      pl.BlockSpec((1, N_KV_A * VT_ROWS, BLOCK), lambda b, i: (b, 0, prev_blk(b, i))),
                      pl.BlockSpec((1, 1, tq), lambda b, i: (b, 0, i)),
                      pl.BlockSpec((1, 1, BLOCK), lambda b, i: (b, 0, prev_blk(b, i)))],
            out_specs=cur(WIDTH_A),
            out_shape=jax.ShapeDtypeStruct((B, S, WIDTH_A), bf16),
            compiler_params=pltpu.CompilerParams(dimension_semantics=("arbitrary", "arbitrary"),
                                                 vmem_limit_bytes=VMEM_LIMIT),
            name="swa",
        )(sinks[l].astype(f32), qa3, ka3, ka3, va_t3, va_t3, pos_r3, pos_r3)

        bq, nh = MLA_BQ, MLA_HEADS
        qb3 = qb.reshape(B, S, N_HEADS_B * LANES)
        kb3 = kb.reshape(B, S, N_HEADS_B * LANES)
        out_b = pl.pallas_call(
            _mla_kernel,
            grid=(B, N_HEADS_B // nh, S // bq),
            in_specs=[pl.BlockSpec((1, bq, nh * LANES), lambda b, j, i: (b, i, j)),
                      pl.BlockSpec((1, S, nh * LANES), lambda b, j, i: (b, 0, j)),
                      pl.BlockSpec((1, nkb, nh * VT_ROWS, bq), lambda b, j, i: (b, 0, j, 0))],
            out_specs=pl.BlockSpec((1, bq, nh * V_DIM_B), lambda b, j, i: (b, i, j)),
            out_shape=jax.ShapeDtypeStruct((B, S, WIDTH_B), bf16),
            scratch_shapes=[pltpu.VMEM((2, nh, bq, bq), f32),
                            pltpu.VMEM((2, nh, 1, bq), f32),
                            pltpu.VMEM((nh, VT_ROWS, bq), f32), pltpu.VMEM((nh, 1, bq), f32)],
            compiler_params=pltpu.CompilerParams(
                dimension_semantics=("arbitrary", "arbitrary", "arbitrary"),
                vmem_limit_bytes=VMEM_LIMIT),
            name="mla",
        )(qb3, kb3, vt)

        tm = MERGE_TM
        tok = lambda w: pl.BlockSpec((tm, w), lambda i: (i, 0))
        x2 = pl.pallas_call(
            _merge_kernel,
            grid=(T // tm,),
            in_specs=[tok(D), tok(WIDTH_A), tok(WIDTH_B), _const_spec((1, D)),
                      w_rows(2 * D, IN_GATES), _const_spec((WIDTH_A, D)), _const_spec((WIDTH_B, D)),
                      _const_spec((D, D)), _const_spec((1, D)), _const_spec((1, D)),
                      _const_spec((D, D_FF)), _const_spec((D_FF, D)), _const_spec((1, D))],
            out_specs=tok(D),
            out_shape=jax.ShapeDtypeStruct((T, D), f32),
            compiler_params=pltpu.CompilerParams(dimension_semantics=("arbitrary",),
                                                 vmem_limit_bytes=VMEM_LIMIT),
            name="merge_mlp",
        )(x2, out_a.reshape(T, WIDTH_A), out_b.reshape(T, WIDTH_B), row(pre_norm_mix[l]), wit,
          w_o_a[l].astype(bf16), w_o_b[l].astype(bf16), w_out[l].astype(bf16),
          row(post_norm_mix[l]), row(pre_norm_mlp[l]), w_up[l].astype(bf16),
          w_down[l].astype(bf16), row(post_norm_mlp[l]))
        x = x2.reshape(B, S, D)
    return x
```

```python
import jax
import jax.numpy as jnp
from jax import lax
from jax.experimental import pallas as pl
from jax.experimental.pallas import tpu as pltpu

D_MODEL = 1024
N_HEADS_A = 8
N_KV_A = 2
HEAD_DIM_A = 64
WINDOW = 128
BLOCK = 128
N_HEADS_B = 8
QK_NOPE = 64
QK_ROPE = 32
V_DIM_B = 64
Q_LORA = 256
KV_LORA = 128
ROPE_THETA = 10000.0
D_FF = 4 * D_MODEL
EPS = 1e-6

WIDTH_A = N_HEADS_A * HEAD_DIM_A
WIDTH_B = N_HEADS_B * V_DIM_B
Q_HEAD_B = QK_NOPE + QK_ROPE
KV_HEAD_B = QK_NOPE + V_DIM_B

LANES = 128
HALF = LANES // 2
VT_ROWS = LANES
VT_PAD = VT_ROWS - V_DIM_B
SCALE_A = HEAD_DIM_A ** -0.5
SCALE_B = Q_HEAD_B ** -0.5
LOG2_E = 1.4426950408889634
QSCALE_A = SCALE_A * LOG2_E
QSCALE_B = SCALE_B * LOG2_E
ALIBI_SLOPES = tuple(2.0 ** (-8.0 * (h + 1) / N_HEADS_A) for h in range(N_HEADS_A))

IN_GATES = 0
IN_QA = 2 * D_MODEL
IN_KA = IN_QA + WIDTH_A
IN_VA = IN_KA + N_KV_A * HEAD_DIM_A
IN_CQ = IN_VA + N_KV_A * HEAD_DIM_A
IN_CKV = IN_CQ + Q_LORA
IN_KR = IN_CKV + KV_LORA
D_IN = IN_KR + QK_ROPE
SM_KA = 0
SM_KR = SM_KA + 2 * LANES
SM_KRS = SM_KR + LANES
SM_ROWS = SM_KRS + LANES

PROJ_TM = 1024
PROJ_SUB = 512
SWA_TQ = 2048
SWA_LEAD = 4
MLA_BQ = 512
MLA_CHUNK = 512
MLA_HEADS = 4
MLA_LEAD = 1
MERGE_TM = 512
FF_CHUNK = 1024
V7X_VMEM_BYTES = 64 * 1024 * 1024
VMEM_LIMIT = V7X_VMEM_BYTES - 4 * 1024 * 1024


def _rms(v):
    return v * lax.rsqrt(jnp.mean(v * v, axis=-1, keepdims=True) + EPS)


def _dot(a, b):
    return jnp.dot(a, b, preferred_element_type=jnp.float32)


def _dot_nt(a, b):
    return lax.dot_general(a, b, (((1,), (1,)), ((), ())), preferred_element_type=jnp.float32)


def _proj_kernel(x_ref, pos_ref, gpre_ref, wqa_ref, wcq_ref, wckv_ref, wsm_ref, wvat_ref, gq_ref,
                 wqm_ref, wqs_ref, gkv_ref, wkk_ref, wkvt_ref, freq_ref,
                 qa_ref, ka_ref, vat_ref, qb_ref, kb_ref, vt_ref):
    bf16 = jnp.bfloat16
    sub = PROJ_SUB

    def first_stage(n):
        rows = slice(n * sub, (n + 1) * sub)
        hb = (_rms(x_ref[rows, :]) * gpre_ref[...]).astype(bf16)
        qa_ref[rows, :] = (_dot_nt(hb, wqa_ref[...]) * QSCALE_A).astype(bf16)
        small = _dot_nt(hb, wsm_ref[...])
        ka_ref[rows, :] = small[:, SM_KA:SM_KR].astype(bf16)
        va_t = _dot_nt(wvat_ref[...], hb)
        varow = lax.broadcasted_iota(jnp.int32, va_t.shape, 0)
        vat_ref[0, :, rows] = jnp.where(varow % VT_ROWS == HEAD_DIM_A, 1.0, va_t).astype(bf16)
        cq = _dot_nt(hb, wcq_ref[...])
        ckv = _dot_nt(hb, wckv_ref[...])
        pos = pos_ref[0, :, rows].astype(jnp.float32)
        ang = freq_ref[...] * pos
        cos_t, sin_t = jnp.cos(ang), jnp.sin(ang)
        one = jnp.ones((QK_NOPE, sub), jnp.float32)
        zero = jnp.zeros((QK_NOPE, sub), jnp.float32)
        pad = LANES - Q_HEAD_B
        cos = jnp.concatenate([one, cos_t, cos_t, one[:pad]], axis=0).T
        sin = jnp.concatenate([zero, -sin_t, sin_t, zero[:pad]], axis=0).T
        k_rot = small[:, SM_KR:SM_KRS] * cos + small[:, SM_KRS:SM_ROWS] * sin
        return cq, ckv, cos, sin, k_rot

    def second_stage(n, cq, ckv, cos, sin, k_rot):
        rows = slice(n * sub, (n + 1) * sub)
        cqn = (_rms(cq) * gq_ref[...]).astype(bf16)
        q_main = _dot(cqn, wqm_ref[...])
        q_swap = _dot(cqn, wqs_ref[...])
        ckvn = (_rms(ckv) * gkv_ref[...]).astype(bf16)
        k_nope = _dot(ckvn, wkk_ref[...])
        v_t = _dot_nt(wkvt_ref[...], ckvn)
        vrow = lax.broadcasted_iota(jnp.int32, v_t.shape, 0)
        vt_ref[0, n] = jnp.where(vrow % VT_ROWS == V_DIM_B, 1.0, v_t).astype(bf16)
        for h in range(N_HEADS_B):
            sl = slice(h * LANES, (h + 1) * LANES)
            qb_ref[rows, sl] = ((q_main[:, sl] * cos + q_swap[:, sl] * sin) * QSCALE_B).astype(bf16)
            kb_ref[rows, sl] = (k_nope[:, sl] + k_rot).astype(bf16)

    n_sub = x_ref.shape[0] // sub
    staged = [first_stage(n) for n in range(n_sub)]
    for n in range(n_sub):
        second_stage(n, *staged[n])


def _swa_kernel(sink_ref, q_ref, kc_ref, kp_ref, vtc_ref, vtp_ref, posc_ref, posp_ref, o_ref):
    bf16 = jnp.bfloat16
    i = pl.program_id(1)
    krow = lax.broadcasted_iota(jnp.int32, (2 * BLOCK, BLOCK), 0)
    qcol = lax.broadcasted_iota(jnp.int32, (2 * BLOCK, BLOCK), 1)
    ahead = krow - qcol
    band = (ahead > 0) & (ahead <= WINDOW)
    lane = lax.broadcasted_iota(jnp.int32, (BLOCK, LANES), 1)
    low_half = lane < HALF
    pos_inf = jnp.float32(jnp.inf)

    n_pairs = N_HEADS_A // 2
    chains = [(blk, pair) for blk in range(SWA_TQ // BLOCK) for pair in range(n_pairs)]
    bands = {}

    def band_of(blk):
        if blk not in bands:
            r0 = blk * BLOCK
            if blk == 0:
                kband = jnp.concatenate([kp_ref[0], kc_ref[0, 0:BLOCK, :]], axis=0)
                vtband = jnp.concatenate([vtp_ref[0], vtc_ref[0, :, 0:BLOCK]], axis=1)
                kpos = jnp.concatenate([posp_ref[0], posc_ref[0, :, 0:BLOCK]], axis=1)
                mask = band & ((krow >= BLOCK) | (i > 0))
            else:
                kband = kc_ref[0, r0 - BLOCK:r0 + BLOCK, :]
                vtband = vtc_ref[0, :, r0 - BLOCK:r0 + BLOCK]
                kpos = posc_ref[0, :, r0 - BLOCK:r0 + BLOCK]
                mask = band
            qpos = posc_ref[0, :, r0:r0 + BLOCK]
            kpos_col = jnp.broadcast_to(kpos, (BLOCK, 2 * BLOCK)).T
            dist = jnp.abs(kpos_col - qpos).astype(jnp.float32) * LOG2_E
            dist = jnp.where(mask, dist, pos_inf)
            bands[blk] = (kband, vtband, dist)
        return bands[blk]

    def scores(blk, pair):
        kv = (2 * pair) // (N_HEADS_A // N_KV_A)
        r0 = blk * BLOCK
        qp = q_ref[0, r0:r0 + BLOCK, pair * LANES:(pair + 1) * LANES]
        zero = jnp.zeros_like(qp)
        q2 = jnp.concatenate([jnp.where(low_half, qp, zero), jnp.where(low_half, zero, qp)],
                             axis=0)
        kx = band_of(blk)[0][:, kv * LANES:(kv + 1) * LANES]
        return _dot_nt(kx, q2)

    def finish(blk, pair, s2):
        kv = (2 * pair) // (N_HEADS_A // N_KV_A)
        r0 = blk * BLOCK
        _, vtband, dist = band_of(blk)
        vt = vtband[kv * VT_ROWS:(kv + 1) * VT_ROWS, :]
        ps, ms = [], []
        for e in range(2):
            h = 2 * pair + e
            s = s2[:, e * BLOCK:(e + 1) * BLOCK] - ALIBI_SLOPES[h] * dist
            m = jnp.maximum(jnp.max(s, axis=0, keepdims=True), sink_ref[h] * LOG2_E)
            ps.append(jnp.exp2(s - m).astype(bf16))
            ms.append(m)
        o2 = _dot(vt, jnp.concatenate(ps, axis=1))
        outs = []
        for e in range(2):
            h = 2 * pair + e
            o_t = o2[:, e * BLOCK:(e + 1) * BLOCK]
            denom = (o_t[HEAD_DIM_A:HEAD_DIM_A + 1]
                     + jnp.exp2(sink_ref[h] * LOG2_E - ms[e]))
            outs.append(o_t[0:HEAD_DIM_A] / denom)
        o_ref[0, r0:r0 + BLOCK, pair * LANES:(pair + 1) * LANES] = (
            jnp.concatenate(outs, axis=0).T.astype(bf16))

    pending = [scores(*c) for c in chains[:SWA_LEAD]]
    for n, c in enumerate(chains):
        s2 = pending.pop(0)
        if n + SWA_LEAD < len(chains):
            pending.append(scores(*chains[n + SWA_LEAD]))
        finish(*c, s2)


def _mla_kernel(q_ref, k_ref, vt_ref, o_ref, s_ref, bmax_ref, acc_ref, m_ref):
    bf16 = jnp.bfloat16
    bq = MLA_BQ
    ch = MLA_CHUNK
    qi = pl.program_id(2)
    krow = lax.broadcasted_iota(jnp.int32, (bq, ch), 0)
    qcol = lax.broadcasted_iota(jnp.int32, (bq, ch), 1)
    neg_inf = jnp.float32(-jnp.inf)
    m_ref[...] = jnp.full(m_ref.shape, neg_inf, jnp.float32)
    acc_ref[...] = jnp.zeros(acc_ref.shape, jnp.float32)

    def scores_head(buf, kb, e):
        start = pl.multiple_of(kb * bq, bq)
        sl = slice(e * LANES, (e + 1) * LANES)
        s = _dot_nt(k_ref[0, pl.ds(start, bq), sl], q_ref[0, :, sl])
        s_ref[buf, e] = s
        bmax_ref[buf, e] = jnp.max(s, axis=0, keepdims=True)

    def scores_into(buf, kb):
        for e in range(MLA_HEADS):
            scores_head(buf, kb, e)

    def softmax_head(buf, e, masked):
        out = []
        for c in range(bq // ch):
            cs = slice(c * ch, (c + 1) * ch)
            s = s_ref[buf, e, :, cs]
            if masked:
                s = jnp.where(krow <= qcol + c * ch, s, neg_inf)
                blockmax = jnp.max(s, axis=0, keepdims=True)
            else:
                blockmax = bmax_ref[buf, e, :, cs]
            m_prev = m_ref[e, :, cs]
            m_new = jnp.maximum(m_prev, blockmax)
            m_ref[e, :, cs] = m_new
            out.append((jnp.exp2(s - m_new).astype(bf16), jnp.exp2(m_prev - m_new)))
        return out

    def pv_head(kb, e, weights):
        vt = vt_ref[0, kb, e * VT_ROWS:(e + 1) * VT_ROWS, :]
        for c, (p, alpha) in enumerate(weights):
            cs = slice(c * ch, (c + 1) * ch)
            acc_ref[e, :, cs] = alpha * acc_ref[e, :, cs] + _dot(vt, p)

    def softmax_pv(buf, kb, masked):
        for e in range(MLA_HEADS):
            pv_head(kb, e, softmax_head(buf, e, masked))

    def pipelined(first_kb, n_blocks):
        blocks = [(j % 2, first_kb + j) for j in range(n_blocks)]
        units = [(buf, kb, e) for buf, kb in blocks for e in range(MLA_HEADS)]
        prods = [(1 - buf, kb + 1, e) for buf, kb in blocks for e in range(MLA_HEADS)]
        for n in range(min(MLA_LEAD, len(prods))):
            scores_head(*prods[n])
        for n, (buf, kb, e) in enumerate(units):
            pv_head(kb, e, softmax_head(buf, e, False))
            if n + MLA_LEAD < len(prods):
                scores_head(*prods[n + MLA_LEAD])

    def octet_body(t, carry):
        pipelined(8 * t, 8)
        return carry

    scores_into(0, 0)
    lax.fori_loop(0, qi // 8, octet_body, 0)

    @pl.when(qi % 8 >= 4)
    def _():
        pipelined((qi // 8) * 8, 4)

    @pl.when(qi % 4 >= 2)
    def _():
        pipelined((qi // 4) * 4, 2)

    @pl.when(qi % 2 == 1)
    def _():
        pipelined(qi - 1, 1)
        softmax_pv(1, qi, True)

    @pl.when(qi % 2 == 0)
    def _():
        softmax_pv(0, qi, True)

    outs = []
    for e in range(MLA_HEADS):
        acc = acc_ref[e]
        outs.append(acc[0:V_DIM_B] / acc[V_DIM_B:V_DIM_B + 1])
    o_ref[0] = jnp.concatenate(outs, axis=0).T.astype(bf16)


def _merge_kernel(x_ref, oa_ref, ob_ref, gpre_ref, wg_ref, woa_ref, wob_ref, wout_ref, gpost_ref,
                  gpre2_ref, wup_ref, wdn_ref, gpost2_ref, o_ref):
    bf16 = jnp.bfloat16
    x = x_ref[...]
    hm = x.shape[0] // 2
    halves = (slice(0, hm), slice(hm, 2 * hm))
    hb = [(_rms(x[r]) * gpre_ref[...]).astype(bf16) for r in halves]
    gate_a = jnp.concatenate([_dot_nt(h, wg_ref[0:D_MODEL, :]) for h in hb], axis=0)
    hb = jnp.concatenate(hb, axis=0)
    gate_a = jax.nn.sigmoid(gate_a)
    gate_b = jax.nn.sigmoid(_dot_nt(hb, wg_ref[D_MODEL:2 * D_MODEL, :]))
    merged = gate_a * _dot(oa_ref[...], woa_ref[...]) + gate_b * _dot(ob_ref[...], wob_ref[...])
    mb = merged.astype(bf16)

    n_chunks = D_FF // FF_CHUNK
    chunk = lambda c: slice(c * FF_CHUNK, (c + 1) * FF_CHUNK)

    def sq_relu(v):
        v = jnp.maximum(v, 0.0)
        return (v * v).astype(bf16)

    x1, h2, up0 = [], [], []
    for r in halves:
        y = _dot(mb[r], wout_ref[...])
        x1.append(x[r] + _rms(y) * gpost_ref[...])
        h2.append((_rms(x1[-1]) * gpre2_ref[...]).astype(bf16))
    for i in range(2):
        up0.append(sq_relu(_dot(h2[i], wup_ref[:, chunk(0)])))
    h2 = jnp.concatenate(h2, axis=0)
    y2 = _dot(jnp.concatenate(up0, axis=0), wdn_ref[chunk(0), :])
    for c in range(1, n_chunks - 1):
        y2 = y2 + _dot(sq_relu(_dot(h2, wup_ref[:, chunk(c)])), wdn_ref[chunk(c), :])
    last = sq_relu(_dot(h2, wup_ref[:, chunk(n_chunks - 1)]))
    for i, r in enumerate(halves):
        y2_half = y2[r] + _dot(last[r], wdn_ref[chunk(n_chunks - 1), :])
        o_ref[r, :] = x1[i] + _rms(y2_half) * gpost2_ref[...]


def _const_spec(shape):
    return pl.BlockSpec(shape, lambda *_: (0,) * len(shape), pipeline_mode=pl.Buffered(1))


def kernel(x, positions, pre_norm_mix, w_in, q_a_norm, w_q_b, kv_a_norm, w_kv_b, sinks, w_o_a,
           w_o_b, w_out, post_norm_mix, pre_norm_mlp, w_up, w_down, post_norm_mlp):
    f32, bf16 = jnp.float32, jnp.bfloat16
    B, S, D = x.shape
    T = B * S
    depth = w_in.shape[0]
    for l in range(depth):
        assert w_in.shape[2] == D_IN
        wit = jnp.swapaxes(w_in[l], 0, 1).astype(bf16)
        ka_t = wit[IN_KA:IN_VA]
        va_t = wit[IN_VA:IN_CQ]
        kr_t = wit[IN_KR:D_IN]
        hr = QK_ROPE // 2
        hd = HEAD_DIM_A
        z = lambda n: jnp.zeros((n, D_MODEL), bf16)
        w_small_t = jnp.concatenate([
            ka_t[:hd], ka_t[:hd], ka_t[hd:], ka_t[hd:],
            z(QK_NOPE), kr_t, z(LANES - Q_HEAD_B),
            z(QK_NOPE), kr_t[hr:], kr_t[:hr], z(LANES - Q_HEAD_B)], axis=0)
        w_va_t = jnp.concatenate([va_t[:hd], z(VT_PAD), va_t[hd:], z(VT_PAD)], axis=0)

        wq = w_q_b[l].reshape(Q_LORA, N_HEADS_B, Q_HEAD_B)
        q_nope, q_rope = wq[..., :QK_NOPE], wq[..., QK_NOPE:]
        zq = lambda n: jnp.zeros((Q_LORA, N_HEADS_B, n), f32)
        wq_main = jnp.concatenate([q_nope, q_rope, zq(LANES - Q_HEAD_B)], -1)
        wq_swap = jnp.concatenate([zq(QK_NOPE), q_rope[..., hr:], q_rope[..., :hr],
                                   zq(LANES - Q_HEAD_B)], -1)
        wq_main = wq_main.reshape(Q_LORA, N_HEADS_B * LANES).astype(bf16)
        wq_swap = wq_swap.reshape(Q_LORA, N_HEADS_B * LANES).astype(bf16)

        wkv = w_kv_b[l].reshape(KV_LORA, N_HEADS_B, KV_HEAD_B)
        kv_k, kv_v = wkv[..., :QK_NOPE], wkv[..., QK_NOPE:]
        zk = jnp.zeros((KV_LORA, N_HEADS_B, HALF), f32)
        wkv_k = jnp.concatenate([kv_k, zk], -1).reshape(KV_LORA, N_HEADS_B * LANES).astype(bf16)
        wkv_vt = jnp.concatenate([kv_v, zk[..., :VT_PAD]], -1)
        wkv_vt = wkv_vt.reshape(KV_LORA, N_HEADS_B * VT_ROWS).T.astype(bf16)

        freq_col = (ROPE_THETA ** (-jnp.arange(0, QK_ROPE, 2, dtype=f32) / QK_ROPE))[:, None]

        row = lambda g: g.reshape(1, -1).astype(f32)
        x2 = x.reshape(T, D)

        tm = PROJ_TM
        assert PROJ_SUB == MLA_BQ and tm % PROJ_SUB == 0 and S % tm == 0
        nkb = S // MLA_BQ
        n_sub = tm // PROJ_SUB
        npb = S // tm
        tok = lambda w: pl.BlockSpec((tm, w), lambda i: (i, 0))
        w_rows = lambda n, start: pl.BlockSpec((n, D), lambda i: (start // n, 0),
                                               pipeline_mode=pl.Buffered(1))
        assert IN_QA % WIDTH_A == 0 and IN_CQ % Q_LORA == 0 and IN_CKV % KV_LORA == 0
        qa, ka_x, va_t3, qb, kb, vt = pl.pallas_call(
            _proj_kernel,
            grid=(T // tm,),
            in_specs=[tok(D), pl.BlockSpec((1, 1, tm), lambda i: (i, 0, 0)), _const_spec((1, D)),
                      w_rows(WIDTH_A, IN_QA), w_rows(Q_LORA, IN_CQ), w_rows(KV_LORA, IN_CKV),
                      _const_spec((SM_ROWS, D)), _const_spec((N_KV_A * VT_ROWS, D)),
                      _const_spec((1, Q_LORA)), _const_spec((Q_LORA, N_HEADS_B * LANES)),
                      _const_spec((Q_LORA, N_HEADS_B * LANES)), _const_spec((1, KV_LORA)),
                      _const_spec((KV_LORA, N_HEADS_B * LANES)),
                      _const_spec((N_HEADS_B * VT_ROWS, KV_LORA)),
                      _const_spec((QK_ROPE // 2, 1))],
            out_specs=[tok(WIDTH_A), tok(2 * LANES),
                       pl.BlockSpec((1, N_KV_A * VT_ROWS, tm), lambda i: (i // npb, 0, i % npb)),
                       tok(N_HEADS_B * LANES), tok(N_HEADS_B * LANES),
                       pl.BlockSpec((1, n_sub, N_HEADS_B * VT_ROWS, PROJ_SUB),
                                    lambda i: (i // npb, i % npb, 0, 0))],
            out_shape=[jax.ShapeDtypeStruct((T, WIDTH_A), bf16),
                       jax.ShapeDtypeStruct((T, 2 * LANES), bf16),
                       jax.ShapeDtypeStruct((B, N_KV_A * VT_ROWS, S), bf16),
                       jax.ShapeDtypeStruct((T, N_HEADS_B * LANES), bf16),
                       jax.ShapeDtypeStruct((T, N_HEADS_B * LANES), bf16),
                       jax.ShapeDtypeStruct((B, nkb, N_HEADS_B * VT_ROWS, PROJ_SUB), bf16)],
            compiler_params=pltpu.CompilerParams(dimension_semantics=("arbitrary",),
                                                 vmem_limit_bytes=VMEM_LIMIT),
            name="proj",
        )(x2, positions.reshape(T // tm, 1, tm), row(pre_norm_mix[l]), wit, wit, wit, w_small_t,
          w_va_t, row(q_a_norm[l]), wq_main, wq_swap, row(kv_a_norm[l]), wkv_k, wkv_vt,
          freq_col)

        tq = SWA_TQ
        nb_per = tq // BLOCK
        qa3 = qa.reshape(B, S, WIDTH_A)
        ka3 = ka_x.reshape(B, S, 2 * LANES)
        pos_r3 = positions.reshape(B, 1, S)
        cur = lambda w: pl.BlockSpec((1, tq, w), lambda b, i: (b, i, 0))
        prev_blk = lambda b, i: jnp.maximum(i * nb_per - 1, 0)
        out_a = pl.pallas_call(
            _swa_kernel,
            grid=(B, S // tq),
            in_specs=[pl.BlockSpec(memory_space=pltpu.SMEM),
                      cur(WIDTH_A), cur(2 * LANES),
                      pl.BlockSpec((1, BLOCK, 2 * LANES), lambda b, i: (b, prev_blk(b, i), 0)),
                      pl.BlockSpec((1, N_KV_A * VT_ROWS, tq), lambda b, i: (b, 0, i)),
                      pl.BlockSpec((1, N_KV_A * VT_ROWS, BLOCK), lambda b, i: (b, 0, prev_blk(b, i))),
                      pl.BlockSpec((1, 1, tq), lambda b, i: (b, 0, i)),
                      pl.BlockSpec((1, 1, BLOCK), lambda b, i: (b, 0, prev_blk(b, i)))],
            out_specs=cur(WIDTH_A),
            out_shape=jax.ShapeDtypeStruct((B, S, WIDTH_A), bf16),
            compiler_params=pltpu.CompilerParams(dimension_semantics=("arbitrary", "arbitrary"),
                                                 vmem_limit_bytes=VMEM_LIMIT),
            name="swa",
        )(sinks[l].astype(f32), qa3, ka3, ka3, va_t3, va_t3, pos_r3, pos_r3)

        bq, nh = MLA_BQ, MLA_HEADS
        qb3 = qb.reshape(B, S, N_HEADS_B * LANES)
        kb3 = kb.reshape(B, S, N_HEADS_B * LANES)
        out_b = pl.pallas_call(
            _mla_kernel,
            grid=(B, N_HEADS_B // nh, S // bq),
            in_specs=[pl.BlockSpec((1, bq, nh * LANES), lambda b, j, i: (b, i, j)),
                      pl.BlockSpec((1, S, nh * LANES), lambda b, j, i: (b, 0, j)),
                      pl.BlockSpec((1, nkb, nh * VT_ROWS, bq), lambda b, j, i: (b, 0, j, 0))],
            out_specs=pl.BlockSpec((1, bq, nh * V_DIM_B), lambda b, j, i: (b, i, j)),
            out_shape=jax.ShapeDtypeStruct((B, S, WIDTH_B), bf16),
            scratch_shapes=[pltpu.VMEM((2, nh, bq, bq), f32),
                            pltpu.VMEM((2, nh, 1, bq), f32),
                            pltpu.VMEM((nh, VT_ROWS, bq), f32), pltpu.VMEM((nh, 1, bq), f32)],
            compiler_params=pltpu.CompilerParams(
                dimension_semantics=("arbitrary", "arbitrary", "arbitrary"),
                vmem_limit_bytes=VMEM_LIMIT),
            name="mla",
        )(qb3, kb3, vt)

        tm = MERGE_TM
        tok = lambda w: pl.BlockSpec((tm, w), lambda i: (i, 0))
        x2 = pl.pallas_call(
            _merge_kernel,
            grid=(T // tm,),
            in_specs=[tok(D), tok(WIDTH_A), tok(WIDTH_B), _const_spec((1, D)),
                      w_rows(2 * D, IN_GATES), _const_spec((WIDTH_A, D)), _const_spec((WIDTH_B, D)),
                      _const_spec((D, D)), _const_spec((1, D)), _const_spec((1, D)),
                      _const_spec((D, D_FF)), _const_spec((D_FF, D)), _const_spec((1, D))],
            out_specs=tok(D),
            out_shape=jax.ShapeDtypeStruct((T, D), f32),
            compiler_params=pltpu.CompilerParams(dimension_semantics=("arbitrary",),
                                                 vmem_limit_bytes=VMEM_LIMIT),
            name="merge_mlp",
        )(x2, out_a.reshape(T, WIDTH_A), out_b.reshape(T, WIDTH_B), row(pre_norm_mix[l]), wit,
          w_o_a[l].astype(bf16), w_o_b[l].astype(bf16), w_out[l].astype(bf16),
          row(post_norm_mix[l]), row(pre_norm_mlp[l]), w_up[l].astype(bf16),
          w_down[l].astype(bf16), row(post_norm_mlp[l]))
        x = x2.reshape(B, S, D)
    return x
```

```python
import jax
import jax.numpy as jnp
from jax import lax
from jax.experimental import pallas as pl
from jax.experimental.pallas import tpu as pltpu

D_MODEL = 1024
N_HEADS_A = 8
N_KV_A = 2
HEAD_DIM_A = 64
WINDOW = 128
BLOCK = 128
N_HEADS_B = 8
QK_NOPE = 64
QK_ROPE = 32
V_DIM_B = 64
Q_LORA = 256
KV_LORA = 128
ROPE_THETA = 10000.0
D_FF = 4 * D_MODEL
EPS = 1e-6

WIDTH_A = N_HEADS_A * HEAD_DIM_A
WIDTH_B = N_HEADS_B * V_DIM_B
Q_HEAD_B = QK_NOPE + QK_ROPE
KV_HEAD_B = QK_NOPE + V_DIM_B

LANES = 128
HALF = LANES // 2
VT_ROWS = LANES
VT_PAD = VT_ROWS - V_DIM_B
SCALE_A = HEAD_DIM_A ** -0.5
SCALE_B = Q_HEAD_B ** -0.5
LOG2_E = 1.4426950408889634
QSCALE_A = SCALE_A * LOG2_E
QSCALE_B = SCALE_B * LOG2_E
ALIBI_SLOPES = tuple(2.0 ** (-8.0 * (h + 1) / N_HEADS_A) for h in range(N_HEADS_A))

IN_GATES = 0
IN_QA = 2 * D_MODEL
IN_KA = IN_QA + WIDTH_A
IN_VA = IN_KA + N_KV_A * HEAD_DIM_A
IN_CQ = IN_VA + N_KV_A * HEAD_DIM_A
IN_CKV = IN_CQ + Q_LORA
IN_KR = IN_CKV + KV_LORA
D_IN = IN_KR + QK_ROPE
SM_KA = 0
SM_KR = SM_KA + 2 * LANES
SM_KRS = SM_KR + LANES
SM_ROWS = SM_KRS + LANES

PROJ_TM = 1024
PROJ_SUB = 512
SWA_TQ = 2048
SWA_LEAD = 4
MLA_BQ = 512
MLA_CHUNK = 512
MLA_HEADS = 4
MLA_LEAD = 1
MERGE_TM = 512
FF_CHUNK = 1024
V7X_VMEM_BYTES = 64 * 1024 * 1024
VMEM_LIMIT = V7X_VMEM_BYTES - 4 * 1024 * 1024


def _rms(v):
    return v * lax.rsqrt(jnp.mean(v * v, axis=-1, keepdims=True) + EPS)


def _dot(a, b):
    return jnp.dot(a, b, preferred_element_type=jnp.float32)


def _dot_nt(a, b):
    return lax.dot_general(a, b, (((1,), (1,)), ((), ())), preferred_element_type=jnp.float32)


def _proj_kernel(x_ref, pos_ref, gpre_ref, wqa_ref, wcq_ref, wckv_ref, wsm_ref, wvat_ref, gq_ref,
                 wqm_ref, wqs_ref, gkv_ref, wkk_ref, wkvt_ref, freq_ref,
                 qa_ref, ka_ref, vat_ref, qb_ref, kb_ref, vt_ref):
    bf16 = jnp.bfloat16
    sub = PROJ_SUB

    def first_stage(n):
        rows = slice(n * sub, (n + 1) * sub)
        hb = (_rms(x_ref[rows, :]) * gpre_ref[...]).astype(bf16)
        qa_ref[rows, :] = (_dot_nt(hb, wqa_ref[...]) * QSCALE_A).astype(bf16)
        small = _dot_nt(hb, wsm_ref[...])
        ka_ref[rows, :] = small[:, SM_KA:SM_KR].astype(bf16)
        va_t = _dot_nt(wvat_ref[...], hb)
        varow = lax.broadcasted_iota(jnp.int32, va_t.shape, 0)
        vat_ref[0, :, rows] = jnp.where(varow % VT_ROWS == HEAD_DIM_A, 1.0, va_t).astype(bf16)
        cq = _dot_nt(hb, wcq_ref[...])
        ckv = _dot_nt(hb, wckv_ref[...])
        pos = pos_ref[0, :, rows].astype(jnp.float32)
        ang = freq_ref[...] * pos
        cos_t, sin_t = jnp.cos(ang), jnp.sin(ang)
        one = jnp.ones((QK_NOPE, sub), jnp.float32)
        zero = jnp.zeros((QK_NOPE, sub), jnp.float32)
        pad = LANES - Q_HEAD_B
        cos = jnp.concatenate([one, cos_t, cos_t, one[:pad]], axis=0).T
        sin = jnp.concatenate([zero, -sin_t, sin_t, zero[:pad]], axis=0).T
        k_rot = small[:, SM_KR:SM_KRS] * cos + small[:, SM_KRS:SM_ROWS] * sin
        return cq, ckv, cos, sin, k_rot

    def second_stage(n, cq, ckv, cos, sin, k_rot):
        rows = slice(n * sub, (n + 1) * sub)
        cqn = (_rms(cq) * gq_ref[...]).astype(bf16)
        q_main = _dot(cqn, wqm_ref[...])
        q_swap = _dot(cqn, wqs_ref[...])
        ckvn = (_rms(ckv) * gkv_ref[...]).astype(bf16)
        k_nope = _dot(ckvn, wkk_ref[...])
        vt_ref[0, n] = _dot_nt(wkvt_ref[...], ckvn).astype(bf16)
        for h in range(N_HEADS_B):
            sl = slice(h * LANES, (h + 1) * LANES)
            qb_ref[rows, sl] = ((q_main[:, sl] * cos + q_swap[:, sl] * sin) * QSCALE_B).astype(bf16)
            kb_ref[rows, sl] = (k_nope[:, sl] + k_rot).astype(bf16)

    n_sub = x_ref.shape[0] // sub
    staged = [first_stage(n) for n in range(n_sub)]
    for n in range(n_sub):
        second_stage(n, *staged[n])


def _swa_kernel(sink_ref, q_ref, kc_ref, kp_ref, vtc_ref, vtp_ref, posc_ref, posp_ref, o_ref):
    bf16 = jnp.bfloat16
    i = pl.program_id(1)
    krow = lax.broadcasted_iota(jnp.int32, (2 * BLOCK, BLOCK), 0)
    qcol = lax.broadcasted_iota(jnp.int32, (2 * BLOCK, BLOCK), 1)
    ahead = krow - qcol
    band = (ahead > 0) & (ahead <= WINDOW)
    lane = lax.broadcasted_iota(jnp.int32, (BLOCK, LANES), 1)
    low_half = lane < HALF
    pos_inf = jnp.float32(jnp.inf)

    n_pairs = N_HEADS_A // 2
    chains = [(blk, pair) for blk in range(SWA_TQ // BLOCK) for pair in range(n_pairs)]
    bands = {}

    def band_of(blk):
        if blk not in bands:
            r0 = blk * BLOCK
            if blk == 0:
                kband = jnp.concatenate([kp_ref[0], kc_ref[0, 0:BLOCK, :]], axis=0)
                vtband = jnp.concatenate([vtp_ref[0], vtc_ref[0, :, 0:BLOCK]], axis=1)
                kpos = jnp.concatenate([posp_ref[0], posc_ref[0, :, 0:BLOCK]], axis=1)
                mask = band & ((krow >= BLOCK) | (i > 0))
            else:
                kband = kc_ref[0, r0 - BLOCK:r0 + BLOCK, :]
                vtband = vtc_ref[0, :, r0 - BLOCK:r0 + BLOCK]
                kpos = posc_ref[0, :, r0 - BLOCK:r0 + BLOCK]
                mask = band
            qpos = posc_ref[0, :, r0:r0 + BLOCK]
            kpos_col = jnp.broadcast_to(kpos, (BLOCK, 2 * BLOCK)).T
            dist = jnp.abs(kpos_col - qpos).astype(jnp.float32) * LOG2_E
            dist = jnp.where(mask, dist, pos_inf)
            bands[blk] = (kband, vtband, dist)
        return bands[blk]

    def scores(blk, pair):
        kv = (2 * pair) // (N_HEADS_A // N_KV_A)
        r0 = blk * BLOCK
        qp = q_ref[0, r0:r0 + BLOCK, pair * LANES:(pair + 1) * LANES]
        zero = jnp.zeros_like(qp)
        q2 = jnp.concatenate([jnp.where(low_half, qp, zero), jnp.where(low_half, zero, qp)],
                             axis=0)
        kx = band_of(blk)[0][:, kv * LANES:(kv + 1) * LANES]
        return _dot_nt(kx, q2)

    def finish(blk, pair, s2):
        kv = (2 * pair) // (N_HEADS_A // N_KV_A)
        r0 = blk * BLOCK
        _, vtband, dist = band_of(blk)
        vt = vtband[kv * VT_ROWS:(kv + 1) * VT_ROWS, :]
        ps, ms = [], []
        for e in range(2):
            h = 2 * pair + e
            s = s2[:, e * BLOCK:(e + 1) * BLOCK] - ALIBI_SLOPES[h] * dist
            m = jnp.maximum(jnp.max(s, axis=0, keepdims=True), sink_ref[h] * LOG2_E)
            ps.append(jnp.exp2(s - m).astype(bf16))
            ms.append(m)
        o2 = _dot(vt, jnp.concatenate(ps, axis=1))
        outs = []
        for e in range(2):
            h = 2 * pair + e
            o_t = o2[:, e * BLOCK:(e + 1) * BLOCK]
            denom = (o_t[HEAD_DIM_A:HEAD_DIM_A + 1]
                     + jnp.exp2(sink_ref[h] * LOG2_E - ms[e]))
            outs.append(o_t[0:HEAD_DIM_A] / denom)
        o_ref[0, r0:r0 + BLOCK, pair * LANES:(pair + 1) * LANES] = (
            jnp.concatenate(outs, axis=0).T.astype(bf16))

    pending = [scores(*c) for c in chains[:SWA_LEAD]]
    for n, c in enumerate(chains):
        s2 = pending.pop(0)
        if n + SWA_LEAD < len(chains):
            pending.append(scores(*chains[n + SWA_LEAD]))
        finish(*c, s2)


def _mla_kernel(q_ref, k_ref, vt_ref, o_ref, s_ref, bmax_ref, acc_ref, m_ref, l_ref):
    bf16 = jnp.bfloat16
    bq = MLA_BQ
    ch = MLA_CHUNK
    qi = pl.program_id(2)
    krow = lax.broadcasted_iota(jnp.int32, (bq, ch), 0)
    qcol = lax.broadcasted_iota(jnp.int32, (bq, ch), 1)
    neg_inf = jnp.float32(-jnp.inf)
    m_ref[...] = jnp.full(m_ref.shape, neg_inf, jnp.float32)
    acc_ref[...] = jnp.zeros(acc_ref.shape, jnp.float32)
    l_ref[...] = jnp.zeros(l_ref.shape, jnp.float32)

    def scores_head(buf, kb, e):
        start = pl.multiple_of(kb * bq, bq)
        sl = slice(e * LANES, (e + 1) * LANES)
        s = _dot_nt(k_ref[0, pl.ds(start, bq), sl], q_ref[0, :, sl])
        s_ref[buf, e] = s
        bmax_ref[buf, e] = jnp.max(s, axis=0, keepdims=True)

    def scores_into(buf, kb):
        for e in range(MLA_HEADS):
            scores_head(buf, kb, e)

    def softmax_head(buf, e, masked):
        out = []
        for c in range(bq // ch):
            cs = slice(c * ch, (c + 1) * ch)
            s = s_ref[buf, e, :, cs]
            if masked:
                s = jnp.where(krow <= qcol + c * ch, s, neg_inf)
                blockmax = jnp.max(s, axis=0, keepdims=True)
            else:
                blockmax = bmax_ref[buf, e, :, cs]
            m_prev = m_ref[e, :, cs]
            m_new = jnp.maximum(m_prev, blockmax)
            m_ref[e, :, cs] = m_new
            p = jnp.exp2(s - m_new)
            out.append((p.astype(bf16), jnp.exp2(m_prev - m_new), jnp.sum(p, axis=0, keepdims=True)))
        return out

    def pv_head(kb, e, weights):
        vt = vt_ref[0, kb, e * V_DIM_B:(e + 1) * V_DIM_B, :]
        for c, (p, alpha, psum) in enumerate(weights):
            cs = slice(c * ch, (c + 1) * ch)
            acc_ref[e, :, cs] = alpha * acc_ref[e, :, cs] + _dot(vt, p)
            l_ref[e, :, cs] = alpha * l_ref[e, :, cs] + psum

    def softmax_pv(buf, kb, masked):
        for e in range(MLA_HEADS):
            pv_head(kb, e, softmax_head(buf, e, masked))

    def pipelined(first_kb, n_blocks):
        blocks = [(j % 2, first_kb + j) for j in range(n_blocks)]
        units = [(buf, kb, e) for buf, kb in blocks for e in range(MLA_HEADS)]
        prods = [(1 - buf, kb + 1, e) for buf, kb in blocks for e in range(MLA_HEADS)]
        for n in range(min(MLA_LEAD, len(prods))):
            scores_head(*prods[n])
        for n, (buf, kb, e) in enumerate(units):
            pv_head(kb, e, softmax_head(buf, e, False))
            if n + MLA_LEAD < len(prods):
                scores_head(*prods[n + MLA_LEAD])

    def quad_body(t, carry):
        pipelined(4 * t, 4)
        return carry

    scores_into(0, 0)
    lax.fori_loop(0, qi // 4, quad_body, 0)

    @pl.when(qi % 4 >= 2)
    def _():
        pipelined((qi // 4) * 4, 2)

    @pl.when(qi % 2 == 1)
    def _():
        pipelined(qi - 1, 1)
        softmax_pv(1, qi, True)

    @pl.when(qi % 2 == 0)
    def _():
        softmax_pv(0, qi, True)

    outs = []
    for e in range(MLA_HEADS):
        outs.append(acc_ref[e] / l_ref[e])
    o_ref[0] = jnp.concatenate(outs, axis=0).T.astype(bf16)


def _merge_kernel(x_ref, oa_ref, ob_ref, gpre_ref, wg_ref, woa_ref, wob_ref, wout_ref, gpost_ref,
                  gpre2_ref, wup_ref, wdn_ref, gpost2_ref, o_ref):
    bf16 = jnp.bfloat16
    x = x_ref[...]
    hm = x.shape[0] // 2
    halves = (slice(0, hm), slice(hm, 2 * hm))
    hb = [(_rms(x[r]) * gpre_ref[...]).astype(bf16) for r in halves]
    gate_a = jnp.concatenate([_dot_nt(h, wg_ref[0:D_MODEL, :]) for h in hb], axis=0)
    hb = jnp.concatenate(hb, axis=0)
    gate_a = jax.nn.sigmoid(gate_a)
    gate_b = jax.nn.sigmoid(_dot_nt(hb, wg_ref[D_MODEL:2 * D_MODEL, :]))
    merged = gate_a * _dot(oa_ref[...], woa_ref[...]) + gate_b * _dot(ob_ref[...], wob_ref[...])
    mb = merged.astype(bf16)

    n_chunks = D_FF // FF_CHUNK
    chunk = lambda c: slice(c * FF_CHUNK, (c + 1) * FF_CHUNK)

    def sq_relu(v):
        v = jnp.maximum(v, 0.0)
        return (v * v).astype(bf16)

    x1, h2, up0 = [], [], []
    for r in halves:
        y = _dot(mb[r], wout_ref[...])
        x1.append(x[r] + _rms(y) * gpost_ref[...])
        h2.append((_rms(x1[-1]) * gpre2_ref[...]).astype(bf16))
    for i in range(2):
        up0.append(sq_relu(_dot(h2[i], wup_ref[:, chunk(0)])))
    h2 = jnp.concatenate(h2, axis=0)
    y2 = _dot(jnp.concatenate(up0, axis=0), wdn_ref[chunk(0), :])
    for c in range(1, n_chunks - 1):
        y2 = y2 + _dot(sq_relu(_dot(h2, wup_ref[:, chunk(c)])), wdn_ref[chunk(c), :])
    last = sq_relu(_dot(h2, wup_ref[:, chunk(n_chunks - 1)]))
    for i, r in enumerate(halves):
        y2_half = y2[r] + _dot(last[r], wdn_ref[chunk(n_chunks - 1), :])
        o_ref[r, :] = x1[i] + _rms(y2_half) * gpost2_ref[...]


def _const_spec(shape):
    return pl.BlockSpec(shape, lambda *_: (0,) * len(shape), pipeline_mode=pl.Buffered(1))


def kernel(x, positions, pre_norm_mix, w_in, q_a_norm, w_q_b, kv_a_norm, w_kv_b, sinks, w_o_a,
           w_o_b, w_out, post_norm_mix, pre_norm_mlp, w_up, w_down, post_norm_mlp):
    f32, bf16 = jnp.float32, jnp.bfloat16
    B, S, D = x.shape
    T = B * S
    depth = w_in.shape[0]
    for l in range(depth):
        assert w_in.shape[2] == D_IN
        wit = jnp.swapaxes(w_in[l], 0, 1).astype(bf16)
        ka_t = wit[IN_KA:IN_VA]
        va_t = wit[IN_VA:IN_CQ]
        kr_t = wit[IN_KR:D_IN]
        hr = QK_ROPE // 2
        hd = HEAD_DIM_A
        z = lambda n: jnp.zeros((n, D_MODEL), bf16)
        w_small_t = jnp.concatenate([
            ka_t[:hd], ka_t[:hd], ka_t[hd:], ka_t[hd:],
            z(QK_NOPE), kr_t, z(LANES - Q_HEAD_B),
            z(QK_NOPE), kr_t[hr:], kr_t[:hr], z(LANES - Q_HEAD_B)], axis=0)
        w_va_t = jnp.concatenate([va_t[:hd], z(VT_PAD), va_t[hd:], z(VT_PAD)], axis=0)

        wq = w_q_b[l].reshape(Q_LORA, N_HEADS_B, Q_HEAD_B)
        q_nope, q_rope = wq[..., :QK_NOPE], wq[..., QK_NOPE:]
        zq = lambda n: jnp.zeros((Q_LORA, N_HEADS_B, n), f32)
        wq_main = jnp.concatenate([q_nope, q_rope, zq(LANES - Q_HEAD_B)], -1)
        wq_swap = jnp.concatenate([zq(QK_NOPE), q_rope[..., hr:], q_rope[..., :hr],
                                   zq(LANES - Q_HEAD_B)], -1)
        wq_main = wq_main.reshape(Q_LORA, N_HEADS_B * LANES).astype(bf16)
        wq_swap = wq_swap.reshape(Q_LORA, N_HEADS_B * LANES).astype(bf16)

        wkv = w_kv_b[l].reshape(KV_LORA, N_HEADS_B, KV_HEAD_B)
        kv_k, kv_v = wkv[..., :QK_NOPE], wkv[..., QK_NOPE:]
        zk = jnp.zeros((KV_LORA, N_HEADS_B, HALF), f32)
        wkv_k = jnp.concatenate([kv_k, zk], -1).reshape(KV_LORA, N_HEADS_B * LANES).astype(bf16)
        wkv_vt = kv_v.reshape(KV_LORA, WIDTH_B).T.astype(bf16)

        freq_col = (ROPE_THETA ** (-jnp.arange(0, QK_ROPE, 2, dtype=f32) / QK_ROPE))[:, None]

        row = lambda g: g.reshape(1, -1).astype(f32)
        x2 = x.reshape(T, D)

        tm = PROJ_TM
        assert PROJ_SUB == MLA_BQ and tm % PROJ_SUB == 0 and S % tm == 0
        nkb = S // MLA_BQ
        n_sub = tm // PROJ_SUB
        npb = S // tm
        tok = lambda w: pl.BlockSpec((tm, w), lambda i: (i, 0))
        w_rows = lambda n, start: pl.BlockSpec((n, D), lambda i: (start // n, 0),
                                               pipeline_mode=pl.Buffered(1))
        assert IN_QA % WIDTH_A == 0 and IN_CQ % Q_LORA == 0 and IN_CKV % KV_LORA == 0
        qa, ka_x, va_t3, qb, kb, vt = pl.pallas_call(
            _proj_kernel,
            grid=(T // tm,),
            in_specs=[tok(D), pl.BlockSpec((1, 1, tm), lambda i: (i, 0, 0)), _const_spec((1, D)),
                      w_rows(WIDTH_A, IN_QA), w_rows(Q_LORA, IN_CQ), w_rows(KV_LORA, IN_CKV),
                      _const_spec((SM_ROWS, D)), _const_spec((N_KV_A * VT_ROWS, D)),
                      _const_spec((1, Q_LORA)), _const_spec((Q_LORA, N_HEADS_B * LANES)),
                      _const_spec((Q_LORA, N_HEADS_B * LANES)), _const_spec((1, KV_LORA)),
                      _const_spec((KV_LORA, N_HEADS_B * LANES)),
                      _const_spec((WIDTH_B, KV_LORA)),
                      _const_spec((QK_ROPE // 2, 1))],
            out_specs=[tok(WIDTH_A), tok(2 * LANES),
                       pl.BlockSpec((1, N_KV_A * VT_ROWS, tm), lambda i: (i // npb, 0, i % npb)),
                       tok(N_HEADS_B * LANES), tok(N_HEADS_B * LANES),
                       pl.BlockSpec((1, n_sub, WIDTH_B, PROJ_SUB),
                                    lambda i: (i // npb, i % npb, 0, 0))],
            out_shape=[jax.ShapeDtypeStruct((T, WIDTH_A), bf16),
                       jax.ShapeDtypeStruct((T, 2 * LANES), bf16),
                       jax.ShapeDtypeStruct((B, N_KV_A * VT_ROWS, S), bf16),
                       jax.ShapeDtypeStruct((T, N_HEADS_B * LANES), bf16),
                       jax.ShapeDtypeStruct((T, N_HEADS_B * LANES), bf16),
                       jax.ShapeDtypeStruct((B, nkb, WIDTH_B, PROJ_SUB), bf16)],
            compiler_params=pltpu.CompilerParams(dimension_semantics=("arbitrary",),
                                                 vmem_limit_bytes=VMEM_LIMIT),
            name="proj",
        )(x2, positions.reshape(T // tm, 1, tm), row(pre_norm_mix[l]), wit, wit, wit, w_small_t,
          w_va_t, row(q_a_norm[l]), wq_main, wq_swap, row(kv_a_norm[l]), wkv_k, wkv_vt,
          freq_col)

        tq = SWA_TQ
        nb_per = tq // BLOCK
        qa3 = qa.reshape(B, S, WIDTH_A)
        ka3 = ka_x.reshape(B, S, 2 * LANES)
        pos_r3 = positions.reshape(B, 1, S)
        cur = lambda w: pl.BlockSpec((1, tq, w), lambda b, i: (b, i, 0))
        prev_blk = lambda b, i: jnp.maximum(i * nb_per - 1, 0)
        out_a = pl.pallas_call(
            _swa_kernel,
            grid=(B, S // tq),
            in_specs=[pl.BlockSpec(memory_space=pltpu.SMEM),
                      cur(WIDTH_A), cur(2 * LANES),
                      pl.BlockSpec((1, BLOCK, 2 * LANES), lambda b, i: (b, prev_blk(b, i), 0)),
                      pl.BlockSpec((1, N_KV_A * VT_ROWS, tq), lambda b, i: (b, 0, i)),
                      pl.BlockSpec((1, N_KV_A * VT_ROWS, BLOCK), lambda b, i: (b, 0, prev_blk(b, i))),
                      pl.BlockSpec((1, 1, tq), lambda b, i: (b, 0, i)),
                      pl.BlockSpec((1, 1, BLOCK), lambda b, i: (b, 0, prev_blk(b, i)))],
            out_specs=cur(WIDTH_A),
            out_shape=jax.ShapeDtypeStruct((B, S, WIDTH_A), bf16),
            compiler_params=pltpu.CompilerParams(dimension_semantics=("arbitrary", "arbitrary"),
                                                 vmem_limit_bytes=VMEM_LIMIT),
            name="swa",
        )(sinks[l].astype(f32), qa3, ka3, ka3, va_t3, va_t3, pos_r3, pos_r3)

        bq, nh = MLA_BQ, MLA_HEADS
        qb3 = qb.reshape(B, S, N_HEADS_B * LANES)
        kb3 = kb.reshape(B, S, N_HEADS_B * LANES)
        out_b = pl.pallas_call(
            _mla_kernel,
            grid=(B, N_HEADS_B // nh, S // bq),
            in_specs=[pl.BlockSpec((1, bq, nh * LANES), lambda b, j, i: (b, i, j)),
                      pl.BlockSpec((1, S, nh * LANES), lambda b, j, i: (b, 0, j)),
                      pl.BlockSpec((1, nkb, nh * V_DIM_B, bq), lambda b, j, i: (b, 0, j, 0))],
            out_specs=pl.BlockSpec((1, bq, nh * V_DIM_B), lambda b, j, i: (b, i, j)),
            out_shape=jax.ShapeDtypeStruct((B, S, WIDTH_B), bf16),
            scratch_shapes=[pltpu.VMEM((2, nh, bq, bq), f32),
                            pltpu.VMEM((2, nh, 1, bq), f32),
                            pltpu.VMEM((nh, V_DIM_B, bq), f32), pltpu.VMEM((nh, 1, bq), f32),
                            pltpu.VMEM((nh, 1, bq), f32)],
            compiler_params=pltpu.CompilerParams(
                dimension_semantics=("arbitrary", "arbitrary", "arbitrary"),
                vmem_limit_bytes=VMEM_LIMIT),
            name="mla",
        )(qb3, kb3, vt)

        tm = MERGE_TM
        tok = lambda w: pl.BlockSpec((tm, w), lambda i: (i, 0))
        x2 = pl.pallas_call(
            _merge_kernel,
            grid=(T // tm,),
            in_specs=[tok(D), tok(WIDTH_A), tok(WIDTH_B), _const_spec((1, D)),
                      w_rows(2 * D, IN_GATES), _const_spec((WIDTH_A, D)), _const_spec((WIDTH_B, D)),
                      _const_spec((D, D)), _const_spec((1, D)), _const_spec((1, D)),
                      _const_spec((D, D_FF)), _const_spec((D_FF, D)), _const_spec((1, D))],
            out_specs=tok(D),
            out_shape=jax.ShapeDtypeStruct((T, D), f32),
            compiler_params=pltpu.CompilerParams(dimension_semantics=("arbitrary",),
                                                 vmem_limit_bytes=VMEM_LIMIT),
            name="merge_mlp",
        )(x2, out_a.reshape(T, WIDTH_A), out_b.reshape(T, WIDTH_B), row(pre_norm_mix[l]), wit,
          w_o_a[l].astype(bf16), w_o_b[l].astype(bf16), w_out[l].astype(bf16),
          row(post_norm_mix[l]), row(pre_norm_mlp[l]), w_up[l].astype(bf16),
          w_down[l].astype(bf16), row(post_norm_mlp[l]))
        x = x2.reshape(B, S, D)
    return x
```

```python
import jax
import jax.numpy as jnp
from jax import lax
from jax.experimental import pallas as pl
from jax.experimental.pallas import tpu as pltpu

D_MODEL = 1024
N_HEADS_A = 8
N_KV_A = 2
HEAD_DIM_A = 64
WINDOW = 128
BLOCK = 128
N_HEADS_B = 8
QK_NOPE = 64
QK_ROPE = 32
V_DIM_B = 64
Q_LORA = 256
KV_LORA = 128
ROPE_THETA = 10000.0
D_FF = 4 * D_MODEL
EPS = 1e-6

WIDTH_A = N_HEADS_A * HEAD_DIM_A
WIDTH_B = N_HEADS_B * V_DIM_B
Q_HEAD_B = QK_NOPE + QK_ROPE
KV_HEAD_B = QK_NOPE + V_DIM_B

LANES = 128
HALF = LANES // 2
VT_ROWS = LANES
VT_PAD = VT_ROWS - V_DIM_B
SCALE_A = HEAD_DIM_A ** -0.5
SCALE_B = Q_HEAD_B ** -0.5
LOG2_E = 1.4426950408889634
QSCALE_A = SCALE_A * LOG2_E
QSCALE_B = SCALE_B * LOG2_E
ALIBI_SLOPES = tuple(2.0 ** (-8.0 * (h + 1) / N_HEADS_A) for h in range(N_HEADS_A))

IN_GATES = 0
IN_QA = 2 * D_MODEL
IN_KA = IN_QA + WIDTH_A
IN_VA = IN_KA + N_KV_A * HEAD_DIM_A
IN_CQ = IN_VA + N_KV_A * HEAD_DIM_A
IN_CKV = IN_CQ + Q_LORA
IN_KR = IN_CKV + KV_LORA
D_IN = IN_KR + QK_ROPE
SM_KA = 0
SM_KR = SM_KA + 2 * LANES
SM_KRS = SM_KR + LANES
SM_ROWS = SM_KRS + LANES

PROJ_TM = 1024
PROJ_SUB = 512
SWA_TQ = 2048
SWA_LEAD = 4
MLA_BQ = 512
MLA_CHUNK = 512
MLA_HEADS = 4
MLA_LEAD = 1
MERGE_TM = 512
FF_CHUNK = 1024
V7X_VMEM_BYTES = 64 * 1024 * 1024
VMEM_LIMIT = V7X_VMEM_BYTES - 4 * 1024 * 1024


def _rms(v):
    return v * lax.rsqrt(jnp.mean(v * v, axis=-1, keepdims=True) + EPS)


def _dot(a, b):
    return jnp.dot(a, b, preferred_element_type=jnp.float32)


def _dot_nt(a, b):
    return lax.dot_general(a, b, (((1,), (1,)), ((), ())), preferred_element_type=jnp.float32)


def _proj_kernel(x_ref, pos_ref, gpre_ref, wqa_ref, wcq_ref, wckv_ref, wsm_ref, wvat_ref, gq_ref,
                 wqm_ref, wqs_ref, gkv_ref, wkk_ref, wkvt_ref, freq_ref,
                 qa_ref, ka_ref, vat_ref, qb_ref, kb_ref, vt_ref):
    bf16 = jnp.bfloat16
    sub = PROJ_SUB

    def first_stage(n):
        rows = slice(n * sub, (n + 1) * sub)
        hb = (_rms(x_ref[rows, :]) * gpre_ref[...]).astype(bf16)
        qa_ref[rows, :] = (_dot_nt(hb, wqa_ref[...]) * QSCALE_A).astype(bf16)
        small = _dot_nt(hb, wsm_ref[...])
        ka_ref[rows, :] = small[:, SM_KA:SM_KR].astype(bf16)
        va_t = _dot_nt(wvat_ref[...], hb)
        varow = lax.broadcasted_iota(jnp.int32, va_t.shape, 0)
        vat_ref[0, :, rows] = jnp.where(varow % VT_ROWS == HEAD_DIM_A, 1.0, va_t).astype(bf16)
        cq = _dot_nt(hb, wcq_ref[...])
        ckv = _dot_nt(hb, wckv_ref[...])
        pos = pos_ref[0, :, rows].astype(jnp.float32)
        ang = freq_ref[...] * pos
        cos_t, sin_t = jnp.cos(ang), jnp.sin(ang)
        one = jnp.ones((QK_NOPE, sub), jnp.float32)
        zero = jnp.zeros((QK_NOPE, sub), jnp.float32)
        pad = LANES - Q_HEAD_B
        cos = jnp.concatenate([one, cos_t, cos_t, one[:pad]], axis=0).T
        sin = jnp.concatenate([zero, -sin_t, sin_t, zero[:pad]], axis=0).T
        k_rot = small[:, SM_KR:SM_KRS] * cos + small[:, SM_KRS:SM_ROWS] * sin
        return cq, ckv, cos, sin, k_rot

    def second_stage(n, cq, ckv, cos, sin, k_rot):
        rows = slice(n * sub, (n + 1) * sub)
        cqn = (_rms(cq) * gq_ref[...]).astype(bf16)
        q_main = _dot(cqn, wqm_ref[...])
        q_swap = _dot(cqn, wqs_ref[...])
        ckvn = (_rms(ckv) * gkv_ref[...]).astype(bf16)
        k_nope = _dot(ckvn, wkk_ref[...])
        v_t = _dot_nt(wkvt_ref[...], ckvn)
        tail_row = lax.broadcasted_iota(jnp.int32, (VT_PAD, sub), 0)
        tail = jnp.where(tail_row == 0, 1.0, 0.0).astype(bf16)
        for h in range(N_HEADS_B):
            vt_ref[0, n, h * VT_ROWS:h * VT_ROWS + V_DIM_B, :] = (
                v_t[h * V_DIM_B:(h + 1) * V_DIM_B].astype(bf16))
            vt_ref[0, n, h * VT_ROWS + V_DIM_B:(h + 1) * VT_ROWS, :] = tail
        for h in range(N_HEADS_B):
            sl = slice(h * LANES, (h + 1) * LANES)
            qb_ref[rows, sl] = ((q_main[:, sl] * cos + q_swap[:, sl] * sin) * QSCALE_B).astype(bf16)
            kb_ref[rows, sl] = (k_nope[:, sl] + k_rot).astype(bf16)

    n_sub = x_ref.shape[0] // sub
    staged = [first_stage(n) for n in range(n_sub)]
    for n in range(n_sub):
        second_stage(n, *staged[n])


def _swa_kernel(sink_ref, q_ref, kc_ref, kp_ref, vtc_ref, vtp_ref, posc_ref, posp_ref, o_ref):
    bf16 = jnp.bfloat16
    i = pl.program_id(1)
    krow = lax.broadcasted_iota(jnp.int32, (2 * BLOCK, BLOCK), 0)
    qcol = lax.broadcasted_iota(jnp.int32, (2 * BLOCK, BLOCK), 1)
    ahead = krow - qcol
    band = (ahead > 0) & (ahead <= WINDOW)
    lane = lax.broadcasted_iota(jnp.int32, (BLOCK, LANES), 1)
    low_half = lane < HALF
    pos_inf = jnp.float32(jnp.inf)

    n_pairs = N_HEADS_A // 2
    chains = [(blk, pair) for blk in range(SWA_TQ // BLOCK) for pair in range(n_pairs)]
    bands = {}

    def band_of(blk):
        if blk not in bands:
            r0 = blk * BLOCK
            if blk == 0:
                kband = jnp.concatenate([kp_ref[0], kc_ref[0, 0:BLOCK, :]], axis=0)
                vtband = jnp.concatenate([vtp_ref[0], vtc_ref[0, :, 0:BLOCK]], axis=1)
                kpos = jnp.concatenate([posp_ref[0], posc_ref[0, :, 0:BLOCK]], axis=1)
                mask = band & ((krow >= BLOCK) | (i > 0))
            else:
                kband = kc_ref[0, r0 - BLOCK:r0 + BLOCK, :]
                vtband = vtc_ref[0, :, r0 - BLOCK:r0 + BLOCK]
                kpos = posc_ref[0, :, r0 - BLOCK:r0 + BLOCK]
                mask = band
            qpos = posc_ref[0, :, r0:r0 + BLOCK]
            kpos_col = jnp.broadcast_to(kpos, (BLOCK, 2 * BLOCK)).T
            dist = jnp.abs(kpos_col - qpos).astype(jnp.float32) * LOG2_E
            dist = jnp.where(mask, dist, pos_inf)
            bands[blk] = (kband, vtband, dist)
        return bands[blk]

    def scores(blk, pair):
        kv = (2 * pair) // (N_HEADS_A // N_KV_A)
        r0 = blk * BLOCK
        qp = q_ref[0, r0:r0 + BLOCK, pair * LANES:(pair + 1) * LANES]
        zero = jnp.zeros_like(qp)
        q2 = jnp.concatenate([jnp.where(low_half, qp, zero), jnp.where(low_half, zero, qp)],
                             axis=0)
        kx = band_of(blk)[0][:, kv * LANES:(kv + 1) * LANES]
        return _dot_nt(kx, q2)

    def finish(blk, pair, s2):
        kv = (2 * pair) // (N_HEADS_A // N_KV_A)
        r0 = blk * BLOCK
        _, vtband, dist = band_of(blk)
        vt = vtband[kv * VT_ROWS:(kv + 1) * VT_ROWS, :]
        ps, ms = [], []
        for e in range(2):
            h = 2 * pair + e
            s = s2[:, e * BLOCK:(e + 1) * BLOCK] - ALIBI_SLOPES[h] * dist
            m = jnp.maximum(jnp.max(s, axis=0, keepdims=True), sink_ref[h] * LOG2_E)
            ps.append(jnp.exp2(s - m).astype(bf16))
            ms.append(m)
        o2 = _dot(vt, jnp.concatenate(ps, axis=1))
        outs = []
        for e in range(2):
            h = 2 * pair + e
            o_t = o2[:, e * BLOCK:(e + 1) * BLOCK]
            denom = (o_t[HEAD_DIM_A:HEAD_DIM_A + 1]
                     + jnp.exp2(sink_ref[h] * LOG2_E - ms[e]))
            outs.append(o_t[0:HEAD_DIM_A] / denom)
        o_ref[0, r0:r0 + BLOCK, pair * LANES:(pair + 1) * LANES] = (
            jnp.concatenate(outs, axis=0).T.astype(bf16))

    pending = [scores(*c) for c in chains[:SWA_LEAD]]
    for n, c in enumerate(chains):
        s2 = pending.pop(0)
        if n + SWA_LEAD < len(chains):
            pending.append(scores(*chains[n + SWA_LEAD]))
        finish(*c, s2)


def _mla_kernel(q_ref, k_ref, vt_ref, o_ref, s_ref, bmax_ref, acc_ref, m_ref):
    bf16 = jnp.bfloat16
    bq = MLA_BQ
    ch = MLA_CHUNK
    qi = pl.program_id(2)
    krow = lax.broadcasted_iota(jnp.int32, (bq, ch), 0)
    qcol = lax.broadcasted_iota(jnp.int32, (bq, ch), 1)
    neg_inf = jnp.float32(-jnp.inf)
    m_ref[...] = jnp.full(m_ref.shape, neg_inf, jnp.float32)
    acc_ref[...] = jnp.zeros(acc_ref.shape, jnp.float32)

    def scores_head(buf, kb, e):
        start = pl.multiple_of(kb * bq, bq)
        sl = slice(e * LANES, (e + 1) * LANES)
        s = _dot_nt(k_ref[0, pl.ds(start, bq), sl], q_ref[0, :, sl])
        s_ref[buf, e] = s
        bmax_ref[buf, e] = jnp.max(s, axis=0, keepdims=True)

    def scores_into(buf, kb):
        for e in range(MLA_HEADS):
            scores_head(buf, kb, e)

    def softmax_head(buf, e, masked):
        out = []
        for c in range(bq // ch):
            cs = slice(c * ch, (c + 1) * ch)
            s = s_ref[buf, e, :, cs]
            if masked:
                s = jnp.where(krow <= qcol + c * ch, s, neg_inf)
                blockmax = jnp.max(s, axis=0, keepdims=True)
            else:
                blockmax = bmax_ref[buf, e, :, cs]
            m_prev = m_ref[e, :, cs]
            m_new = jnp.maximum(m_prev, blockmax)
            m_ref[e, :, cs] = m_new
            out.append((jnp.exp2(s - m_new).astype(bf16), jnp.exp2(m_prev - m_new)))
        return out

    def pv_head(kb, e, weights):
        vt = vt_ref[0, kb, e * VT_ROWS:(e + 1) * VT_ROWS, :]
        for c, (p, alpha) in enumerate(weights):
            cs = slice(c * ch, (c + 1) * ch)
            acc_ref[e, :, cs] = alpha * acc_ref[e, :, cs] + _dot(vt, p)

    def softmax_pv(buf, kb, masked):
        for e in range(MLA_HEADS):
            pv_head(kb, e, softmax_head(buf, e, masked))

    def pipelined(first_kb, n_blocks):
        blocks = [(j % 2, first_kb + j) for j in range(n_blocks)]
        units = [(buf, kb, e) for buf, kb in blocks for e in range(MLA_HEADS)]
        prods = [(1 - buf, kb + 1, e) for buf, kb in blocks for e in range(MLA_HEADS)]
        for n in range(min(MLA_LEAD, len(prods))):
            scores_head(*prods[n])
        for n, (buf, kb, e) in enumerate(units):
            pv_head(kb, e, softmax_head(buf, e, False))
            if n + MLA_LEAD < len(prods):
                scores_head(*prods[n + MLA_LEAD])

    def quad_body(t, carry):
        pipelined(4 * t, 4)
        return carry

    scores_into(0, 0)
    lax.fori_loop(0, qi // 4, quad_body, 0)

    @pl.when(qi % 4 >= 2)
    def _():
        pipelined((qi // 4) * 4, 2)

    @pl.when(qi % 2 == 1)
    def _():
        pipelined(qi - 1, 1)
        softmax_pv(1, qi, True)

    @pl.when(qi % 2 == 0)
    def _():
        softmax_pv(0, qi, True)

    outs = []
    for e in range(MLA_HEADS):
        acc = acc_ref[e]
        outs.append(acc[0:V_DIM_B] / acc[V_DIM_B:V_DIM_B + 1])
    o_ref[0] = jnp.concatenate(outs, axis=0).T.astype(bf16)


def _merge_kernel(x_ref, oa_ref, ob_ref, gpre_ref, wg_ref, woa_ref, wob_ref, wout_ref, gpost_ref,
                  gpre2_ref, wup_ref, wdn_ref, gpost2_ref, o_ref):
    bf16 = jnp.bfloat16
    x = x_ref[...]
    hm = x.shape[0] // 2
    halves = (slice(0, hm), slice(hm, 2 * hm))
    hb = [(_rms(x[r]) * gpre_ref[...]).astype(bf16) for r in halves]
    gate_a = jnp.concatenate([_dot_nt(h, wg_ref[0:D_MODEL, :]) for h in hb], axis=0)
    hb = jnp.concatenate(hb, axis=0)
    gate_a = jax.nn.sigmoid(gate_a)
    gate_b = jax.nn.sigmoid(_dot_nt(hb, wg_ref[D_MODEL:2 * D_MODEL, :]))
    merged = gate_a * _dot(oa_ref[...], woa_ref[...]) + gate_b * _dot(ob_ref[...], wob_ref[...])
    mb = merged.astype(bf16)

    n_chunks = D_FF // FF_CHUNK
    chunk = lambda c: slice(c * FF_CHUNK, (c + 1) * FF_CHUNK)

    def sq_relu(v):
        v = jnp.maximum(v, 0.0)
        return (v * v).astype(bf16)

    x1, h2, up0 = [], [], []
    for r in halves:
        y = _dot(mb[r], wout_ref[...])
        x1.append(x[r] + _rms(y) * gpost_ref[...])
        h2.append((_rms(x1[-1]) * gpre2_ref[...]).astype(bf16))
    for i in range(2):
        up0.append(sq_relu(_dot(h2[i], wup_ref[:, chunk(0)])))
    h2 = jnp.concatenate(h2, axis=0)
    y2 = _dot(jnp.concatenate(up0, axis=0), wdn_ref[chunk(0), :])
    for c in range(1, n_chunks - 1):
        y2 = y2 + _dot(sq_relu(_dot(h2, wup_ref[:, chunk(c)])), wdn_ref[chunk(c), :])
    last = sq_relu(_dot(h2, wup_ref[:, chunk(n_chunks - 1)]))
    for i, r in enumerate(halves):
        y2_half = y2[r] + _dot(last[r], wdn_ref[chunk(n_chunks - 1), :])
        o_ref[r, :] = x1[i] + _rms(y2_half) * gpost2_ref[...]


def _const_spec(shape):
    return pl.BlockSpec(shape, lambda *_: (0,) * len(shape), pipeline_mode=pl.Buffered(1))


def kernel(x, positions, pre_norm_mix, w_in, q_a_norm, w_q_b, kv_a_norm, w_kv_b, sinks, w_o_a,
           w_o_b, w_out, post_norm_mix, pre_norm_mlp, w_up, w_down, post_norm_mlp):
    f32, bf16 = jnp.float32, jnp.bfloat16
    B, S, D = x.shape
    T = B * S
    depth = w_in.shape[0]
    for l in range(depth):
        assert w_in.shape[2] == D_IN
        wit = jnp.swapaxes(w_in[l], 0, 1).astype(bf16)
        ka_t = wit[IN_KA:IN_VA]
        va_t = wit[IN_VA:IN_CQ]
        kr_t = wit[IN_KR:D_IN]
        hr = QK_ROPE // 2
        hd = HEAD_DIM_A
        z = lambda n: jnp.zeros((n, D_MODEL), bf16)
        w_small_t = jnp.concatenate([
            ka_t[:hd], ka_t[:hd], ka_t[hd:], ka_t[hd:],
            z(QK_NOPE), kr_t, z(LANES - Q_HEAD_B),
            z(QK_NOPE), kr_t[hr:], kr_t[:hr], z(LANES - Q_HEAD_B)], axis=0)
        w_va_t = jnp.concatenate([va_t[:hd], z(VT_PAD), va_t[hd:], z(VT_PAD)], axis=0)

        wq = w_q_b[l].reshape(Q_LORA, N_HEADS_B, Q_HEAD_B)
        q_nope, q_rope = wq[..., :QK_NOPE], wq[..., QK_NOPE:]
        zq = lambda n: jnp.zeros((Q_LORA, N_HEADS_B, n), f32)
        wq_main = jnp.concatenate([q_nope, q_rope, zq(LANES - Q_HEAD_B)], -1)
        wq_swap = jnp.concatenate([zq(QK_NOPE), q_rope[..., hr:], q_rope[..., :hr],
                                   zq(LANES - Q_HEAD_B)], -1)
        wq_main = wq_main.reshape(Q_LORA, N_HEADS_B * LANES).astype(bf16)
        wq_swap = wq_swap.reshape(Q_LORA, N_HEADS_B * LANES).astype(bf16)

        wkv = w_kv_b[l].reshape(KV_LORA, N_HEADS_B, KV_HEAD_B)
        kv_k, kv_v = wkv[..., :QK_NOPE], wkv[..., QK_NOPE:]
        zk = jnp.zeros((KV_LORA, N_HEADS_B, HALF), f32)
        wkv_k = jnp.concatenate([kv_k, zk], -1).reshape(KV_LORA, N_HEADS_B * LANES).astype(bf16)
        wkv_vt = kv_v.reshape(KV_LORA, WIDTH_B).T.astype(bf16)

        freq_col = (ROPE_THETA ** (-jnp.arange(0, QK_ROPE, 2, dtype=f32) / QK_ROPE))[:, None]

        row = lambda g: g.reshape(1, -1).astype(f32)
        x2 = x.reshape(T, D)

        tm = PROJ_TM
        assert PROJ_SUB == MLA_BQ and tm % PROJ_SUB == 0 and S % tm == 0
        nkb = S // MLA_BQ
        n_sub = tm // PROJ_SUB
        npb = S // tm
        tok = lambda w: pl.BlockSpec((tm, w), lambda i: (i, 0))
        w_rows = lambda n, start: pl.BlockSpec((n, D), lambda i: (start // n, 0),
                                               pipeline_mode=pl.Buffered(1))
        assert IN_QA % WIDTH_A == 0 and IN_CQ % Q_LORA == 0 and IN_CKV % KV_LORA == 0
        qa, ka_x, va_t3, qb, kb, vt = pl.pallas_call(
            _proj_kernel,
            grid=(T // tm,),
            in_specs=[tok(D), pl.BlockSpec((1, 1, tm), lambda i: (i, 0, 0)), _const_spec((1, D)),
                      w_rows(WIDTH_A, IN_QA), w_rows(Q_LORA, IN_CQ), w_rows(KV_LORA, IN_CKV),
                      _const_spec((SM_ROWS, D)), _const_spec((N_KV_A * VT_ROWS, D)),
                      _const_spec((1, Q_LORA)), _const_spec((Q_LORA, N_HEADS_B * LANES)),
                      _const_spec((Q_LORA, N_HEADS_B * LANES)), _const_spec((1, KV_LORA)),
                      _const_spec((KV_LORA, N_HEADS_B * LANES)),
                      _const_spec((WIDTH_B, KV_LORA)),
                      _const_spec((QK_ROPE // 2, 1))],
            out_specs=[tok(WIDTH_A), tok(2 * LANES),
                       pl.BlockSpec((1, N_KV_A * VT_ROWS, tm), lambda i: (i // npb, 0, i % npb)),
                       tok(N_HEADS_B * LANES), tok(N_HEADS_B * LANES),
                       pl.BlockSpec((1, n_sub, N_HEADS_B * VT_ROWS, PROJ_SUB),
                                    lambda i: (i // npb, i % npb, 0, 0))],
            out_shape=[jax.ShapeDtypeStruct((T, WIDTH_A), bf16),
                       jax.ShapeDtypeStruct((T, 2 * LANES), bf16),
                       jax.ShapeDtypeStruct((B, N_KV_A * VT_ROWS, S), bf16),
                       jax.ShapeDtypeStruct((T, N_HEADS_B * LANES), bf16),
                       jax.ShapeDtypeStruct((T, N_HEADS_B * LANES), bf16),
                       jax.ShapeDtypeStruct((B, nkb, N_HEADS_B * VT_ROWS, PROJ_SUB), bf16)],
            compiler_params=pltpu.CompilerParams(dimension_semantics=("arbitrary",),
                                                 vmem_limit_bytes=VMEM_LIMIT),
            name="proj",
        )(x2, positions.reshape(T // tm, 1, tm), row(pre_norm_mix[l]), wit, wit, wit, w_small_t,
          w_va_t, row(q_a_norm[l]), wq_main, wq_swap, row(kv_a_norm[l]), wkv_k, wkv_vt,
          freq_col)

        tq = SWA_TQ
        nb_per = tq // BLOCK
        qa3 = qa.reshape(B, S, WIDTH_A)
        ka3 = ka_x.reshape(B, S, 2 * LANES)
        pos_r3 = positions.reshape(B, 1, S)
        cur = lambda w: pl.BlockSpec((1, tq, w), lambda b, i: (b, i, 0))
        prev_blk = lambda b, i: jnp.maximum(i * nb_per - 1, 0)
        out_a = pl.pallas_call(
            _swa_kernel,
            grid=(B, S // tq),
            in_specs=[pl.BlockSpec(memory_space=pltpu.SMEM),
                      cur(WIDTH_A), cur(2 * LANES),
                      pl.BlockSpec((1, BLOCK, 2 * LANES), lambda b, i: (b, prev_blk(b, i), 0)),
                      pl.BlockSpec((1, N_KV_A * VT_ROWS, tq), lambda b, i: (b, 0, i)),
                      pl.BlockSpec((1, N_KV_A * VT_ROWS, BLOCK), lambda b, i: (b, 0, prev_blk(b, i))),
                      pl.BlockSpec((1, 1, tq), lambda b, i: (b, 0, i)),
                      pl.BlockSpec((1, 1, BLOCK), lambda b, i: (b, 0, prev_blk(b, i)))],
            out_specs=cur(WIDTH_A),
            out_shape=jax.ShapeDtypeStruct((B, S, WIDTH_A), bf16),
            compiler_params=pltpu.CompilerParams(dimension_semantics=("arbitrary", "arbitrary"),
                                                 vmem_limit_bytes=VMEM_LIMIT),
            name="swa",
        )(sinks[l].astype(f32), qa3, ka3, ka3, va_t3, va_t3, pos_r3, pos_r3)

        bq, nh = MLA_BQ, MLA_HEADS
        qb3 = qb.reshape(B, S, N_HEADS_B * LANES)
        kb3 = kb.reshape(B, S, N_HEADS_B * LANES)
        out_b = pl.pallas_call(
            _mla_kernel,
            grid=(B, N_HEADS_B // nh, S // bq),
            in_specs=[pl.BlockSpec((1, bq, nh * LANES), lambda b, j, i: (b, i, j)),
                      pl.BlockSpec((1, S, nh * LANES), lambda b, j, i: (b, 0, j)),
                      pl.BlockSpec((1, nkb, nh * VT_ROWS, bq), lambda b, j, i: (b, 0, j, 0))],
            out_specs=pl.BlockSpec((1, bq, nh * V_DIM_B), lambda b, j, i: (b, i, j)),
            out_shape=jax.ShapeDtypeStruct((B, S, WIDTH_B), bf16),
            scratch_shapes=[pltpu.VMEM((2, nh, bq, bq), f32),
                            pltpu.VMEM((2, nh, 1, bq), f32),
                            pltpu.VMEM((nh, VT_ROWS, bq), f32), pltpu.VMEM((nh, 1, bq), f32)],
            compiler_params=pltpu.CompilerParams(
                dimension_semantics=("arbitrary", "arbitrary", "arbitrary"),
                vmem_limit_bytes=VMEM_LIMIT),
            name="mla",
        )(qb3, kb3, vt)

        tm = MERGE_TM
        tok = lambda w: pl.BlockSpec((tm, w), lambda i: (i, 0))
        x2 = pl.pallas_call(
            _merge_kernel,
            grid=(T // tm,),
            in_specs=[tok(D), tok(WIDTH_A), tok(WIDTH_B), _const_spec((1, D)),
                      w_rows(2 * D, IN_GATES), _const_spec((WIDTH_A, D)), _const_spec((WIDTH_B, D)),
                      _const_spec((D, D)), _const_spec((1, D)), _const_spec((1, D)),
                      _const_spec((D, D_FF)), _const_spec((D_FF, D)), _const_spec((1, D))],
            out_specs=tok(D),
            out_shape=jax.ShapeDtypeStruct((T, D), f32),
            compiler_params=pltpu.CompilerParams(dimension_semantics=("arbitrary",),
                                                 vmem_limit_bytes=VMEM_LIMIT),
            name="merge_mlp",
        )(x2, out_a.reshape(T, WIDTH_A), out_b.reshape(T, WIDTH_B), row(pre_norm_mix[l]), wit,
          w_o_a[l].astype(bf16), w_o_b[l].astype(bf16), w_out[l].astype(bf16),
          row(post_norm_mix[l]), row(pre_norm_mlp[l]), w_up[l].astype(bf16),
          w_down[l].astype(bf16), row(post_norm_mlp[l]))
        x = x2.reshape(B, S, D)
    return x
```

```python
import jax
import jax.numpy as jnp
from jax import lax
from jax.experimental import pallas as pl
from jax.experimental.pallas import tpu as pltpu

D_MODEL = 1024
N_HEADS_A = 8
N_KV_A = 2
HEAD_DIM_A = 64
WINDOW = 128
BLOCK = 128
N_HEADS_B = 8
QK_NOPE = 64
QK_ROPE = 32
V_DIM_B = 64
Q_LORA = 256
KV_LORA = 128
ROPE_THETA = 10000.0
D_FF = 4 * D_MODEL
EPS = 1e-6

WIDTH_A = N_HEADS_A * HEAD_DIM_A
WIDTH_B = N_HEADS_B * V_DIM_B
Q_HEAD_B = QK_NOPE + QK_ROPE
KV_HEAD_B = QK_NOPE + V_DIM_B

LANES = 128
HALF = LANES // 2
VT_ROWS = LANES
VT_PAD = VT_ROWS - V_DIM_B
SCALE_A = HEAD_DIM_A ** -0.5
SCALE_B = Q_HEAD_B ** -0.5
LOG2_E = 1.4426950408889634
QSCALE_A = SCALE_A * LOG2_E
QSCALE_B = SCALE_B * LOG2_E
ALIBI_SLOPES = tuple(2.0 ** (-8.0 * (h + 1) / N_HEADS_A) for h in range(N_HEADS_A))

IN_GATES = 0
IN_QA = 2 * D_MODEL
IN_KA = IN_QA + WIDTH_A
IN_VA = IN_KA + N_KV_A * HEAD_DIM_A
IN_CQ = IN_VA + N_KV_A * HEAD_DIM_A
IN_CKV = IN_CQ + Q_LORA
IN_KR = IN_CKV + KV_LORA
D_IN = IN_KR + QK_ROPE
SM_KA = 0
SM_KR = SM_KA + 2 * LANES
SM_KRS = SM_KR + LANES
SM_ROWS = SM_KRS + LANES

PROJ_TM = 1024
PROJ_SUB = 512
SWA_TQ = 2048
SWA_LEAD = 4
MLA_BQ = 512
MLA_CHUNK = 512
MLA_HEADS = 4
MLA_LEAD = 1
MERGE_TM = 512
FF_CHUNK = 1024
V7X_VMEM_BYTES = 64 * 1024 * 1024
VMEM_LIMIT = V7X_VMEM_BYTES - 4 * 1024 * 1024


def _rms(v):
    return v * lax.rsqrt(jnp.mean(v * v, axis=-1, keepdims=True) + EPS)


def _dot(a, b):
    return jnp.dot(a, b, preferred_element_type=jnp.float32)


def _dot_nt(a, b):
    return lax.dot_general(a, b, (((1,), (1,)), ((), ())), preferred_element_type=jnp.float32)


def _proj_kernel(x_ref, pos_ref, gpre_ref, wqa_ref, wcq_ref, wckv_ref, wsm_ref, wvat_ref, gq_ref,
                 wqm_ref, wqs_ref, gkv_ref, wkk_ref, wkvt_ref, freq_ref,
                 qa_ref, ka_ref, vat_ref, qb_ref, kb_ref, vt_ref):
    bf16 = jnp.bfloat16
    sub = PROJ_SUB

    def first_stage(n):
        rows = slice(n * sub, (n + 1) * sub)
        hb = (_rms(x_ref[rows, :]) * gpre_ref[...]).astype(bf16)
        qa_ref[rows, :] = (_dot_nt(hb, wqa_ref[...]) * QSCALE_A).astype(bf16)
        small = _dot_nt(hb, wsm_ref[...])
        ka_ref[rows, :] = small[:, SM_KA:SM_KR].astype(bf16)
        va_t = _dot_nt(wvat_ref[...], hb)
        varow = lax.broadcasted_iota(jnp.int32, va_t.shape, 0)
        vat_ref[0, :, rows] = jnp.where(varow % VT_ROWS == HEAD_DIM_A, 1.0, va_t).astype(bf16)
        cq = _dot_nt(hb, wcq_ref[...])
        ckv = _dot_nt(hb, wckv_ref[...])
        pos = pos_ref[0, :, rows].astype(jnp.float32)
        ang = freq_ref[...] * pos
        cos_t, sin_t = jnp.cos(ang), jnp.sin(ang)
        one = jnp.ones((QK_NOPE, sub), jnp.float32)
        zero = jnp.zeros((QK_NOPE, sub), jnp.float32)
        pad = LANES - Q_HEAD_B
        cos = jnp.concatenate([one, cos_t, cos_t, one[:pad]], axis=0).T
        sin = jnp.concatenate([zero, -sin_t, sin_t, zero[:pad]], axis=0).T
        k_rot = small[:, SM_KR:SM_KRS] * cos + small[:, SM_KRS:SM_ROWS] * sin
        return cq, ckv, cos, sin, k_rot

    def second_stage(n, cq, ckv, cos, sin, k_rot):
        rows = slice(n * sub, (n + 1) * sub)
        cqn = (_rms(cq) * gq_ref[...]).astype(bf16)
        q_main = _dot(cqn, wqm_ref[...])
        q_swap = _dot(cqn, wqs_ref[...])
        ckvn = (_rms(ckv) * gkv_ref[...]).astype(bf16)
        k_nope = _dot(ckvn, wkk_ref[...])
        v_t = _dot_nt(wkvt_ref[...], ckvn)
        tail_row = lax.broadcasted_iota(jnp.int32, (VT_PAD, sub), 0)
        tail = jnp.where(tail_row == 0, 1.0, 0.0).astype(bf16)
        for h in range(N_HEADS_B):
            vt_ref[0, n, h * VT_ROWS:h * VT_ROWS + V_DIM_B, :] = (
                v_t[h * V_DIM_B:(h + 1) * V_DIM_B].astype(bf16))
            vt_ref[0, n, h * VT_ROWS + V_DIM_B:(h + 1) * VT_ROWS, :] = tail
        for h in range(N_HEADS_B):
            sl = slice(h * LANES, (h + 1) * LANES)
            qb_ref[rows, sl] = ((q_main[:, sl] * cos + q_swap[:, sl] * sin) * QSCALE_B).astype(bf16)
            kb_ref[rows, sl] = (k_nope[:, sl] + k_rot).astype(bf16)

    n_sub = x_ref.shape[0] // sub
    staged = [first_stage(n) for n in range(n_sub)]
    for n in range(n_sub):
        second_stage(n, *staged[n])


def _swa_kernel(sink_ref, q_ref, kc_ref, kp_ref, vtc_ref, vtp_ref, posc_ref, posp_ref, o_ref):
    bf16 = jnp.bfloat16
    i = pl.program_id(1)
    krow = lax.broadcasted_iota(jnp.int32, (2 * BLOCK, BLOCK), 0)
    qcol = lax.broadcasted_iota(jnp.int32, (2 * BLOCK, BLOCK), 1)
    ahead = krow - qcol
    band = (ahead > 0) & (ahead <= WINDOW)
    lane = lax.broadcasted_iota(jnp.int32, (BLOCK, LANES), 1)
    low_half = lane < HALF
    pos_inf = jnp.float32(jnp.inf)

    n_pairs = N_HEADS_A // 2
    chains = [(blk, pair) for blk in range(SWA_TQ // BLOCK) for pair in range(n_pairs)]
    bands = {}

    def band_of(blk):
        if blk not in bands:
            r0 = blk * BLOCK
            if blk == 0:
                kband = jnp.concatenate([kp_ref[0], kc_ref[0, 0:BLOCK, :]], axis=0)
                vtband = jnp.concatenate([vtp_ref[0], vtc_ref[0, :, 0:BLOCK]], axis=1)
                kpos = jnp.concatenate([posp_ref[0], posc_ref[0, :, 0:BLOCK]], axis=1)
                mask = band & ((krow >= BLOCK) | (i > 0))
            else:
                kband = kc_ref[0, r0 - BLOCK:r0 + BLOCK, :]
                vtband = vtc_ref[0, :, r0 - BLOCK:r0 + BLOCK]
                kpos = posc_ref[0, :, r0 - BLOCK:r0 + BLOCK]
                mask = band
            qpos = posc_ref[0, :, r0:r0 + BLOCK]
            kpos_col = jnp.broadcast_to(kpos, (BLOCK, 2 * BLOCK)).T
            dist = jnp.abs(kpos_col - qpos).astype(jnp.float32) * LOG2_E
            dist = jnp.where(mask, dist, pos_inf)
            bands[blk] = (kband, vtband, dist)
        return bands[blk]

    def scores(blk, pair):
        kv = (2 * pair) // (N_HEADS_A // N_KV_A)
        r0 = blk * BLOCK
        qp = q_ref[0, r0:r0 + BLOCK, pair * LANES:(pair + 1) * LANES]
        zero = jnp.zeros_like(qp)
        q2 = jnp.concatenate([jnp.where(low_half, qp, zero), jnp.where(low_half, zero, qp)],
                             axis=0)
        kx = band_of(blk)[0][:, kv * LANES:(kv + 1) * LANES]
        return _dot_nt(kx, q2)

    def finish(blk, pair, s2):
        kv = (2 * pair) // (N_HEADS_A // N_KV_A)
        r0 = blk * BLOCK
        _, vtband, dist = band_of(blk)
        vt = vtband[kv * VT_ROWS:(kv + 1) * VT_ROWS, :]
        ps, ms = [], []
        for e in range(2):
            h = 2 * pair + e
            s = s2[:, e * BLOCK:(e + 1) * BLOCK] - ALIBI_SLOPES[h] * dist
            m = jnp.maximum(jnp.max(s, axis=0, keepdims=True), sink_ref[h] * LOG2_E)
            ps.append(jnp.exp2(s - m).astype(bf16))
            ms.append(m)
        o2 = _dot(vt, jnp.concatenate(ps, axis=1))
        outs = []
        for e in range(2):
            h = 2 * pair + e
            o_t = o2[:, e * BLOCK:(e + 1) * BLOCK]
            denom = (o_t[HEAD_DIM_A:HEAD_DIM_A + 1]
                     + jnp.exp2(sink_ref[h] * LOG2_E - ms[e]))
            outs.append(o_t[0:HEAD_DIM_A] / denom)
        o_ref[0, r0:r0 + BLOCK, pair * LANES:(pair + 1) * LANES] = (
            jnp.concatenate(outs, axis=0).T.astype(bf16))

    pending = [scores(*c) for c in chains[:SWA_LEAD]]
    for n, c in enumerate(chains):
        s2 = pending.pop(0)
        if n + SWA_LEAD < len(chains):
            pending.append(scores(*chains[n + SWA_LEAD]))
        finish(*c, s2)


def _mla_kernel(q_ref, k_ref, vt_ref, o_ref, s_ref, bmax_ref, acc_ref, m_ref):
    bf16 = jnp.bfloat16
    bq = MLA_BQ
    ch = MLA_CHUNK
    qi = pl.program_id(2)
    krow = lax.broadcasted_iota(jnp.int32, (bq, ch), 0)
    qcol = lax.broadcasted_iota(jnp.int32, (bq, ch), 1)
    neg_inf = jnp.float32(-jnp.inf)
    m_ref[...] = jnp.full(m_ref.shape, neg_inf, jnp.float32)
    acc_ref[...] = jnp.zeros(acc_ref.shape, jnp.float32)

    def scores_head(buf, kb, e):
        start = pl.multiple_of(kb * bq, bq)
        sl = slice(e * LANES, (e + 1) * LANES)
        s = _dot_nt(k_ref[0, pl.ds(start, bq), sl], q_ref[0, :, sl])
        s_ref[buf, e] = s
        bmax_ref[buf, e] = jnp.max(s, axis=0, keepdims=True)

    def scores_into(buf, kb):
        for e in range(MLA_HEADS):
            scores_head(buf, kb, e)

    def softmax_head(buf, e, masked):
        out = []
        for c in range(bq // ch):
            cs = slice(c * ch, (c + 1) * ch)
            s = s_ref[buf, e, :, cs]
            if masked:
                s = jnp.where(krow <= qcol + c * ch, s, neg_inf)
                blockmax = jnp.max(s, axis=0, keepdims=True)
            else:
                blockmax = bmax_ref[buf, e, :, cs]
            m_prev = m_ref[e, :, cs]
            m_new = jnp.maximum(m_prev, blockmax)
            m_ref[e, :, cs] = m_new
            out.append((jnp.exp2(s - m_new).astype(bf16), jnp.exp2(m_prev - m_new)))
        return out

    def pv_head(kb, e, weights):
        vt = vt_ref[0, kb, e * VT_ROWS:(e + 1) * VT_ROWS, :]
        for c, (p, alpha) in enumerate(weights):
            cs = slice(c * ch, (c + 1) * ch)
            acc_ref[e, :, cs] = alpha * acc_ref[e, :, cs] + _dot(vt, p)

    def softmax_pv(buf, kb, masked):
        for e in range(MLA_HEADS):
            pv_head(kb, e, softmax_head(buf, e, masked))

    def pipelined(first_kb, n_blocks):
        blocks = [(j % 2, first_kb + j) for j in range(n_blocks)]
        units = [(buf, kb, e) for buf, kb in blocks for e in range(MLA_HEADS)]
        prods = [(1 - buf, kb + 1, e) for buf, kb in blocks for e in range(MLA_HEADS)]
        for n in range(min(MLA_LEAD, len(prods))):
            scores_head(*prods[n])
        for n, (buf, kb, e) in enumerate(units):
            pv_head(kb, e, softmax_head(buf, e, False))
            if n + MLA_LEAD < len(prods):
                scores_head(*prods[n + MLA_LEAD])

    def quad_body(t, carry):
        pipelined(4 * t, 4)
        return carry

    scores_into(0, 0)
    lax.fori_loop(0, qi // 4, quad_body, 0)

    @pl.when(qi % 4 >= 2)
    def _():
        pipelined((qi // 4) * 4, 2)

    @pl.when(qi % 2 == 1)
    def _():
        pipelined(qi - 1, 1)
        softmax_pv(1, qi, True)

    @pl.when(qi % 2 == 0)
    def _():
        softmax_pv(0, qi, True)

    outs = []
    for e in range(MLA_HEADS):
        acc = acc_ref[e]
        outs.append(acc[0:V_DIM_B] / acc[V_DIM_B:V_DIM_B + 1])
    o_ref[0] = jnp.concatenate(outs, axis=0).T.astype(bf16)


def _merge_kernel(x_ref, oa_ref, ob_ref, gpre_ref, wg_ref, woa_ref, wob_ref, wout_ref, gpost_ref,
                  gpre2_ref, wup_ref, wdn_ref, gpost2_ref, o_ref):
    bf16 = jnp.bfloat16
    x = x_ref[...]
    hm = x.shape[0] // 2
    halves = (slice(0, hm), slice(hm, 2 * hm))
    hb = [(_rms(x[r]) * gpre_ref[...]).astype(bf16) for r in halves]
    gate_a = jnp.concatenate([_dot_nt(h, wg_ref[0:D_MODEL, :]) for h in hb], axis=0)
    hb = jnp.concatenate(hb, axis=0)
    gate_a = jax.nn.sigmoid(gate_a)
    gate_b = jax.nn.sigmoid(_dot_nt(hb, wg_ref[D_MODEL:2 * D_MODEL, :]))
    merged = gate_a * _dot(oa_ref[...], woa_ref[...]) + gate_b * _dot(ob_ref[...], wob_ref[...])
    mb = merged.astype(bf16)

    n_chunks = D_FF // FF_CHUNK
    chunk = lambda c: slice(c * FF_CHUNK, (c + 1) * FF_CHUNK)

    def w_down(c):
        return wdn_ref[chunk(c), :].astype(bf16)

    def sq_relu(v):
        v = jnp.maximum(v, 0.0)
        return (v * v).astype(bf16)

    x1, h2, up0 = [], [], []
    for r in halves:
        y = _dot(mb[r], wout_ref[...])
        x1.append(x[r] + _rms(y) * gpost_ref[...])
        h2.append((_rms(x1[-1]) * gpre2_ref[...]).astype(bf16))
    for i in range(2):
        up0.append(sq_relu(_dot(h2[i], wup_ref[:, chunk(0)])))
    h2 = jnp.concatenate(h2, axis=0)
    y2 = _dot(jnp.concatenate(up0, axis=0), w_down(0))
    for c in range(1, n_chunks - 1):
        y2 = y2 + _dot(sq_relu(_dot(h2, wup_ref[:, chunk(c)])), w_down(c))
    last = sq_relu(_dot(h2, wup_ref[:, chunk(n_chunks - 1)]))
    w_last = w_down(n_chunks - 1)
    for i, r in enumerate(halves):
        y2_half = y2[r] + _dot(last[r], w_last)
        o_ref[r, :] = x1[i] + _rms(y2_half) * gpost2_ref[...]


def _const_spec(shape):
    return pl.BlockSpec(shape, lambda *_: (0,) * len(shape), pipeline_mode=pl.Buffered(1))


def kernel(x, positions, pre_norm_mix, w_in, q_a_norm, w_q_b, kv_a_norm, w_kv_b, sinks, w_o_a,
           w_o_b, w_out, post_norm_mix, pre_norm_mlp, w_up, w_down, post_norm_mlp):
    f32, bf16 = jnp.float32, jnp.bfloat16
    B, S, D = x.shape
    T = B * S
    depth = w_in.shape[0]
    for l in range(depth):
        assert w_in.shape[2] == D_IN
        wit = jnp.swapaxes(w_in[l], 0, 1).astype(bf16)
        ka_t = wit[IN_KA:IN_VA]
        va_t = wit[IN_VA:IN_CQ]
        kr_t = wit[IN_KR:D_IN]
        hr = QK_ROPE // 2
        hd = HEAD_DIM_A
        z = lambda n: jnp.zeros((n, D_MODEL), bf16)
        w_small_t = jnp.concatenate([
            ka_t[:hd], ka_t[:hd], ka_t[hd:], ka_t[hd:],
            z(QK_NOPE), kr_t, z(LANES - Q_HEAD_B),
            z(QK_NOPE), kr_t[hr:], kr_t[:hr], z(LANES - Q_HEAD_B)], axis=0)
        w_va_t = jnp.concatenate([va_t[:hd], z(VT_PAD), va_t[hd:], z(VT_PAD)], axis=0)

        wq = w_q_b[l].reshape(Q_LORA, N_HEADS_B, Q_HEAD_B)
        q_nope, q_rope = wq[..., :QK_NOPE], wq[..., QK_NOPE:]
        zq = lambda n: jnp.zeros((Q_LORA, N_HEADS_B, n), f32)
        wq_main = jnp.concatenate([q_nope, q_rope, zq(LANES - Q_HEAD_B)], -1)
        wq_swap = jnp.concatenate([zq(QK_NOPE), q_rope[..., hr:], q_rope[..., :hr],
                                   zq(LANES - Q_HEAD_B)], -1)
        wq_main = wq_main.reshape(Q_LORA, N_HEADS_B * LANES).astype(bf16)
        wq_swap = wq_swap.reshape(Q_LORA, N_HEADS_B * LANES).astype(bf16)

        wkv = w_kv_b[l].reshape(KV_LORA, N_HEADS_B, KV_HEAD_B)
        kv_k, kv_v = wkv[..., :QK_NOPE], wkv[..., QK_NOPE:]
        zk = jnp.zeros((KV_LORA, N_HEADS_B, HALF), f32)
        wkv_k = jnp.concatenate([kv_k, zk], -1).reshape(KV_LORA, N_HEADS_B * LANES).astype(bf16)
        wkv_vt = kv_v.reshape(KV_LORA, WIDTH_B).T.astype(bf16)

        freq_col = (ROPE_THETA ** (-jnp.arange(0, QK_ROPE, 2, dtype=f32) / QK_ROPE))[:, None]

        row = lambda g: g.reshape(1, -1).astype(f32)
        x2 = x.reshape(T, D)

        tm = PROJ_TM
        assert PROJ_SUB == MLA_BQ and tm % PROJ_SUB == 0 and S % tm == 0
        nkb = S // MLA_BQ
        n_sub = tm // PROJ_SUB
        npb = S // tm
        tok = lambda w: pl.BlockSpec((tm, w), lambda i: (i, 0))
        w_rows = lambda n, start: pl.BlockSpec((n, D), lambda i: (start // n, 0),
                                               pipeline_mode=pl.Buffered(1))
        assert IN_QA % WIDTH_A == 0 and IN_CQ % Q_LORA == 0 and IN_CKV % KV_LORA == 0
        qa, ka_x, va_t3, qb, kb, vt = pl.pallas_call(
            _proj_kernel,
            grid=(T // tm,),
            in_specs=[tok(D), pl.BlockSpec((1, 1, tm), lambda i: (i, 0, 0)), _const_spec((1, D)),
                      w_rows(WIDTH_A, IN_QA), w_rows(Q_LORA, IN_CQ), w_rows(KV_LORA, IN_CKV),
                      _const_spec((SM_ROWS, D)), _const_spec((N_KV_A * VT_ROWS, D)),
                      _const_spec((1, Q_LORA)), _const_spec((Q_LORA, N_HEADS_B * LANES)),
                      _const_spec((Q_LORA, N_HEADS_B * LANES)), _const_spec((1, KV_LORA)),
                      _const_spec((KV_LORA, N_HEADS_B * LANES)),
                      _const_spec((WIDTH_B, KV_LORA)),
                      _const_spec((QK_ROPE // 2, 1))],
            out_specs=[tok(WIDTH_A), tok(2 * LANES),
                       pl.BlockSpec((1, N_KV_A * VT_ROWS, tm), lambda i: (i // npb, 0, i % npb)),
                       tok(N_HEADS_B * LANES), tok(N_HEADS_B * LANES),
                       pl.BlockSpec((1, n_sub, N_HEADS_B * VT_ROWS, PROJ_SUB),
                                    lambda i: (i // npb, i % npb, 0, 0))],
            out_shape=[jax.ShapeDtypeStruct((T, WIDTH_A), bf16),
                       jax.ShapeDtypeStruct((T, 2 * LANES), bf16),
                       jax.ShapeDtypeStruct((B, N_KV_A * VT_ROWS, S), bf16),
                       jax.ShapeDtypeStruct((T, N_HEADS_B * LANES), bf16),
                       jax.ShapeDtypeStruct((T, N_HEADS_B * LANES), bf16),
                       jax.ShapeDtypeStruct((B, nkb, N_HEADS_B * VT_ROWS, PROJ_SUB), bf16)],
            compiler_params=pltpu.CompilerParams(dimension_semantics=("arbitrary",),
                                                 vmem_limit_bytes=VMEM_LIMIT),
            name="proj",
        )(x2, positions.reshape(T // tm, 1, tm), row(pre_norm_mix[l]), wit, wit, wit, w_small_t,
          w_va_t, row(q_a_norm[l]), wq_main, wq_swap, row(kv_a_norm[l]), wkv_k, wkv_vt,
          freq_col)

        tq = SWA_TQ
        nb_per = tq // BLOCK
        qa3 = qa.reshape(B, S, WIDTH_A)
        ka3 = ka_x.reshape(B, S, 2 * LANES)
        pos_r3 = positions.reshape(B, 1, S)
        cur = lambda w: pl.BlockSpec((1, tq, w), lambda b, i: (b, i, 0))
        prev_blk = lambda b, i: jnp.maximum(i * nb_per - 1, 0)
        out_a = pl.pallas_call(
            _swa_kernel,
            grid=(B, S // tq),
            in_specs=[pl.BlockSpec(memory_space=pltpu.SMEM),
                      cur(WIDTH_A), cur(2 * LANES),
                      pl.BlockSpec((1, BLOCK, 2 * LANES), lambda b, i: (b, prev_blk(b, i), 0)),
                      pl.BlockSpec((1, N_KV_A * VT_ROWS, tq), lambda b, i: (b, 0, i)),
                      pl.BlockSpec((1, N_KV_A * VT_ROWS, BLOCK), lambda b, i: (b, 0, prev_blk(b, i))),
                      pl.BlockSpec((1, 1, tq), lambda b, i: (b, 0, i)),
                      pl.BlockSpec((1, 1, BLOCK), lambda b, i: (b, 0, prev_blk(b, i)))],
            out_specs=cur(WIDTH_A),
            out_shape=jax.ShapeDtypeStruct((B, S, WIDTH_A), bf16),
            compiler_params=pltpu.CompilerParams(dimension_semantics=("arbitrary", "arbitrary"),
                                                 vmem_limit_bytes=VMEM_LIMIT),
            name="swa",
        )(sinks[l].astype(f32), qa3, ka3, ka3, va_t3, va_t3, pos_r3, pos_r3)

        bq, nh = MLA_BQ, MLA_HEADS
        qb3 = qb.reshape(B, S, N_HEADS_B * LANES)
        kb3 = kb.reshape(B, S, N_HEADS_B * LANES)
        out_b = pl.pallas_call(
            _mla_kernel,
            grid=(B, N_HEADS_B // nh, S // bq),
            in_specs=[pl.BlockSpec((1, bq, nh * LANES), lambda b, j, i: (b, i, j)),
                      pl.BlockSpec((1, S, nh * LANES), lambda b, j, i: (b, 0, j)),
                      pl.BlockSpec((1, nkb, nh * VT_ROWS, bq), lambda b, j, i: (b, 0, j, 0))],
            out_specs=pl.BlockSpec((1, bq, nh * V_DIM_B), lambda b, j, i: (b, i, j)),
            out_shape=jax.ShapeDtypeStruct((B, S, WIDTH_B), bf16),
            scratch_shapes=[pltpu.VMEM((2, nh, bq, bq), f32),
                            pltpu.VMEM((2, nh, 1, bq), f32),
                            pltpu.VMEM((nh, VT_ROWS, bq), f32), pltpu.VMEM((nh, 1, bq), f32)],
            compiler_params=pltpu.CompilerParams(
                dimension_semantics=("arbitrary", "arbitrary", "arbitrary"),
                vmem_limit_bytes=VMEM_LIMIT),
            name="mla",
        )(qb3, kb3, vt)

        tm = MERGE_TM
        tok = lambda w: pl.BlockSpec((tm, w), lambda i: (i, 0))
        x2 = pl.pallas_call(
            _merge_kernel,
            grid=(T // tm,),
            in_specs=[tok(D), tok(WIDTH_A), tok(WIDTH_B), _const_spec((1, D)),
                      w_rows(2 * D, IN_GATES), _const_spec((WIDTH_A, D)), _const_spec((WIDTH_B, D)),
                      _const_spec((D, D)), _const_spec((1, D)), _const_spec((1, D)),
                      _const_spec((D, D_FF)), _const_spec((D_FF, D)), _const_spec((1, D))],
            out_specs=tok(D),
            out_shape=jax.ShapeDtypeStruct((T, D), f32),
            compiler_params=pltpu.CompilerParams(dimension_semantics=("arbitrary",),
                                                 vmem_limit_bytes=VMEM_LIMIT),
            name="merge_mlp",
        )(x2, out_a.reshape(T, WIDTH_A), out_b.reshape(T, WIDTH_B), row(pre_norm_mix[l]), wit,
          w_o_a[l].astype(bf16), w_o_b[l].astype(bf16), w_out[l].astype(bf16),
          row(post_norm_mix[l]), row(pre_norm_mlp[l]), w_up[l].astype(bf16),
          w_down[l], row(post_norm_mlp[l]))
        x = x2.reshape(B, S, D)
    return x
```

```python
import jax
import jax.numpy as jnp
from jax import lax
from jax.experimental import pallas as pl
from jax.experimental.pallas import tpu as pltpu

D_MODEL = 1024
N_HEADS_A = 8
N_KV_A = 2
HEAD_DIM_A = 64
WINDOW = 128
BLOCK = 128
N_HEADS_B = 8
QK_NOPE = 64
QK_ROPE = 32
V_DIM_B = 64
Q_LORA = 256
KV_LORA = 128
ROPE_THETA = 10000.0
D_FF = 4 * D_MODEL
EPS = 1e-6

WIDTH_A = N_HEADS_A * HEAD_DIM_A
WIDTH_B = N_HEADS_B * V_DIM_B
Q_HEAD_B = QK_NOPE + QK_ROPE
KV_HEAD_B = QK_NOPE + V_DIM_B

LANES = 128
HALF = LANES // 2
VT_ROWS = LANES
VT_PAD = VT_ROWS - V_DIM_B
SCALE_A = HEAD_DIM_A ** -0.5
SCALE_B = Q_HEAD_B ** -0.5
LOG2_E = 1.4426950408889634
QSCALE_A = SCALE_A * LOG2_E
QSCALE_B = SCALE_B * LOG2_E
ALIBI_SLOPES = tuple(2.0 ** (-8.0 * (h + 1) / N_HEADS_A) for h in range(N_HEADS_A))

IN_GATES = 0
IN_QA = 2 * D_MODEL
IN_KA = IN_QA + WIDTH_A
IN_VA = IN_KA + N_KV_A * HEAD_DIM_A
IN_CQ = IN_VA + N_KV_A * HEAD_DIM_A
IN_CKV = IN_CQ + Q_LORA
IN_KR = IN_CKV + KV_LORA
D_IN = IN_KR + QK_ROPE
SM_KA = 0
SM_KR = SM_KA + 2 * LANES
SM_KRS = SM_KR + LANES
SM_ROWS = SM_KRS + LANES

PROJ_TM = 1024
PROJ_SUB = 512
SWA_TQ = 2048
SWA_LEAD = 4
MLA_BQ = 512
MLA_CHUNK = 512
MLA_HEADS = 4
MLA_LEAD = 1
MERGE_TM = 512
FF_CHUNK = 1024
V7X_VMEM_BYTES = 64 * 1024 * 1024
VMEM_LIMIT = V7X_VMEM_BYTES - 4 * 1024 * 1024


def _rms(v):
    return v * lax.rsqrt(jnp.mean(v * v, axis=-1, keepdims=True) + EPS)


def _dot(a, b):
    return jnp.dot(a, b, preferred_element_type=jnp.float32)


def _dot_nt(a, b):
    return lax.dot_general(a, b, (((1,), (1,)), ((), ())), preferred_element_type=jnp.float32)


def _proj_kernel(x_ref, pos_ref, gpre_ref, wqa_ref, wcq_ref, wckv_ref, wsm_ref, wvat_ref, gq_ref,
                 wqm_ref, wqs_ref, gkv_ref, wkk_ref, wkvt_ref, freq_ref,
                 qa_ref, ka_ref, vat_ref, qb_ref, kb_ref, vt_ref):
    bf16 = jnp.bfloat16
    sub = PROJ_SUB

    def first_stage(n):
        rows = slice(n * sub, (n + 1) * sub)
        hb = (_rms(x_ref[rows, :]) * gpre_ref[...]).astype(bf16)
        qa_ref[rows, :] = (_dot_nt(hb, wqa_ref[...]) * QSCALE_A).astype(bf16)
        small = _dot_nt(hb, wsm_ref[...])
        ka_ref[rows, :] = small[:, SM_KA:SM_KR].astype(bf16)
        va_t = _dot_nt(wvat_ref[...], hb)
        varow = lax.broadcasted_iota(jnp.int32, va_t.shape, 0)
        vat_ref[0, :, rows] = jnp.where(varow % VT_ROWS == HEAD_DIM_A, 1.0, va_t).astype(bf16)
        cq = _dot_nt(hb, wcq_ref[...])
        ckv = _dot_nt(hb, wckv_ref[...])
        pos = pos_ref[0, :, rows].astype(jnp.float32)
        ang = freq_ref[...] * pos
        cos_t, sin_t = jnp.cos(ang), jnp.sin(ang)
        one = jnp.ones((QK_NOPE, sub), jnp.float32)
        zero = jnp.zeros((QK_NOPE, sub), jnp.float32)
        pad = LANES - Q_HEAD_B
        cos = jnp.concatenate([one, cos_t, cos_t, one[:pad]], axis=0).T
        sin = jnp.concatenate([zero, -sin_t, sin_t, zero[:pad]], axis=0).T
        k_rot = small[:, SM_KR:SM_KRS] * cos + small[:, SM_KRS:SM_ROWS] * sin
        return cq, ckv, cos, sin, k_rot

    def second_stage(n, cq, ckv, cos, sin, k_rot):
        rows = slice(n * sub, (n + 1) * sub)
        cqn = (_rms(cq) * gq_ref[...]).astype(bf16)
        q_main = _dot(cqn, wqm_ref[...])
        q_swap = _dot(cqn, wqs_ref[...])
        ckvn = (_rms(ckv) * gkv_ref[...]).astype(bf16)
        k_nope = _dot(ckvn, wkk_ref[...])
        v_t = _dot_nt(wkvt_ref[...], ckvn)
        tail_row = lax.broadcasted_iota(jnp.int32, (VT_PAD, sub), 0)
        tail = jnp.where(tail_row == 0, 1.0, 0.0).astype(bf16)
        for h in range(N_HEADS_B):
            vt_ref[0, n, h * VT_ROWS:h * VT_ROWS + V_DIM_B, :] = (
                v_t[h * V_DIM_B:(h + 1) * V_DIM_B].astype(bf16))
            vt_ref[0, n, h * VT_ROWS + V_DIM_B:(h + 1) * VT_ROWS, :] = tail
        for h in range(N_HEADS_B):
            sl = slice(h * LANES, (h + 1) * LANES)
            qb_ref[rows, sl] = ((q_main[:, sl] * cos + q_swap[:, sl] * sin) * QSCALE_B).astype(bf16)
            kb_ref[rows, sl] = (k_nope[:, sl] + k_rot).astype(bf16)

    n_sub = x_ref.shape[0] // sub
    staged = [first_stage(n) for n in range(n_sub)]
    for n in range(n_sub):
        second_stage(n, *staged[n])


def _swa_kernel(sink_ref, q_ref, kc_ref, kp_ref, vtc_ref, vtp_ref, posc_ref, posp_ref, o_ref):
    bf16 = jnp.bfloat16
    i = pl.program_id(1)
    krow = lax.broadcasted_iota(jnp.int32, (2 * BLOCK, BLOCK), 0)
    qcol = lax.broadcasted_iota(jnp.int32, (2 * BLOCK, BLOCK), 1)
    ahead = krow - qcol
    band = (ahead > 0) & (ahead <= WINDOW)
    lane = lax.broadcasted_iota(jnp.int32, (BLOCK, LANES), 1)
    low_half = lane < HALF
    pos_inf = jnp.float32(jnp.inf)

    n_pairs = N_HEADS_A // 2
    chains = [(blk, pair) for blk in range(SWA_TQ // BLOCK) for pair in range(n_pairs)]
    bands = {}

    def band_of(blk):
        if blk not in bands:
            r0 = blk * BLOCK
            if blk == 0:
                kband = jnp.concatenate([kp_ref[0], kc_ref[0, 0:BLOCK, :]], axis=0)
                vtband = jnp.concatenate([vtp_ref[0], vtc_ref[0, :, 0:BLOCK]], axis=1)
                kpos = jnp.concatenate([posp_ref[0], posc_ref[0, :, 0:BLOCK]], axis=1)
                mask = band & ((krow >= BLOCK) | (i > 0))
            else:
                kband = kc_ref[0, r0 - BLOCK:r0 + BLOCK, :]
                vtband = vtc_ref[0, :, r0 - BLOCK:r0 + BLOCK]
                kpos = posc_ref[0, :, r0 - BLOCK:r0 + BLOCK]
                mask = band
            qpos = posc_ref[0, :, r0:r0 + BLOCK]
            kpos_col = jnp.broadcast_to(kpos, (BLOCK, 2 * BLOCK)).T
            dist = jnp.abs(kpos_col - qpos).astype(jnp.float32) * LOG2_E
            dist = jnp.where(mask, dist, pos_inf)
            bands[blk] = (kband, vtband, dist)
        return bands[blk]

    def scores(blk, pair):
        kv = (2 * pair) // (N_HEADS_A // N_KV_A)
        r0 = blk * BLOCK
        qp = q_ref[0, r0:r0 + BLOCK, pair * LANES:(pair + 1) * LANES]
        zero = jnp.zeros_like(qp)
        q2 = jnp.concatenate([jnp.where(low_half, qp, zero), jnp.where(low_half, zero, qp)],
                             axis=0)
        kx = band_of(blk)[0][:, kv * LANES:(kv + 1) * LANES]
        return _dot_nt(kx, q2)

    def finish(blk, pair, s2):
        kv = (2 * pair) // (N_HEADS_A // N_KV_A)
        r0 = blk * BLOCK
        _, vtband, dist = band_of(blk)
        vt = vtband[kv * VT_ROWS:(kv + 1) * VT_ROWS, :]
        ps, ms = [], []
        for e in range(2):
            h = 2 * pair + e
            s = s2[:, e * BLOCK:(e + 1) * BLOCK] - ALIBI_SLOPES[h] * dist
            m = jnp.maximum(jnp.max(s, axis=0, keepdims=True), sink_ref[h] * LOG2_E)
            ps.append(jnp.exp2(s - m).astype(bf16))
            ms.append(m)
        o2 = _dot(vt, jnp.concatenate(ps, axis=1))
        outs = []
        for e in range(2):
            h = 2 * pair + e
            o_t = o2[:, e * BLOCK:(e + 1) * BLOCK]
            denom = (o_t[HEAD_DIM_A:HEAD_DIM_A + 1]
                     + jnp.exp2(sink_ref[h] * LOG2_E - ms[e]))
            outs.append(o_t[0:HEAD_DIM_A] / denom)
        o_ref[0, r0:r0 + BLOCK, pair * LANES:(pair + 1) * LANES] = (
            jnp.concatenate(outs, axis=0).T.astype(bf16))

    pending = [scores(*c) for c in chains[:SWA_LEAD]]
    for n, c in enumerate(chains):
        s2 = pending.pop(0)
        if n + SWA_LEAD < len(chains):
            pending.append(scores(*chains[n + SWA_LEAD]))
        finish(*c, s2)


def _mla_kernel(q_ref, k_ref, vt_ref, o_ref, s_ref, bmax_ref, acc_ref, m_ref):
    bf16 = jnp.bfloat16
    bq = MLA_BQ
    ch = MLA_CHUNK
    qi = pl.program_id(2)
    krow = lax.broadcasted_iota(jnp.int32, (bq, ch), 0)
    qcol = lax.broadcasted_iota(jnp.int32, (bq, ch), 1)
    neg_inf = jnp.float32(-jnp.inf)
    m_ref[...] = jnp.full(m_ref.shape, neg_inf, jnp.float32)
    acc_ref[...] = jnp.zeros(acc_ref.shape, jnp.float32)

    def scores_head(buf, kb, e):
        start = pl.multiple_of(kb * bq, bq)
        sl = slice(e * LANES, (e + 1) * LANES)
        s = _dot_nt(k_ref[0, pl.ds(start, bq), sl], q_ref[0, :, sl])
        s_ref[buf, e] = s
        bmax_ref[buf, e] = jnp.max(s, axis=0, keepdims=True)

    def scores_into(buf, kb):
        for e in range(MLA_HEADS):
            scores_head(buf, kb, e)

    def softmax_head(buf, e, masked):
        out = []
        for c in range(bq // ch):
            cs = slice(c * ch, (c + 1) * ch)
            s = s_ref[buf, e, :, cs]
            if masked:
                s = jnp.where(krow <= qcol + c * ch, s, neg_inf)
                blockmax = jnp.max(s, axis=0, keepdims=True)
            else:
                blockmax = bmax_ref[buf, e, :, cs]
            m_prev = m_ref[e, :, cs]
            m_new = jnp.maximum(m_prev, blockmax)
            m_ref[e, :, cs] = m_new
            out.append((jnp.exp2(s - m_new).astype(bf16), jnp.exp2(m_prev - m_new)))
        return out

    def pv_head(kb, e, weights):
        vt = vt_ref[0, kb, e * VT_ROWS:(e + 1) * VT_ROWS, :]
        for c, (p, alpha) in enumerate(weights):
            cs = slice(c * ch, (c + 1) * ch)
            acc_ref[e, :, cs] = alpha * acc_ref[e, :, cs] + _dot(vt, p)

    def softmax_pv(buf, kb, masked):
        for e in range(MLA_HEADS):
            pv_head(kb, e, softmax_head(buf, e, masked))

    def pipelined(first_kb, n_blocks):
        blocks = [(j % 2, first_kb + j) for j in range(n_blocks)]
        units = [(buf, kb, e) for buf, kb in blocks for e in range(MLA_HEADS)]
        prods = [(1 - buf, kb + 1, e) for buf, kb in blocks for e in range(MLA_HEADS)]
        for n in range(min(MLA_LEAD, len(prods))):
            scores_head(*prods[n])
        for n, (buf, kb, e) in enumerate(units):
            pv_head(kb, e, softmax_head(buf, e, False))
            if n + MLA_LEAD < len(prods):
                scores_head(*prods[n + MLA_LEAD])

    def quad_body(t, carry):
        pipelined(4 * t, 4)
        return carry

    scores_into(0, 0)
    lax.fori_loop(0, qi // 4, quad_body, 0)

    @pl.when(qi % 4 >= 2)
    def _():
        pipelined((qi // 4) * 4, 2)

    @pl.when(qi % 2 == 1)
    def _():
        pipelined(qi - 1, 1)
        softmax_pv(1, qi, True)

    @pl.when(qi % 2 == 0)
    def _():
        softmax_pv(0, qi, True)

    outs = []
    for e in range(MLA_HEADS):
        acc = acc_ref[e]
        outs.append(acc[0:V_DIM_B] / acc[V_DIM_B:V_DIM_B + 1])
    o_ref[0] = jnp.concatenate(outs, axis=0).T.astype(bf16)


def _merge_kernel(x_ref, oa_ref, ob_ref, gpre_ref, wg_ref, woa_ref, wob_ref, wout_ref, gpost_ref,
                  gpre2_ref, wup_ref, wdn_ref, gpost2_ref, o_ref):
    bf16 = jnp.bfloat16
    x = x_ref[...]
    hm = x.shape[0] // 2
    halves = (slice(0, hm), slice(hm, 2 * hm))
    hb = [(_rms(x[r]) * gpre_ref[...]).astype(bf16) for r in halves]
    gate_a = jnp.concatenate([_dot_nt(h, wg_ref[0:D_MODEL, :]) for h in hb], axis=0)
    hb = jnp.concatenate(hb, axis=0)
    gate_a = jax.nn.sigmoid(gate_a)
    gate_b = jax.nn.sigmoid(_dot_nt(hb, wg_ref[D_MODEL:2 * D_MODEL, :]))
    merged = gate_a * _dot(oa_ref[...], woa_ref[...]) + gate_b * _dot(ob_ref[...], wob_ref[...])
    mb = merged.astype(bf16)

    n_chunks = D_FF // FF_CHUNK
    chunk = lambda c: slice(c * FF_CHUNK, (c + 1) * FF_CHUNK)

    def w_up(c):
        return wup_ref[:, chunk(c)].astype(bf16)

    def w_down(c):
        return wdn_ref[chunk(c), :].astype(bf16)

    def sq_relu(v):
        v = jnp.maximum(v, 0.0)
        return (v * v).astype(bf16)

    x1, h2, up0 = [], [], []
    for r in halves:
        y = _dot(mb[r], wout_ref[...])
        x1.append(x[r] + _rms(y) * gpost_ref[...])
        h2.append((_rms(x1[-1]) * gpre2_ref[...]).astype(bf16))
    w_first = w_up(0)
    for i in range(2):
        up0.append(sq_relu(_dot(h2[i], w_first)))
    h2 = jnp.concatenate(h2, axis=0)
    y2 = _dot(jnp.concatenate(up0, axis=0), w_down(0))
    for c in range(1, n_chunks - 1):
        y2 = y2 + _dot(sq_relu(_dot(h2, w_up(c))), w_down(c))
    last = sq_relu(_dot(h2, w_up(n_chunks - 1)))
    w_last = w_down(n_chunks - 1)
    for i, r in enumerate(halves):
        y2_half = y2[r] + _dot(last[r], w_last)
        o_ref[r, :] = x1[i] + _rms(y2_half) * gpost2_ref[...]


def _const_spec(shape):
    return pl.BlockSpec(shape, lambda *_: (0,) * len(shape), pipeline_mode=pl.Buffered(1))


def kernel(x, positions, pre_norm_mix, w_in, q_a_norm, w_q_b, kv_a_norm, w_kv_b, sinks, w_o_a,
           w_o_b, w_out, post_norm_mix, pre_norm_mlp, w_up, w_down, post_norm_mlp):
    f32, bf16 = jnp.float32, jnp.bfloat16
    B, S, D = x.shape
    T = B * S
    depth = w_in.shape[0]
    for l in range(depth):
        assert w_in.shape[2] == D_IN
        wit = jnp.swapaxes(w_in[l], 0, 1).astype(bf16)
        ka_t = wit[IN_KA:IN_VA]
        va_t = wit[IN_VA:IN_CQ]
        kr_t = wit[IN_KR:D_IN]
        hr = QK_ROPE // 2
        hd = HEAD_DIM_A
        z = lambda n: jnp.zeros((n, D_MODEL), bf16)
        w_small_t = jnp.concatenate([
            ka_t[:hd], ka_t[:hd], ka_t[hd:], ka_t[hd:],
            z(QK_NOPE), kr_t, z(LANES - Q_HEAD_B),
            z(QK_NOPE), kr_t[hr:], kr_t[:hr], z(LANES - Q_HEAD_B)], axis=0)
        w_va_t = jnp.concatenate([va_t[:hd], z(VT_PAD), va_t[hd:], z(VT_PAD)], axis=0)

        wq = w_q_b[l].reshape(Q_LORA, N_HEADS_B, Q_HEAD_B)
        q_nope, q_rope = wq[..., :QK_NOPE], wq[..., QK_NOPE:]
        zq = lambda n: jnp.zeros((Q_LORA, N_HEADS_B, n), f32)
        wq_main = jnp.concatenate([q_nope, q_rope, zq(LANES - Q_HEAD_B)], -1)
        wq_swap = jnp.concatenate([zq(QK_NOPE), q_rope[..., hr:], q_rope[..., :hr],
                                   zq(LANES - Q_HEAD_B)], -1)
        wq_main = wq_main.reshape(Q_LORA, N_HEADS_B * LANES).astype(bf16)
        wq_swap = wq_swap.reshape(Q_LORA, N_HEADS_B * LANES).astype(bf16)

        wkv = w_kv_b[l].reshape(KV_LORA, N_HEADS_B, KV_HEAD_B)
        kv_k, kv_v = wkv[..., :QK_NOPE], wkv[..., QK_NOPE:]
        zk = jnp.zeros((KV_LORA, N_HEADS_B, HALF), f32)
        wkv_k = jnp.concatenate([kv_k, zk], -1).reshape(KV_LORA, N_HEADS_B * LANES).astype(bf16)
        wkv_vt = kv_v.reshape(KV_LORA, WIDTH_B).T.astype(bf16)

        freq_col = (ROPE_THETA ** (-jnp.arange(0, QK_ROPE, 2, dtype=f32) / QK_ROPE))[:, None]

        row = lambda g: g.reshape(1, -1).astype(f32)
        x2 = x.reshape(T, D)

        tm = PROJ_TM
        assert PROJ_SUB == MLA_BQ and tm % PROJ_SUB == 0 and S % tm == 0
        nkb = S // MLA_BQ
        n_sub = tm // PROJ_SUB
        npb = S // tm
        tok = lambda w: pl.BlockSpec((tm, w), lambda i: (i, 0))
        w_rows = lambda n, start: pl.BlockSpec((n, D), lambda i: (start // n, 0),
                                               pipeline_mode=pl.Buffered(1))
        assert IN_QA % WIDTH_A == 0 and IN_CQ % Q_LORA == 0 and IN_CKV % KV_LORA == 0
        qa, ka_x, va_t3, qb, kb, vt = pl.pallas_call(
            _proj_kernel,
            grid=(T // tm,),
            in_specs=[tok(D), pl.BlockSpec((1, 1, tm), lambda i: (i, 0, 0)), _const_spec((1, D)),
                      w_rows(WIDTH_A, IN_QA), w_rows(Q_LORA, IN_CQ), w_rows(KV_LORA, IN_CKV),
                      _const_spec((SM_ROWS, D)), _const_spec((N_KV_A * VT_ROWS, D)),
                      _const_spec((1, Q_LORA)), _const_spec((Q_LORA, N_HEADS_B * LANES)),
                      _const_spec((Q_LORA, N_HEADS_B * LANES)), _const_spec((1, KV_LORA)),
                      _const_spec((KV_LORA, N_HEADS_B * LANES)),
                      _const_spec((WIDTH_B, KV_LORA)),
                      _const_spec((QK_ROPE // 2, 1))],
            out_specs=[tok(WIDTH_A), tok(2 * LANES),
                       pl.BlockSpec((1, N_KV_A * VT_ROWS, tm), lambda i: (i // npb, 0, i % npb)),
                       tok(N_HEADS_B * LANES), tok(N_HEADS_B * LANES),
                       pl.BlockSpec((1, n_sub, N_HEADS_B * VT_ROWS, PROJ_SUB),
                                    lambda i: (i // npb, i % npb, 0, 0))],
            out_shape=[jax.ShapeDtypeStruct((T, WIDTH_A), bf16),
                       jax.ShapeDtypeStruct((T, 2 * LANES), bf16),
                       jax.ShapeDtypeStruct((B, N_KV_A * VT_ROWS, S), bf16),
                       jax.ShapeDtypeStruct((T, N_HEADS_B * LANES), bf16),
                       jax.ShapeDtypeStruct((T, N_HEADS_B * LANES), bf16),
                       jax.ShapeDtypeStruct((B, nkb, N_HEADS_B * VT_ROWS, PROJ_SUB), bf16)],
            compiler_params=pltpu.CompilerParams(dimension_semantics=("arbitrary",),
                                                 vmem_limit_bytes=VMEM_LIMIT),
            name="proj",
        )(x2, positions.reshape(T // tm, 1, tm), row(pre_norm_mix[l]), wit, wit, wit, w_small_t,
          w_va_t, row(q_a_norm[l]), wq_main, wq_swap, row(kv_a_norm[l]), wkv_k, wkv_vt,
          freq_col)

        tq = SWA_TQ
        nb_per = tq // BLOCK
        qa3 = qa.reshape(B, S, WIDTH_A)
        ka3 = ka_x.reshape(B, S, 2 * LANES)
        pos_r3 = positions.reshape(B, 1, S)
        cur = lambda w: pl.BlockSpec((1, tq, w), lambda b, i: (b, i, 0))
        prev_blk = lambda b, i: jnp.maximum(i * nb_per - 1, 0)
        out_a = pl.pallas_call(
            _swa_kernel,
            grid=(B, S // tq),
            in_specs=[pl.BlockSpec(memory_space=pltpu.SMEM),
                      cur(WIDTH_A), cur(2 * LANES),
                      pl.BlockSpec((1, BLOCK, 2 * LANES), lambda b, i: (b, prev_blk(b, i), 0)),
                      pl.BlockSpec((1, N_KV_A * VT_ROWS, tq), lambda b, i: (b, 0, i)),
                      pl.BlockSpec((1, N_KV_A * VT_ROWS, BLOCK), lambda b, i: (b, 0, prev_blk(b, i))),
                      pl.BlockSpec((1, 1, tq), lambda b, i: (b, 0, i)),
                      pl.BlockSpec((1, 1, BLOCK), lambda b, i: (b, 0, prev_blk(b, i)))],
            out_specs=cur(WIDTH_A),
            out_shape=jax.ShapeDtypeStruct((B, S, WIDTH_A), bf16),
            compiler_params=pltpu.CompilerParams(dimension_semantics=("arbitrary", "arbitrary"),
                                                 vmem_limit_bytes=VMEM_LIMIT),
            name="swa",
        )(sinks[l].astype(f32), qa3, ka3, ka3, va_t3, va_t3, pos_r3, pos_r3)

        bq, nh = MLA_BQ, MLA_HEADS
        qb3 = qb.reshape(B, S, N_HEADS_B * LANES)
        kb3 = kb.reshape(B, S, N_HEADS_B * LANES)
        out_b = pl.pallas_call(
            _mla_kernel,
            grid=(B, N_HEADS_B // nh, S // bq),
            in_specs=[pl.BlockSpec((1, bq, nh * LANES), lambda b, j, i: (b, i, j)),
                      pl.BlockSpec((1, S, nh * LANES), lambda b, j, i: (b, 0, j)),
                      pl.BlockSpec((1, nkb, nh * VT_ROWS, bq), lambda b, j, i: (b, 0, j, 0))],
            out_specs=pl.BlockSpec((1, bq, nh * V_DIM_B), lambda b, j, i: (b, i, j)),
            out_shape=jax.ShapeDtypeStruct((B, S, WIDTH_B), bf16),
            scratch_shapes=[pltpu.VMEM((2, nh, bq, bq), f32),
                            pltpu.VMEM((2, nh, 1, bq), f32),
                            pltpu.VMEM((nh, VT_ROWS, bq), f32), pltpu.VMEM((nh, 1, bq), f32)],
            compiler_params=pltpu.CompilerParams(
                dimension_semantics=("arbitrary", "arbitrary", "arbitrary"),
                vmem_limit_bytes=VMEM_LIMIT),
            name="mla",
        )(qb3, kb3, vt)

        tm = MERGE_TM
        tok = lambda w: pl.BlockSpec((tm, w), lambda i: (i, 0))
        x2 = pl.pallas_call(
            _merge_kernel,
            grid=(T // tm,),
            in_specs=[tok(D), tok(WIDTH_A), tok(WIDTH_B), _const_spec((1, D)),
                      w_rows(2 * D, IN_GATES), _const_spec((WIDTH_A, D)), _const_spec((WIDTH_B, D)),
                      _const_spec((D, D)), _const_spec((1, D)), _const_spec((1, D)),
                      _const_spec((D, D_FF)), _const_spec((D_FF, D)), _const_spec((1, D))],
            out_specs=tok(D),
            out_shape=jax.ShapeDtypeStruct((T, D), f32),
            compiler_params=pltpu.CompilerParams(dimension_semantics=("arbitrary",),
                                                 vmem_limit_bytes=VMEM_LIMIT),
            name="merge_mlp",
        )(x2, out_a.reshape(T, WIDTH_A), out_b.reshape(T, WIDTH_B), row(pre_norm_mix[l]), wit,
          w_o_a[l].astype(bf16), w_o_b[l].astype(bf16), w_out[l].astype(bf16),
          row(post_norm_mix[l]), row(pre_norm_mlp[l]), w_up[l],
          w_down[l], row(post_norm_mlp[l]))
        x = x2.reshape(B, S, D)
    return x
```

```python
import jax
import jax.numpy as jnp
from jax import lax
from jax.experimental import pallas as pl
from jax.experimental.pallas import tpu as pltpu

D_MODEL = 1024
N_HEADS_A = 8
N_KV_A = 2
HEAD_DIM_A = 64
WINDOW = 128
BLOCK = 128
N_HEADS_B = 8
QK_NOPE = 64
QK_ROPE = 32
V_DIM_B = 64
Q_LORA = 256
KV_LORA = 128
ROPE_THETA = 10000.0
D_FF = 4 * D_MODEL
EPS = 1e-6

WIDTH_A = N_HEADS_A * HEAD_DIM_A
WIDTH_B = N_HEADS_B * V_DIM_B
Q_HEAD_B = QK_NOPE + QK_ROPE
KV_HEAD_B = QK_NOPE + V_DIM_B

LANES = 128
HALF = LANES // 2
VT_ROWS = LANES
VT_PAD = VT_ROWS - V_DIM_B
SCALE_A = HEAD_DIM_A ** -0.5
SCALE_B = Q_HEAD_B ** -0.5
LOG2_E = 1.4426950408889634
QSCALE_A = SCALE_A * LOG2_E
QSCALE_B = SCALE_B * LOG2_E
ALIBI_SLOPES = tuple(2.0 ** (-8.0 * (h + 1) / N_HEADS_A) for h in range(N_HEADS_A))

IN_GATES = 0
IN_QA = 2 * D_MODEL
IN_KA = IN_QA + WIDTH_A
IN_VA = IN_KA + N_KV_A * HEAD_DIM_A
IN_CQ = IN_VA + N_KV_A * HEAD_DIM_A
IN_CKV = IN_CQ + Q_LORA
IN_KR = IN_CKV + KV_LORA
D_IN = IN_KR + QK_ROPE
SM_KA = 0
SM_KR = SM_KA + 2 * LANES
SM_KRS = SM_KR + LANES
SM_ROWS = SM_KRS + LANES

PROJ_TM = 1024
PROJ_SUB = 512
SWA_TQ = 2048
SWA_LEAD = 4
MLA_BQ = 512
MLA_CHUNK = 512
MLA_HEADS = 4
MLA_LEAD = 1
MERGE_TM = 512
FF_CHUNK = 1024
V7X_VMEM_BYTES = 64 * 1024 * 1024
VMEM_LIMIT = V7X_VMEM_BYTES - 4 * 1024 * 1024


def _rms(v):
    return v * lax.rsqrt(jnp.mean(v * v, axis=-1, keepdims=True) + EPS)


def _dot(a, b):
    return jnp.dot(a, b, preferred_element_type=jnp.float32)


def _dot_nt(a, b):
    return lax.dot_general(a, b, (((1,), (1,)), ((), ())), preferred_element_type=jnp.float32)


def _proj_kernel(x_ref, pos_ref, gpre_ref, wqa_ref, wcq_ref, wckv_ref, wsm_ref, wvat_ref, gq_ref,
                 wqm_ref, wqs_ref, gkv_ref, wkk_ref, wkvt_ref, freq_ref,
                 qa_ref, ka_ref, vat_ref, qb_ref, kb_ref, vt_ref):
    bf16 = jnp.bfloat16
    sub = PROJ_SUB

    def first_stage(n):
        rows = slice(n * sub, (n + 1) * sub)
        hb = (_rms(x_ref[rows, :]) * gpre_ref[...]).astype(bf16)
        qa_ref[rows, :] = (_dot_nt(hb, wqa_ref[...].astype(bf16)) * QSCALE_A).astype(bf16)
        small = _dot_nt(hb, wsm_ref[...])
        ka_ref[rows, :] = small[:, SM_KA:SM_KR].astype(bf16)
        va_t = _dot_nt(wvat_ref[...], hb)
        varow = lax.broadcasted_iota(jnp.int32, va_t.shape, 0)
        vat_ref[0, :, rows] = jnp.where(varow % VT_ROWS == HEAD_DIM_A, 1.0, va_t).astype(bf16)
        cq = _dot_nt(hb, wcq_ref[...].astype(bf16))
        ckv = _dot_nt(hb, wckv_ref[...].astype(bf16))
        pos = pos_ref[0, :, rows].astype(jnp.float32)
        ang = freq_ref[...] * pos
        cos_t, sin_t = jnp.cos(ang), jnp.sin(ang)
        one = jnp.ones((QK_NOPE, sub), jnp.float32)
        zero = jnp.zeros((QK_NOPE, sub), jnp.float32)
        pad = LANES - Q_HEAD_B
        cos = jnp.concatenate([one, cos_t, cos_t, one[:pad]], axis=0).T
        sin = jnp.concatenate([zero, -sin_t, sin_t, zero[:pad]], axis=0).T
        k_rot = small[:, SM_KR:SM_KRS] * cos + small[:, SM_KRS:SM_ROWS] * sin
        return cq, ckv, cos, sin, k_rot

    def second_stage(n, cq, ckv, cos, sin, k_rot):
        rows = slice(n * sub, (n + 1) * sub)
        cqn = (_rms(cq) * gq_ref[...]).astype(bf16)
        q_main = _dot(cqn, wqm_ref[...])
        q_swap = _dot(cqn, wqs_ref[...])
        ckvn = (_rms(ckv) * gkv_ref[...]).astype(bf16)
        k_nope = _dot(ckvn, wkk_ref[...])
        v_t = _dot_nt(wkvt_ref[...], ckvn)
        tail_row = lax.broadcasted_iota(jnp.int32, (VT_PAD, sub), 0)
        tail = jnp.where(tail_row == 0, 1.0, 0.0).astype(bf16)
        for h in range(N_HEADS_B):
            vt_ref[0, n, h * VT_ROWS:h * VT_ROWS + V_DIM_B, :] = (
                v_t[h * V_DIM_B:(h + 1) * V_DIM_B].astype(bf16))
            vt_ref[0, n, h * VT_ROWS + V_DIM_B:(h + 1) * VT_ROWS, :] = tail
        for h in range(N_HEADS_B):
            sl = slice(h * LANES, (h + 1) * LANES)
            qb_ref[rows, sl] = ((q_main[:, sl] * cos + q_swap[:, sl] * sin) * QSCALE_B).astype(bf16)
            kb_ref[rows, sl] = (k_nope[:, sl] + k_rot).astype(bf16)

    n_sub = x_ref.shape[0] // sub
    staged = [first_stage(n) for n in range(n_sub)]
    for n in range(n_sub):
        second_stage(n, *staged[n])


def _swa_kernel(sink_ref, q_ref, kc_ref, kp_ref, vtc_ref, vtp_ref, posc_ref, posp_ref, o_ref):
    bf16 = jnp.bfloat16
    i = pl.program_id(1)
    krow = lax.broadcasted_iota(jnp.int32, (2 * BLOCK, BLOCK), 0)
    qcol = lax.broadcasted_iota(jnp.int32, (2 * BLOCK, BLOCK), 1)
    ahead = krow - qcol
    band = (ahead > 0) & (ahead <= WINDOW)
    lane = lax.broadcasted_iota(jnp.int32, (BLOCK, LANES), 1)
    low_half = lane < HALF
    pos_inf = jnp.float32(jnp.inf)

    n_pairs = N_HEADS_A // 2
    chains = [(blk, pair) for blk in range(SWA_TQ // BLOCK) for pair in range(n_pairs)]
    bands = {}

    def band_of(blk):
        if blk not in bands:
            r0 = blk * BLOCK
            if blk == 0:
                kband = jnp.concatenate([kp_ref[0], kc_ref[0, 0:BLOCK, :]], axis=0)
                vtband = jnp.concatenate([vtp_ref[0], vtc_ref[0, :, 0:BLOCK]], axis=1)
                kpos = jnp.concatenate([posp_ref[0], posc_ref[0, :, 0:BLOCK]], axis=1)
                mask = band & ((krow >= BLOCK) | (i > 0))
            else:
                kband = kc_ref[0, r0 - BLOCK:r0 + BLOCK, :]
                vtband = vtc_ref[0, :, r0 - BLOCK:r0 + BLOCK]
                kpos = posc_ref[0, :, r0 - BLOCK:r0 + BLOCK]
                mask = band
            qpos = posc_ref[0, :, r0:r0 + BLOCK]
            kpos_col = jnp.broadcast_to(kpos, (BLOCK, 2 * BLOCK)).T
            dist = jnp.abs(kpos_col - qpos).astype(jnp.float32) * LOG2_E
            dist = jnp.where(mask, dist, pos_inf)
            bands[blk] = (kband, vtband, dist)
        return bands[blk]

    def scores(blk, pair):
        kv = (2 * pair) // (N_HEADS_A // N_KV_A)
        r0 = blk * BLOCK
        qp = q_ref[0, r0:r0 + BLOCK, pair * LANES:(pair + 1) * LANES]
        zero = jnp.zeros_like(qp)
        q2 = jnp.concatenate([jnp.where(low_half, qp, zero), jnp.where(low_half, zero, qp)],
                             axis=0)
        kx = band_of(blk)[0][:, kv * LANES:(kv + 1) * LANES]
        return _dot_nt(kx, q2)

    def finish(blk, pair, s2):
        kv = (2 * pair) // (N_HEADS_A // N_KV_A)
        r0 = blk * BLOCK
        _, vtband, dist = band_of(blk)
        vt = vtband[kv * VT_ROWS:(kv + 1) * VT_ROWS, :]
        ps, ms = [], []
        for e in range(2):
            h = 2 * pair + e
            s = s2[:, e * BLOCK:(e + 1) * BLOCK] - ALIBI_SLOPES[h] * dist
            m = jnp.maximum(jnp.max(s, axis=0, keepdims=True), sink_ref[h] * LOG2_E)
            ps.append(jnp.exp2(s - m).astype(bf16))
            ms.append(m)
        o2 = _dot(vt, jnp.concatenate(ps, axis=1))
        outs = []
        for e in range(2):
            h = 2 * pair + e
            o_t = o2[:, e * BLOCK:(e + 1) * BLOCK]
            denom = (o_t[HEAD_DIM_A:HEAD_DIM_A + 1]
                     + jnp.exp2(sink_ref[h] * LOG2_E - ms[e]))
            outs.append(o_t[0:HEAD_DIM_A] / denom)
        o_ref[0, r0:r0 + BLOCK, pair * LANES:(pair + 1) * LANES] = (
            jnp.concatenate(outs, axis=0).T.astype(bf16))

    pending = [scores(*c) for c in chains[:SWA_LEAD]]
    for n, c in enumerate(chains):
        s2 = pending.pop(0)
        if n + SWA_LEAD < len(chains):
            pending.append(scores(*chains[n + SWA_LEAD]))
        finish(*c, s2)


def _mla_kernel(q_ref, k_ref, vt_ref, o_ref, s_ref, bmax_ref, acc_ref, m_ref):
    bf16 = jnp.bfloat16
    bq = MLA_BQ
    ch = MLA_CHUNK
    qi = pl.program_id(2)
    krow = lax.broadcasted_iota(jnp.int32, (bq, ch), 0)
    qcol = lax.broadcasted_iota(jnp.int32, (bq, ch), 1)
    neg_inf = jnp.float32(-jnp.inf)
    m_ref[...] = jnp.full(m_ref.shape, neg_inf, jnp.float32)
    acc_ref[...] = jnp.zeros(acc_ref.shape, jnp.float32)

    def scores_head(buf, kb, e):
        start = pl.multiple_of(kb * bq, bq)
        sl = slice(e * LANES, (e + 1) * LANES)
        s = _dot_nt(k_ref[0, pl.ds(start, bq), sl], q_ref[0, :, sl])
        s_ref[buf, e] = s
        bmax_ref[buf, e] = jnp.max(s, axis=0, keepdims=True)

    def scores_into(buf, kb):
        for e in range(MLA_HEADS):
            scores_head(buf, kb, e)

    def softmax_head(buf, e, masked):
        out = []
        for c in range(bq // ch):
            cs = slice(c * ch, (c + 1) * ch)
            s = s_ref[buf, e, :, cs]
            if masked:
                s = jnp.where(krow <= qcol + c * ch, s, neg_inf)
                blockmax = jnp.max(s, axis=0, keepdims=True)
            else:
                blockmax = bmax_ref[buf, e, :, cs]
            m_prev = m_ref[e, :, cs]
            m_new = jnp.maximum(m_prev, blockmax)
            m_ref[e, :, cs] = m_new
            out.append((jnp.exp2(s - m_new).astype(bf16), jnp.exp2(m_prev - m_new)))
        return out

    def pv_head(kb, e, weights):
        vt = vt_ref[0, kb, e * VT_ROWS:(e + 1) * VT_ROWS, :]
        for c, (p, alpha) in enumerate(weights):
            cs = slice(c * ch, (c + 1) * ch)
            acc_ref[e, :, cs] = alpha * acc_ref[e, :, cs] + _dot(vt, p)

    def softmax_pv(buf, kb, masked):
        for e in range(MLA_HEADS):
            pv_head(kb, e, softmax_head(buf, e, masked))

    def pipelined(first_kb, n_blocks):
        blocks = [(j % 2, first_kb + j) for j in range(n_blocks)]
        units = [(buf, kb, e) for buf, kb in blocks for e in range(MLA_HEADS)]
        prods = [(1 - buf, kb + 1, e) for buf, kb in blocks for e in range(MLA_HEADS)]
        for n in range(min(MLA_LEAD, len(prods))):
            scores_head(*prods[n])
        for n, (buf, kb, e) in enumerate(units):
            pv_head(kb, e, softmax_head(buf, e, False))
            if n + MLA_LEAD < len(prods):
                scores_head(*prods[n + MLA_LEAD])

    def quad_body(t, carry):
        pipelined(4 * t, 4)
        return carry

    scores_into(0, 0)
    lax.fori_loop(0, qi // 4, quad_body, 0)

    @pl.when(qi % 4 >= 2)
    def _():
        pipelined((qi // 4) * 4, 2)

    @pl.when(qi % 2 == 1)
    def _():
        pipelined(qi - 1, 1)
        softmax_pv(1, qi, True)

    @pl.when(qi % 2 == 0)
    def _():
        softmax_pv(0, qi, True)

    outs = []
    for e in range(MLA_HEADS):
        acc = acc_ref[e]
        outs.append(acc[0:V_DIM_B] / acc[V_DIM_B:V_DIM_B + 1])
    o_ref[0] = jnp.concatenate(outs, axis=0).T.astype(bf16)


def _merge_kernel(x_ref, oa_ref, ob_ref, gpre_ref, wg_ref, woa_ref, wob_ref, wout_ref, gpost_ref,
                  gpre2_ref, wup_ref, wdn_ref, gpost2_ref, o_ref):
    bf16 = jnp.bfloat16
    x = x_ref[...]
    hm = x.shape[0] // 2
    halves = (slice(0, hm), slice(hm, 2 * hm))
    hb = [(_rms(x[r]) * gpre_ref[...]).astype(bf16) for r in halves]
    gate_a = jnp.concatenate([_dot_nt(h, wg_ref[0:D_MODEL, :]) for h in hb], axis=0)
    hb = jnp.concatenate(hb, axis=0)
    gate_a = jax.nn.sigmoid(gate_a)
    gate_b = jax.nn.sigmoid(_dot_nt(hb, wg_ref[D_MODEL:2 * D_MODEL, :]))
    merged = gate_a * _dot(oa_ref[...], woa_ref[...]) + gate_b * _dot(ob_ref[...], wob_ref[...])
    mb = merged.astype(bf16)

    n_chunks = D_FF // FF_CHUNK
    chunk = lambda c: slice(c * FF_CHUNK, (c + 1) * FF_CHUNK)

    def w_up(c):
        return wup_ref[:, chunk(c)].astype(bf16)

    def w_down(c):
        return wdn_ref[chunk(c), :].astype(bf16)

    def sq_relu(v):
        v = jnp.maximum(v, 0.0)
        return (v * v).astype(bf16)

    x1, h2, up0 = [], [], []
    for r in halves:
        y = _dot(mb[r], wout_ref[...])
        x1.append(x[r] + _rms(y) * gpost_ref[...])
        h2.append((_rms(x1[-1]) * gpre2_ref[...]).astype(bf16))
    w_first = w_up(0)
    for i in range(2):
        up0.append(sq_relu(_dot(h2[i], w_first)))
    h2 = jnp.concatenate(h2, axis=0)
    y2 = _dot(jnp.concatenate(up0, axis=0), w_down(0))
    for c in range(1, n_chunks - 1):
        y2 = y2 + _dot(sq_relu(_dot(h2, w_up(c))), w_down(c))
    last = sq_relu(_dot(h2, w_up(n_chunks - 1)))
    w_last = w_down(n_chunks - 1)
    for i, r in enumerate(halves):
        y2_half = y2[r] + _dot(last[r], w_last)
        o_ref[r, :] = x1[i] + _rms(y2_half) * gpost2_ref[...]


def _const_spec(shape):
    return pl.BlockSpec(shape, lambda *_: (0,) * len(shape), pipeline_mode=pl.Buffered(1))


def kernel(x, positions, pre_norm_mix, w_in, q_a_norm, w_q_b, kv_a_norm, w_kv_b, sinks, w_o_a,
           w_o_b, w_out, post_norm_mix, pre_norm_mlp, w_up, w_down, post_norm_mlp):
    f32, bf16 = jnp.float32, jnp.bfloat16
    B, S, D = x.shape
    T = B * S
    depth = w_in.shape[0]
    for l in range(depth):
        assert w_in.shape[2] == D_IN
        wit = jnp.swapaxes(w_in[l], 0, 1)
        w_gates_t = wit[IN_GATES:IN_QA].astype(bf16)
        ka_t = wit[IN_KA:IN_VA].astype(bf16)
        va_t = wit[IN_VA:IN_CQ].astype(bf16)
        kr_t = wit[IN_KR:D_IN].astype(bf16)
        hr = QK_ROPE // 2
        hd = HEAD_DIM_A
        z = lambda n: jnp.zeros((n, D_MODEL), bf16)
        w_small_t = jnp.concatenate([
            ka_t[:hd], ka_t[:hd], ka_t[hd:], ka_t[hd:],
            z(QK_NOPE), kr_t, z(LANES - Q_HEAD_B),
            z(QK_NOPE), kr_t[hr:], kr_t[:hr], z(LANES - Q_HEAD_B)], axis=0)
        w_va_t = jnp.concatenate([va_t[:hd], z(VT_PAD), va_t[hd:], z(VT_PAD)], axis=0)

        wq = w_q_b[l].reshape(Q_LORA, N_HEADS_B, Q_HEAD_B)
        q_nope, q_rope = wq[..., :QK_NOPE], wq[..., QK_NOPE:]
        zq = lambda n: jnp.zeros((Q_LORA, N_HEADS_B, n), f32)
        wq_main = jnp.concatenate([q_nope, q_rope, zq(LANES - Q_HEAD_B)], -1)
        wq_swap = jnp.concatenate([zq(QK_NOPE), q_rope[..., hr:], q_rope[..., :hr],
                                   zq(LANES - Q_HEAD_B)], -1)
        wq_main = wq_main.reshape(Q_LORA, N_HEADS_B * LANES).astype(bf16)
        wq_swap = wq_swap.reshape(Q_LORA, N_HEADS_B * LANES).astype(bf16)

        wkv = w_kv_b[l].reshape(KV_LORA, N_HEADS_B, KV_HEAD_B)
        kv_k, kv_v = wkv[..., :QK_NOPE], wkv[..., QK_NOPE:]
        zk = jnp.zeros((KV_LORA, N_HEADS_B, HALF), f32)
        wkv_k = jnp.concatenate([kv_k, zk], -1).reshape(KV_LORA, N_HEADS_B * LANES).astype(bf16)
        wkv_vt = kv_v.reshape(KV_LORA, WIDTH_B).T.astype(bf16)

        freq_col = (ROPE_THETA ** (-jnp.arange(0, QK_ROPE, 2, dtype=f32) / QK_ROPE))[:, None]

        row = lambda g: g.reshape(1, -1).astype(f32)
        x2 = x.reshape(T, D)

        tm = PROJ_TM
        assert PROJ_SUB == MLA_BQ and tm % PROJ_SUB == 0 and S % tm == 0
        nkb = S // MLA_BQ
        n_sub = tm // PROJ_SUB
        npb = S // tm
        tok = lambda w: pl.BlockSpec((tm, w), lambda i: (i, 0))
        w_rows = lambda n, start: pl.BlockSpec((n, D), lambda i: (start // n, 0),
                                               pipeline_mode=pl.Buffered(1))
        assert IN_QA % WIDTH_A == 0 and IN_CQ % Q_LORA == 0 and IN_CKV % KV_LORA == 0
        qa, ka_x, va_t3, qb, kb, vt = pl.pallas_call(
            _proj_kernel,
            grid=(T // tm,),
            in_specs=[tok(D), pl.BlockSpec((1, 1, tm), lambda i: (i, 0, 0)), _const_spec((1, D)),
                      w_rows(WIDTH_A, IN_QA), w_rows(Q_LORA, IN_CQ), w_rows(KV_LORA, IN_CKV),
                      _const_spec((SM_ROWS, D)), _const_spec((N_KV_A * VT_ROWS, D)),
                      _const_spec((1, Q_LORA)), _const_spec((Q_LORA, N_HEADS_B * LANES)),
                      _const_spec((Q_LORA, N_HEADS_B * LANES)), _const_spec((1, KV_LORA)),
                      _const_spec((KV_LORA, N_HEADS_B * LANES)),
                      _const_spec((WIDTH_B, KV_LORA)),
                      _const_spec((QK_ROPE // 2, 1))],
            out_specs=[tok(WIDTH_A), tok(2 * LANES),
                       pl.BlockSpec((1, N_KV_A * VT_ROWS, tm), lambda i: (i // npb, 0, i % npb)),
                       tok(N_HEADS_B * LANES), tok(N_HEADS_B * LANES),
                       pl.BlockSpec((1, n_sub, N_HEADS_B * VT_ROWS, PROJ_SUB),
                                    lambda i: (i // npb, i % npb, 0, 0))],
            out_shape=[jax.ShapeDtypeStruct((T, WIDTH_A), bf16),
                       jax.ShapeDtypeStruct((T, 2 * LANES), bf16),
                       jax.ShapeDtypeStruct((B, N_KV_A * VT_ROWS, S), bf16),
                       jax.ShapeDtypeStruct((T, N_HEADS_B * LANES), bf16),
                       jax.ShapeDtypeStruct((T, N_HEADS_B * LANES), bf16),
                       jax.ShapeDtypeStruct((B, nkb, N_HEADS_B * VT_ROWS, PROJ_SUB), bf16)],
            compiler_params=pltpu.CompilerParams(dimension_semantics=("arbitrary",),
                                                 vmem_limit_bytes=VMEM_LIMIT),
            name="proj",
        )(x2, positions.reshape(T // tm, 1, tm), row(pre_norm_mix[l]), wit, wit, wit, w_small_t,
          w_va_t, row(q_a_norm[l]), wq_main, wq_swap, row(kv_a_norm[l]), wkv_k, wkv_vt,
          freq_col)

        tq = SWA_TQ
        nb_per = tq // BLOCK
        qa3 = qa.reshape(B, S, WIDTH_A)
        ka3 = ka_x.reshape(B, S, 2 * LANES)
        pos_r3 = positions.reshape(B, 1, S)
        cur = lambda w: pl.BlockSpec((1, tq, w), lambda b, i: (b, i, 0))
        prev_blk = lambda b, i: jnp.maximum(i * nb_per - 1, 0)
        out_a = pl.pallas_call(
            _swa_kernel,
            grid=(B, S // tq),
            in_specs=[pl.BlockSpec(memory_space=pltpu.SMEM),
                      cur(WIDTH_A), cur(2 * LANES),
                      pl.BlockSpec((1, BLOCK, 2 * LANES), lambda b, i: (b, prev_blk(b, i), 0)),
                      pl.BlockSpec((1, N_KV_A * VT_ROWS, tq), lambda b, i: (b, 0, i)),
                      pl.BlockSpec((1, N_KV_A * VT_ROWS, BLOCK), lambda b, i: (b, 0, prev_blk(b, i))),
                      pl.BlockSpec((1, 1, tq), lambda b, i: (b, 0, i)),
                      pl.BlockSpec((1, 1, BLOCK), lambda b, i: (b, 0, prev_blk(b, i)))],
            out_specs=cur(WIDTH_A),
            out_shape=jax.ShapeDtypeStruct((B, S, WIDTH_A), bf16),
            compiler_params=pltpu.CompilerParams(dimension_semantics=("arbitrary", "arbitrary"),
                                                 vmem_limit_bytes=VMEM_LIMIT),
            name="swa",
        )(sinks[l].astype(f32), qa3, ka3, ka3, va_t3, va_t3, pos_r3, pos_r3)

        bq, nh = MLA_BQ, MLA_HEADS
        qb3 = qb.reshape(B, S, N_HEADS_B * LANES)
        kb3 = kb.reshape(B, S, N_HEADS_B * LANES)
        out_b = pl.pallas_call(
            _mla_kernel,
            grid=(B, N_HEADS_B // nh, S // bq),
            in_specs=[pl.BlockSpec((1, bq, nh * LANES), lambda b, j, i: (b, i, j)),
                      pl.BlockSpec((1, S, nh * LANES), lambda b, j, i: (b, 0, j)),
                      pl.BlockSpec((1, nkb, nh * VT_ROWS, bq), lambda b, j, i: (b, 0, j, 0))],
            out_specs=pl.BlockSpec((1, bq, nh * V_DIM_B), lambda b, j, i: (b, i, j)),
            out_shape=jax.ShapeDtypeStruct((B, S, WIDTH_B), bf16),
            scratch_shapes=[pltpu.VMEM((2, nh, bq, bq), f32),
                            pltpu.VMEM((2, nh, 1, bq), f32),
                            pltpu.VMEM((nh, VT_ROWS, bq), f32), pltpu.VMEM((nh, 1, bq), f32)],
            compiler_params=pltpu.CompilerParams(
                dimension_semantics=("arbitrary", "arbitrary", "arbitrary"),
                vmem_limit_bytes=VMEM_LIMIT),
            name="mla",
        )(qb3, kb3, vt)

        tm = MERGE_TM
        tok = lambda w: pl.BlockSpec((tm, w), lambda i: (i, 0))
        x2 = pl.pallas_call(
            _merge_kernel,
            grid=(T // tm,),
            in_specs=[tok(D), tok(WIDTH_A), tok(WIDTH_B), _const_spec((1, D)),
                      _const_spec((2 * D, D)), _const_spec((WIDTH_A, D)), _const_spec((WIDTH_B, D)),
                      _const_spec((D, D)), _const_spec((1, D)), _const_spec((1, D)),
                      _const_spec((D, D_FF)), _const_spec((D_FF, D)), _const_spec((1, D))],
            out_specs=tok(D),
            out_shape=jax.ShapeDtypeStruct((T, D), f32),
            compiler_params=pltpu.CompilerParams(dimension_semantics=("arbitrary",),
                                                 vmem_limit_bytes=VMEM_LIMIT),
            name="merge_mlp",
        )(x2, out_a.reshape(T, WIDTH_A), out_b.reshape(T, WIDTH_B), row(pre_norm_mix[l]), w_gates_t,
          w_o_a[l].astype(bf16), w_o_b[l].astype(bf16), w_out[l].astype(bf16),
          row(post_norm_mix[l]), row(pre_norm_mlp[l]), w_up[l],
          w_down[l], row(post_norm_mlp[l]))
        x = x2.reshape(B, S, D)
    return x
```

```python
import jax
import jax.numpy as jnp
from jax import lax
from jax.experimental import pallas as pl
from jax.experimental.pallas import tpu as pltpu

D_MODEL = 1024
N_HEADS_A = 8
N_KV_A = 2
HEAD_DIM_A = 64
WINDOW = 128
BLOCK = 128
N_HEADS_B = 8
QK_NOPE = 64
QK_ROPE = 32
V_DIM_B = 64
Q_LORA = 256
KV_LORA = 128
ROPE_THETA = 10000.0
D_FF = 4 * D_MODEL
EPS = 1e-6

WIDTH_A = N_HEADS_A * HEAD_DIM_A
WIDTH_B = N_HEADS_B * V_DIM_B
Q_HEAD_B = QK_NOPE + QK_ROPE
KV_HEAD_B = QK_NOPE + V_DIM_B

LANES = 128
HALF = LANES // 2
VT_ROWS = LANES
VT_PAD = VT_ROWS - V_DIM_B
SCALE_A = HEAD_DIM_A ** -0.5
SCALE_B = Q_HEAD_B ** -0.5
LOG2_E = 1.4426950408889634
QSCALE_A = SCALE_A * LOG2_E
QSCALE_B = SCALE_B * LOG2_E
ALIBI_SLOPES = tuple(2.0 ** (-8.0 * (h + 1) / N_HEADS_A) for h in range(N_HEADS_A))

IN_GATES = 0
IN_QA = 2 * D_MODEL
IN_KA = IN_QA + WIDTH_A
IN_VA = IN_KA + N_KV_A * HEAD_DIM_A
IN_CQ = IN_VA + N_KV_A * HEAD_DIM_A
IN_CKV = IN_CQ + Q_LORA
IN_KR = IN_CKV + KV_LORA
D_IN = IN_KR + QK_ROPE
SM_KA = 0
SM_KR = SM_KA + 2 * LANES
SM_KRS = SM_KR + LANES
SM_ROWS = SM_KRS + LANES

PROJ_TM = 1024
PROJ_SUB = 512
SWA_TQ = 2048
SWA_LEAD = 4
MLA_BQ = 512
MLA_CHUNK = 512
MLA_HEADS = 4
MLA_LEAD = 1
MERGE_TM = 512
FF_CHUNK = 1024
V7X_VMEM_BYTES = 64 * 1024 * 1024
VMEM_LIMIT = V7X_VMEM_BYTES - 1024 * 1024


def _rms(v):
    return v * lax.rsqrt(jnp.mean(v * v, axis=-1, keepdims=True) + EPS)


def _dot(a, b):
    return jnp.dot(a, b, preferred_element_type=jnp.float32)


def _dot_nt(a, b):
    return lax.dot_general(a, b, (((1,), (1,)), ((), ())), preferred_element_type=jnp.float32)


def _proj_kernel(x_ref, pos_ref, gpre_ref, wqa_ref, wcq_ref, wckv_ref, wsm_ref, wvat_ref, gq_ref,
                 wqm_ref, wqs_ref, gkv_ref, wkk_ref, wkvt_ref, freq_ref,
                 qa_ref, ka_ref, vat_ref, qb_ref, kb_ref, vt_ref):
    bf16 = jnp.bfloat16
    sub = PROJ_SUB

    def first_stage(n):
        rows = slice(n * sub, (n + 1) * sub)
        hb = (_rms(x_ref[rows, :]) * gpre_ref[...]).astype(bf16)
        qa_ref[rows, :] = (_dot_nt(hb, wqa_ref[...].astype(bf16)) * QSCALE_A).astype(bf16)
        small = _dot_nt(hb, wsm_ref[...])
        ka_ref[rows, :] = small[:, SM_KA:SM_KR].astype(bf16)
        va_t = _dot_nt(wvat_ref[...], hb)
        varow = lax.broadcasted_iota(jnp.int32, va_t.shape, 0)
        vat_ref[0, :, rows] = jnp.where(varow % VT_ROWS == HEAD_DIM_A, 1.0, va_t).astype(bf16)
        cq = _dot_nt(hb, wcq_ref[...].astype(bf16))
        ckv = _dot_nt(hb, wckv_ref[...].astype(bf16))
        pos = pos_ref[0, :, rows].astype(jnp.float32)
        ang = freq_ref[...] * pos
        cos_t, sin_t = jnp.cos(ang), jnp.sin(ang)
        one = jnp.ones((QK_NOPE, sub), jnp.float32)
        zero = jnp.zeros((QK_NOPE, sub), jnp.float32)
        pad = LANES - Q_HEAD_B
        cos = jnp.concatenate([one, cos_t, cos_t, one[:pad]], axis=0).T
        sin = jnp.concatenate([zero, -sin_t, sin_t, zero[:pad]], axis=0).T
        k_rot = small[:, SM_KR:SM_KRS] * cos + small[:, SM_KRS:SM_ROWS] * sin
        return cq, ckv, cos, sin, k_rot

    def second_stage(n, cq, ckv, cos, sin, k_rot):
        rows = slice(n * sub, (n + 1) * sub)
        cqn = (_rms(cq) * gq_ref[...]).astype(bf16)
        q_main = _dot(cqn, wqm_ref[...])
        q_swap = _dot(cqn, wqs_ref[...])
        ckvn = (_rms(ckv) * gkv_ref[...]).astype(bf16)
        k_nope = _dot(ckvn, wkk_ref[...])
        v_t = _dot_nt(wkvt_ref[...], ckvn)
        tail_row = lax.broadcasted_iota(jnp.int32, (VT_PAD, sub), 0)
        tail = jnp.where(tail_row == 0, 1.0, 0.0).astype(bf16)
        for h in range(N_HEADS_B):
            vt_ref[0, n, h * VT_ROWS:h * VT_ROWS + V_DIM_B, :] = (
                v_t[h * V_DIM_B:(h + 1) * V_DIM_B].astype(bf16))
            vt_ref[0, n, h * VT_ROWS + V_DIM_B:(h + 1) * VT_ROWS, :] = tail
        for h in range(N_HEADS_B):
            sl = slice(h * LANES, (h + 1) * LANES)
            qb_ref[rows, sl] = ((q_main[:, sl] * cos + q_swap[:, sl] * sin) * QSCALE_B).astype(bf16)
            kb_ref[rows, sl] = (k_nope[:, sl] + k_rot).astype(bf16)

    n_sub = x_ref.shape[0] // sub
    staged = [first_stage(n) for n in range(n_sub)]
    for n in range(n_sub):
        second_stage(n, *staged[n])


def _swa_kernel(sink_ref, q_ref, kc_ref, kp_ref, vtc_ref, vtp_ref, posc_ref, posp_ref, o_ref):
    bf16 = jnp.bfloat16
    i = pl.program_id(1)
    krow = lax.broadcasted_iota(jnp.int32, (2 * BLOCK, BLOCK), 0)
    qcol = lax.broadcasted_iota(jnp.int32, (2 * BLOCK, BLOCK), 1)
    ahead = krow - qcol
    band = (ahead > 0) & (ahead <= WINDOW)
    lane = lax.broadcasted_iota(jnp.int32, (BLOCK, LANES), 1)
    low_half = lane < HALF
    pos_inf = jnp.float32(jnp.inf)

    n_pairs = N_HEADS_A // 2
    chains = [(blk, pair) for blk in range(SWA_TQ // BLOCK) for pair in range(n_pairs)]
    bands = {}

    def band_of(blk):
        if blk not in bands:
            r0 = blk * BLOCK
            if blk == 0:
                kband = jnp.concatenate([kp_ref[0], kc_ref[0, 0:BLOCK, :]], axis=0)
                vtband = jnp.concatenate([vtp_ref[0], vtc_ref[0, :, 0:BLOCK]], axis=1)
                kpos = jnp.concatenate([posp_ref[0], posc_ref[0, :, 0:BLOCK]], axis=1)
                mask = band & ((krow >= BLOCK) | (i > 0))
            else:
                kband = kc_ref[0, r0 - BLOCK:r0 + BLOCK, :]
                vtband = vtc_ref[0, :, r0 - BLOCK:r0 + BLOCK]
                kpos = posc_ref[0, :, r0 - BLOCK:r0 + BLOCK]
                mask = band
            qpos = posc_ref[0, :, r0:r0 + BLOCK]
            kpos_col = jnp.broadcast_to(kpos, (BLOCK, 2 * BLOCK)).T
            dist = jnp.abs(kpos_col - qpos).astype(jnp.float32) * LOG2_E
            dist = jnp.where(mask, dist, pos_inf)
            bands[blk] = (kband, vtband, dist)
        return bands[blk]

    def scores(blk, pair):
        kv = (2 * pair) // (N_HEADS_A // N_KV_A)
        r0 = blk * BLOCK
        qp = q_ref[0, r0:r0 + BLOCK, pair * LANES:(pair + 1) * LANES]
        zero = jnp.zeros_like(qp)
        q2 = jnp.concatenate([jnp.where(low_half, qp, zero), jnp.where(low_half, zero, qp)],
                             axis=0)
        kx = band_of(blk)[0][:, kv * LANES:(kv + 1) * LANES]
        return _dot_nt(kx, q2)

    def finish(blk, pair, s2):
        kv = (2 * pair) // (N_HEADS_A // N_KV_A)
        r0 = blk * BLOCK
        _, vtband, dist = band_of(blk)
        vt = vtband[kv * VT_ROWS:(kv + 1) * VT_ROWS, :]
        ps, ms = [], []
        for e in range(2):
            h = 2 * pair + e
            s = s2[:, e * BLOCK:(e + 1) * BLOCK] - ALIBI_SLOPES[h] * dist
            m = jnp.maximum(jnp.max(s, axis=0, keepdims=True), sink_ref[h] * LOG2_E)
            ps.append(jnp.exp2(s - m).astype(bf16))
            ms.append(m)
        o2 = _dot(vt, jnp.concatenate(ps, axis=1))
        outs = []
        for e in range(2):
            h = 2 * pair + e
            o_t = o2[:, e * BLOCK:(e + 1) * BLOCK]
            denom = (o_t[HEAD_DIM_A:HEAD_DIM_A + 1]
                     + jnp.exp2(sink_ref[h] * LOG2_E - ms[e]))
            outs.append(o_t[0:HEAD_DIM_A] / denom)
        o_ref[0, r0:r0 + BLOCK, pair * LANES:(pair + 1) * LANES] = (
            jnp.concatenate(outs, axis=0).T.astype(bf16))

    pending = [scores(*c) for c in chains[:SWA_LEAD]]
    for n, c in enumerate(chains):
        s2 = pending.pop(0)
        if n + SWA_LEAD < len(chains):
            pending.append(scores(*chains[n + SWA_LEAD]))
        finish(*c, s2)


def _mla_kernel(q_ref, k_ref, vt_ref, o_ref, s_ref, bmax_ref, acc_ref, m_ref):
    bf16 = jnp.bfloat16
    bq = MLA_BQ
    ch = MLA_CHUNK
    qi = pl.program_id(2)
    krow = lax.broadcasted_iota(jnp.int32, (bq, ch), 0)
    qcol = lax.broadcasted_iota(jnp.int32, (bq, ch), 1)
    neg_inf = jnp.float32(-jnp.inf)
    m_ref[...] = jnp.full(m_ref.shape, neg_inf, jnp.float32)
    acc_ref[...] = jnp.zeros(acc_ref.shape, jnp.float32)

    def scores_head(buf, kb, e):
        start = pl.multiple_of(kb * bq, bq)
        sl = slice(e * LANES, (e + 1) * LANES)
        s = _dot_nt(k_ref[0, pl.ds(start, bq), sl], q_ref[0, :, sl])
        s_ref[buf, e] = s
        bmax_ref[buf, e] = jnp.max(s, axis=0, keepdims=True)

    def scores_into(buf, kb):
        for e in range(MLA_HEADS):
            scores_head(buf, kb, e)

    def softmax_head(buf, e, masked):
        out = []
        for c in range(bq // ch):
            cs = slice(c * ch, (c + 1) * ch)
            s = s_ref[buf, e, :, cs]
            if masked:
                s = jnp.where(krow <= qcol + c * ch, s, neg_inf)
                blockmax = jnp.max(s, axis=0, keepdims=True)
            else:
                blockmax = bmax_ref[buf, e, :, cs]
            m_prev = m_ref[e, :, cs]
            m_new = jnp.maximum(m_prev, blockmax)
            m_ref[e, :, cs] = m_new
            out.append((jnp.exp2(s - m_new).astype(bf16), jnp.exp2(m_prev - m_new)))
        return out

    def pv_head(kb, e, weights):
        vt = vt_ref[0, kb, e * VT_ROWS:(e + 1) * VT_ROWS, :]
        for c, (p, alpha) in enumerate(weights):
            cs = slice(c * ch, (c + 1) * ch)
            acc_ref[e, :, cs] = alpha * acc_ref[e, :, cs] + _dot(vt, p)

    def softmax_pv(buf, kb, masked):
        for e in range(MLA_HEADS):
            pv_head(kb, e, softmax_head(buf, e, masked))

    def pipelined(first_kb, n_blocks):
        blocks = [(j % 2, first_kb + j) for j in range(n_blocks)]
        units = [(buf, kb, e) for buf, kb in blocks for e in range(MLA_HEADS)]
        prods = [(1 - buf, kb + 1, e) for buf, kb in blocks for e in range(MLA_HEADS)]
        for n in range(min(MLA_LEAD, len(prods))):
            scores_head(*prods[n])
        for n, (buf, kb, e) in enumerate(units):
            pv_head(kb, e, softmax_head(buf, e, False))
            if n + MLA_LEAD < len(prods):
                scores_head(*prods[n + MLA_LEAD])

    def quad_body(t, carry):
        pipelined(4 * t, 4)
        return carry

    scores_into(0, 0)
    lax.fori_loop(0, qi // 4, quad_body, 0)

    @pl.when(qi % 4 >= 2)
    def _():
        pipelined((qi // 4) * 4, 2)

    @pl.when(qi % 2 == 1)
    def _():
        pipelined(qi - 1, 1)
        softmax_pv(1, qi, True)

    @pl.when(qi % 2 == 0)
    def _():
        softmax_pv(0, qi, True)

    outs = []
    for e in range(MLA_HEADS):
        acc = acc_ref[e]
        outs.append(acc[0:V_DIM_B] / acc[V_DIM_B:V_DIM_B + 1])
    o_ref[0] = jnp.concatenate(outs, axis=0).T.astype(bf16)


def _merge_kernel(x_ref, oa_ref, ob_ref, gpre_ref, wg_ref, woa_ref, wob_ref, wout_ref, gpost_ref,
                  gpre2_ref, wup_ref, wdn_ref, gpost2_ref, o_ref):
    bf16 = jnp.bfloat16
    x = x_ref[...]
    hm = x.shape[0] // 2
    halves = (slice(0, hm), slice(hm, 2 * hm))
    hb = [(_rms(x[r]) * gpre_ref[...]).astype(bf16) for r in halves]
    wg_a = wg_ref[0:D_MODEL, :].astype(bf16)
    gate_a = jnp.concatenate([_dot_nt(h, wg_a) for h in hb], axis=0)
    hb = jnp.concatenate(hb, axis=0)
    gate_a = jax.nn.sigmoid(gate_a)
    gate_b = jax.nn.sigmoid(_dot_nt(hb, wg_ref[D_MODEL:2 * D_MODEL, :].astype(bf16)))
    merged = gate_a * _dot(oa_ref[...], woa_ref[...]) + gate_b * _dot(ob_ref[...], wob_ref[...])
    mb = merged.astype(bf16)

    n_chunks = D_FF // FF_CHUNK
    chunk = lambda c: slice(c * FF_CHUNK, (c + 1) * FF_CHUNK)

    def w_up(c):
        return wup_ref[:, chunk(c)].astype(bf16)

    def w_down(c):
        return wdn_ref[chunk(c), :].astype(bf16)

    def sq_relu(v):
        v = jnp.maximum(v, 0.0)
        return (v * v).astype(bf16)

    x1, h2, up0 = [], [], []
    for r in halves:
        y = _dot(mb[r], wout_ref[...])
        x1.append(x[r] + _rms(y) * gpost_ref[...])
        h2.append((_rms(x1[-1]) * gpre2_ref[...]).astype(bf16))
    w_first = w_up(0)
    for i in range(2):
        up0.append(sq_relu(_dot(h2[i], w_first)))
    h2 = jnp.concatenate(h2, axis=0)
    y2 = _dot(jnp.concatenate(up0, axis=0), w_down(0))
    for c in range(1, n_chunks - 1):
        y2 = y2 + _dot(sq_relu(_dot(h2, w_up(c))), w_down(c))
    last = sq_relu(_dot(h2, w_up(n_chunks - 1)))
    w_last = w_down(n_chunks - 1)
    for i, r in enumerate(halves):
        y2_half = y2[r] + _dot(last[r], w_last)
        o_ref[r, :] = x1[i] + _rms(y2_half) * gpost2_ref[...]


def _const_spec(shape):
    return pl.BlockSpec(shape, lambda *_: (0,) * len(shape), pipeline_mode=pl.Buffered(1))


def kernel(x, positions, pre_norm_mix, w_in, q_a_norm, w_q_b, kv_a_norm, w_kv_b, sinks, w_o_a,
           w_o_b, w_out, post_norm_mix, pre_norm_mlp, w_up, w_down, post_norm_mlp):
    f32, bf16 = jnp.float32, jnp.bfloat16
    B, S, D = x.shape
    T = B * S
    depth = w_in.shape[0]
    for l in range(depth):
        assert w_in.shape[2] == D_IN
        wit = jnp.swapaxes(w_in[l], 0, 1)
        ka_t = wit[IN_KA:IN_VA].astype(bf16)
        va_t = wit[IN_VA:IN_CQ].astype(bf16)
        kr_t = wit[IN_KR:D_IN].astype(bf16)
        hr = QK_ROPE // 2
        hd = HEAD_DIM_A
        z = lambda n: jnp.zeros((n, D_MODEL), bf16)
        w_small_t = jnp.concatenate([
            ka_t[:hd], ka_t[:hd], ka_t[hd:], ka_t[hd:],
            z(QK_NOPE), kr_t, z(LANES - Q_HEAD_B),
            z(QK_NOPE), kr_t[hr:], kr_t[:hr], z(LANES - Q_HEAD_B)], axis=0)
        w_va_t = jnp.concatenate([va_t[:hd], z(VT_PAD), va_t[hd:], z(VT_PAD)], axis=0)

        wq = w_q_b[l].reshape(Q_LORA, N_HEADS_B, Q_HEAD_B)
        q_nope, q_rope = wq[..., :QK_NOPE], wq[..., QK_NOPE:]
        zq = lambda n: jnp.zeros((Q_LORA, N_HEADS_B, n), f32)
        wq_main = jnp.concatenate([q_nope, q_rope, zq(LANES - Q_HEAD_B)], -1)
        wq_swap = jnp.concatenate([zq(QK_NOPE), q_rope[..., hr:], q_rope[..., :hr],
                                   zq(LANES - Q_HEAD_B)], -1)
        wq_main = wq_main.reshape(Q_LORA, N_HEADS_B * LANES).astype(bf16)
        wq_swap = wq_swap.reshape(Q_LORA, N_HEADS_B * LANES).astype(bf16)

        wkv = w_kv_b[l].reshape(KV_LORA, N_HEADS_B, KV_HEAD_B)
        kv_k, kv_v = wkv[..., :QK_NOPE], wkv[..., QK_NOPE:]
        zk = jnp.zeros((KV_LORA, N_HEADS_B, HALF), f32)
        wkv_k = jnp.concatenate([kv_k, zk], -1).reshape(KV_LORA, N_HEADS_B * LANES).astype(bf16)
        wkv_vt = kv_v.reshape(KV_LORA, WIDTH_B).T.astype(bf16)

        freq_col = (ROPE_THETA ** (-jnp.arange(0, QK_ROPE, 2, dtype=f32) / QK_ROPE))[:, None]

        row = lambda g: g.reshape(1, -1).astype(f32)
        x2 = x.reshape(T, D)

        tm = PROJ_TM
        assert PROJ_SUB == MLA_BQ and tm % PROJ_SUB == 0 and S % tm == 0
        nkb = S // MLA_BQ
        n_sub = tm // PROJ_SUB
        npb = S // tm
        tok = lambda w: pl.BlockSpec((tm, w), lambda i: (i, 0))
        w_rows = lambda n, start: pl.BlockSpec((n, D), lambda i: (start // n, 0),
                                               pipeline_mode=pl.Buffered(1))
        assert IN_QA % WIDTH_A == 0 and IN_CQ % Q_LORA == 0 and IN_CKV % KV_LORA == 0
        qa, ka_x, va_t3, qb, kb, vt = pl.pallas_call(
            _proj_kernel,
            grid=(T // tm,),
            in_specs=[tok(D), pl.BlockSpec((1, 1, tm), lambda i: (i, 0, 0)), _const_spec((1, D)),
                      w_rows(WIDTH_A, IN_QA), w_rows(Q_LORA, IN_CQ), w_rows(KV_LORA, IN_CKV),
                      _const_spec((SM_ROWS, D)), _const_spec((N_KV_A * VT_ROWS, D)),
                      _const_spec((1, Q_LORA)), _const_spec((Q_LORA, N_HEADS_B * LANES)),
                      _const_spec((Q_LORA, N_HEADS_B * LANES)), _const_spec((1, KV_LORA)),
                      _const_spec((KV_LORA, N_HEADS_B * LANES)),
                      _const_spec((WIDTH_B, KV_LORA)),
                      _const_spec((QK_ROPE // 2, 1))],
            out_specs=[tok(WIDTH_A), tok(2 * LANES),
                       pl.BlockSpec((1, N_KV_A * VT_ROWS, tm), lambda i: (i // npb, 0, i % npb)),
                       tok(N_HEADS_B * LANES), tok(N_HEADS_B * LANES),
                       pl.BlockSpec((1, n_sub, N_HEADS_B * VT_ROWS, PROJ_SUB),
                                    lambda i: (i // npb, i % npb, 0, 0))],
            out_shape=[jax.ShapeDtypeStruct((T, WIDTH_A), bf16),
                       jax.ShapeDtypeStruct((T, 2 * LANES), bf16),
                       jax.ShapeDtypeStruct((B, N_KV_A * VT_ROWS, S), bf16),
                       jax.ShapeDtypeStruct((T, N_HEADS_B * LANES), bf16),
                       jax.ShapeDtypeStruct((T, N_HEADS_B * LANES), bf16),
                       jax.ShapeDtypeStruct((B, nkb, N_HEADS_B * VT_ROWS, PROJ_SUB), bf16)],
            compiler_params=pltpu.CompilerParams(dimension_semantics=("arbitrary",),
                                                 vmem_limit_bytes=VMEM_LIMIT),
            name="proj",
        )(x2, positions.reshape(T // tm, 1, tm), row(pre_norm_mix[l]), wit, wit, wit, w_small_t,
          w_va_t, row(q_a_norm[l]), wq_main, wq_swap, row(kv_a_norm[l]), wkv_k, wkv_vt,
          freq_col)

        tq = SWA_TQ
        nb_per = tq // BLOCK
        qa3 = qa.reshape(B, S, WIDTH_A)
        ka3 = ka_x.reshape(B, S, 2 * LANES)
        pos_r3 = positions.reshape(B, 1, S)
        cur = lambda w: pl.BlockSpec((1, tq, w), lambda b, i: (b, i, 0))
        prev_blk = lambda b, i: jnp.maximum(i * nb_per - 1, 0)
        out_a = pl.pallas_call(
            _swa_kernel,
            grid=(B, S // tq),
            in_specs=[pl.BlockSpec(memory_space=pltpu.SMEM),
                      cur(WIDTH_A), cur(2 * LANES),
                      pl.BlockSpec((1, BLOCK, 2 * LANES), lambda b, i: (b, prev_blk(b, i), 0)),
                      pl.BlockSpec((1, N_KV_A * VT_ROWS, tq), lambda b, i: (b, 0, i)),
                      pl.BlockSpec((1, N_KV_A * VT_ROWS, BLOCK), lambda b, i: (b, 0, prev_blk(b, i))),
                      pl.BlockSpec((1, 1, tq), lambda b, i: (b, 0, i)),
                      pl.BlockSpec((1, 1, BLOCK), lambda b, i: (b, 0, prev_blk(b, i)))],
            out_specs=cur(WIDTH_A),
            out_shape=jax.ShapeDtypeStruct((B, S, WIDTH_A), bf16),
            compiler_params=pltpu.CompilerParams(dimension_semantics=("arbitrary", "arbitrary"),
                                                 vmem_limit_bytes=VMEM_LIMIT),
            name="swa",
        )(sinks[l].astype(f32), qa3, ka3, ka3, va_t3, va_t3, pos_r3, pos_r3)

        bq, nh = MLA_BQ, MLA_HEADS
        qb3 = qb.reshape(B, S, N_HEADS_B * LANES)
        kb3 = kb.reshape(B, S, N_HEADS_B * LANES)
        out_b = pl.pallas_call(
            _mla_kernel,
            grid=(B, N_HEADS_B // nh, S // bq),
            in_specs=[pl.BlockSpec((1, bq, nh * LANES), lambda b, j, i: (b, i, j)),
                      pl.BlockSpec((1, S, nh * LANES), lambda b, j, i: (b, 0, j)),
                      pl.BlockSpec((1, nkb, nh * VT_ROWS, bq), lambda b, j, i: (b, 0, j, 0))],
            out_specs=pl.BlockSpec((1, bq, nh * V_DIM_B), lambda b, j, i: (b, i, j)),
            out_shape=jax.ShapeDtypeStruct((B, S, WIDTH_B), bf16),
            scratch_shapes=[pltpu.VMEM((2, nh, bq, bq), f32),
                            pltpu.VMEM((2, nh, 1, bq), f32),
                            pltpu.VMEM((nh, VT_ROWS, bq), f32), pltpu.VMEM((nh, 1, bq), f32)],
            compiler_params=pltpu.CompilerParams(
                dimension_semantics=("arbitrary", "arbitrary", "arbitrary"),
                vmem_limit_bytes=VMEM_LIMIT),
            name="mla",
        )(qb3, kb3, vt)

        tm = MERGE_TM
        tok = lambda w: pl.BlockSpec((tm, w), lambda i: (i, 0))
        x2 = pl.pallas_call(
            _merge_kernel,
            grid=(T // tm,),
            in_specs=[tok(D), tok(WIDTH_A), tok(WIDTH_B), _const_spec((1, D)),
                      w_rows(2 * D, IN_GATES), _const_spec((WIDTH_A, D)), _const_spec((WIDTH_B, D)),
                      _const_spec((D, D)), _const_spec((1, D)), _const_spec((1, D)),
                      _const_spec((D, D_FF)), _const_spec((D_FF, D)), _const_spec((1, D))],
            out_specs=tok(D),
            out_shape=jax.ShapeDtypeStruct((T, D), f32),
            compiler_params=pltpu.CompilerParams(dimension_semantics=("arbitrary",),
                                                 vmem_limit_bytes=VMEM_LIMIT),
            name="merge_mlp",
        )(x2, out_a.reshape(T, WIDTH_A), out_b.reshape(T, WIDTH_B), row(pre_norm_mix[l]), wit,
          w_o_a[l].astype(bf16), w_o_b[l].astype(bf16), w_out[l].astype(bf16),
          row(post_norm_mix[l]), row(pre_norm_mlp[l]), w_up[l],
          w_down[l], row(post_norm_mlp[l]))
        x = x2.reshape(B, S, D)
    return x
```

```python
import jax
import jax.numpy as jnp
from jax import lax
from jax.experimental import pallas as pl
from jax.experimental.pallas import tpu as pltpu

D_MODEL = 1024
N_HEADS_A = 8
N_KV_A = 2
HEAD_DIM_A = 64
WINDOW = 128
BLOCK = 128
N_HEADS_B = 8
QK_NOPE = 64
QK_ROPE = 32
V_DIM_B = 64
Q_LORA = 256
KV_LORA = 128
ROPE_THETA = 10000.0
D_FF = 4 * D_MODEL
EPS = 1e-6

WIDTH_A = N_HEADS_A * HEAD_DIM_A
WIDTH_B = N_HEADS_B * V_DIM_B
Q_HEAD_B = QK_NOPE + QK_ROPE
KV_HEAD_B = QK_NOPE + V_DIM_B

LANES = 128
HALF = LANES // 2
VT_ROWS = LANES
VT_PAD = VT_ROWS - V_DIM_B
SCALE_A = HEAD_DIM_A ** -0.5
SCALE_B = Q_HEAD_B ** -0.5
LOG2_E = 1.4426950408889634
QSCALE_A = SCALE_A * LOG2_E
QSCALE_B = SCALE_B * LOG2_E
ALIBI_SLOPES = tuple(2.0 ** (-8.0 * (h + 1) / N_HEADS_A) for h in range(N_HEADS_A))

IN_GATES = 0
IN_QA = 2 * D_MODEL
IN_KA = IN_QA + WIDTH_A
IN_VA = IN_KA + N_KV_A * HEAD_DIM_A
IN_CQ = IN_VA + N_KV_A * HEAD_DIM_A
IN_CKV = IN_CQ + Q_LORA
IN_KR = IN_CKV + KV_LORA
D_IN = IN_KR + QK_ROPE
SM_KA = 0
SM_KR = SM_KA + 2 * LANES
SM_KRS = SM_KR + LANES
SM_ROWS = SM_KRS + LANES

PROJ_TM = 1024
PROJ_SUB = 512
SWA_TQ = 2048
SWA_LEAD = 4
MLA_BQ = 512
MLA_CHUNK = 512
MLA_HEADS = 4
MLA_LEAD = 1
MERGE_TM = 512
FF_CHUNK = 1024
V7X_VMEM_BYTES = 64 * 1024 * 1024
VMEM_LIMIT = V7X_VMEM_BYTES - 4 * 1024 * 1024


def _rms(v):
    return v * lax.rsqrt(jnp.mean(v * v, axis=-1, keepdims=True) + EPS)


def _dot(a, b):
    return jnp.dot(a, b, preferred_element_type=jnp.float32)


def _dot_nt(a, b):
    return lax.dot_general(a, b, (((1,), (1,)), ((), ())), preferred_element_type=jnp.float32)


def _proj_kernel(x_ref, pos_ref, gpre_ref, wqa_ref, wcq_ref, wckv_ref, wsm_ref, wvat_ref, gq_ref,
                 wqm_ref, wqs_ref, gkv_ref, wkk_ref, wkvt_ref, freq_ref,
                 qa_ref, ka_ref, vat_ref, qb_ref, kb_ref, vt_ref):
    bf16 = jnp.bfloat16
    sub = PROJ_SUB

    def first_stage(n):
        rows = slice(n * sub, (n + 1) * sub)
        hb = (_rms(x_ref[rows, :]) * gpre_ref[...]).astype(bf16)
        qa_ref[rows, :] = (_dot_nt(hb, wqa_ref[...]) * QSCALE_A).astype(bf16)
        small = _dot_nt(hb, wsm_ref[...])
        ka_ref[rows, :] = small[:, SM_KA:SM_KR].astype(bf16)
        va_t = _dot_nt(wvat_ref[...], hb)
        varow = lax.broadcasted_iota(jnp.int32, va_t.shape, 0)
        vat_ref[0, :, rows] = jnp.where(varow % VT_ROWS == HEAD_DIM_A, 1.0, va_t).astype(bf16)
        cq = _dot_nt(hb, wcq_ref[...])
        ckv = _dot_nt(hb, wckv_ref[...])
        pos = pos_ref[0, :, rows].astype(jnp.float32)
        ang = freq_ref[...] * pos
        cos_t, sin_t = jnp.cos(ang), jnp.sin(ang)
        one = jnp.ones((QK_NOPE, sub), jnp.float32)
        zero = jnp.zeros((QK_NOPE, sub), jnp.float32)
        pad = LANES - Q_HEAD_B
        cos = jnp.concatenate([one, cos_t, cos_t, one[:pad]], axis=0).T
        sin = jnp.concatenate([zero, -sin_t, sin_t, zero[:pad]], axis=0).T
        k_rot = small[:, SM_KR:SM_KRS] * cos + small[:, SM_KRS:SM_ROWS] * sin
        return cq, ckv, cos, sin, k_rot

    def second_stage(n, cq, ckv, cos, sin, k_rot):
        rows = slice(n * sub, (n + 1) * sub)
        cqn = (_rms(cq) * gq_ref[...]).astype(bf16)
        q_main = _dot(cqn, wqm_ref[...])
        q_swap = _dot(cqn, wqs_ref[...])
        ckvn = (_rms(ckv) * gkv_ref[...]).astype(bf16)
        k_nope = _dot(ckvn, wkk_ref[...])
        v_t = _dot_nt(wkvt_ref[...], ckvn)
        tail_row = lax.broadcasted_iota(jnp.int32, (VT_PAD, sub), 0)
        tail = jnp.where(tail_row == 0, 1.0, 0.0).astype(bf16)
        for h in range(N_HEADS_B):
            vt_ref[0, n, h * VT_ROWS:h * VT_ROWS + V_DIM_B, :] = (
                v_t[h * V_DIM_B:(h + 1) * V_DIM_B].astype(bf16))
            vt_ref[0, n, h * VT_ROWS + V_DIM_B:(h + 1) * VT_ROWS, :] = tail
        for h in range(N_HEADS_B):
            sl = slice(h * LANES, (h + 1) * LANES)
            qb_ref[rows, sl] = ((q_main[:, sl] * cos + q_swap[:, sl] * sin) * QSCALE_B).astype(bf16)
            kb_ref[rows, sl] = (k_nope[:, sl] + k_rot).astype(bf16)

    n_sub = x_ref.shape[0] // sub
    staged = [first_stage(n) for n in range(n_sub)]
    for n in range(n_sub):
        second_stage(n, *staged[n])


def _swa_kernel(sink_ref, q_ref, kc_ref, kp_ref, vtc_ref, vtp_ref, posc_ref, posp_ref, o_ref):
    bf16 = jnp.bfloat16
    i = pl.program_id(1)
    krow = lax.broadcasted_iota(jnp.int32, (2 * BLOCK, BLOCK), 0)
    qcol = lax.broadcasted_iota(jnp.int32, (2 * BLOCK, BLOCK), 1)
    ahead = krow - qcol
    band = (ahead > 0) & (ahead <= WINDOW)
    lane = lax.broadcasted_iota(jnp.int32, (BLOCK, LANES), 1)
    low_half = lane < HALF
    pos_inf = jnp.float32(jnp.inf)

    n_pairs = N_HEADS_A // 2
    chains = [(blk, pair) for blk in range(SWA_TQ // BLOCK) for pair in range(n_pairs)]
    bands = {}

    def band_of(blk):
        if blk not in bands:
            r0 = blk * BLOCK
            if blk == 0:
                kband = jnp.concatenate([kp_ref[0], kc_ref[0, 0:BLOCK, :]], axis=0)
                vtband = jnp.concatenate([vtp_ref[0], vtc_ref[0, :, 0:BLOCK]], axis=1)
                kpos = jnp.concatenate([posp_ref[0], posc_ref[0, :, 0:BLOCK]], axis=1)
                mask = band & ((krow >= BLOCK) | (i > 0))
            else:
                kband = kc_ref[0, r0 - BLOCK:r0 + BLOCK, :]
                vtband = vtc_ref[0, :, r0 - BLOCK:r0 + BLOCK]
                kpos = posc_ref[0, :, r0 - BLOCK:r0 + BLOCK]
                mask = band
            qpos = posc_ref[0, :, r0:r0 + BLOCK]
            kpos_col = jnp.broadcast_to(kpos, (BLOCK, 2 * BLOCK)).T
            dist = jnp.abs(kpos_col - qpos).astype(jnp.float32) * LOG2_E
            dist = jnp.where(mask, dist, pos_inf)
            bands[blk] = (kband, vtband, dist)
        return bands[blk]

    def scores(blk, pair):
        kv = (2 * pair) // (N_HEADS_A // N_KV_A)
        r0 = blk * BLOCK
        qp = q_ref[0, r0:r0 + BLOCK, pair * LANES:(pair + 1) * LANES]
        zero = jnp.zeros_like(qp)
        q2 = jnp.concatenate([jnp.where(low_half, qp, zero), jnp.where(low_half, zero, qp)],
                             axis=0)
        kx = band_of(blk)[0][:, kv * LANES:(kv + 1) * LANES]
        return _dot_nt(kx, q2)

    def finish(blk, pair, s2):
        kv = (2 * pair) // (N_HEADS_A // N_KV_A)
        r0 = blk * BLOCK
        _, vtband, dist = band_of(blk)
        vt = vtband[kv * VT_ROWS:(kv + 1) * VT_ROWS, :]
        ps, ms = [], []
        for e in range(2):
            h = 2 * pair + e
            s = s2[:, e * BLOCK:(e + 1) * BLOCK] - ALIBI_SLOPES[h] * dist
            m = jnp.maximum(jnp.max(s, axis=0, keepdims=True), sink_ref[h] * LOG2_E)
            ps.append(jnp.exp2(s - m).astype(bf16))
            ms.append(m)
        o2 = _dot(vt, jnp.concatenate(ps, axis=1))
        outs = []
        for e in range(2):
            h = 2 * pair + e
            o_t = o2[:, e * BLOCK:(e + 1) * BLOCK]
            denom = (o_t[HEAD_DIM_A:HEAD_DIM_A + 1]
                     + jnp.exp2(sink_ref[h] * LOG2_E - ms[e]))
            outs.append(o_t[0:HEAD_DIM_A] / denom)
        o_ref[0, r0:r0 + BLOCK, pair * LANES:(pair + 1) * LANES] = (
            jnp.concatenate(outs, axis=0).T.astype(bf16))

    pending = [scores(*c) for c in chains[:SWA_LEAD]]
    for n, c in enumerate(chains):
        s2 = pending.pop(0)
        if n + SWA_LEAD < len(chains):
            pending.append(scores(*chains[n + SWA_LEAD]))
        finish(*c, s2)


def _mla_kernel(q_ref, k_ref, vt_ref, o_ref, s_ref, bmax_ref, acc_ref, m_ref):
    bf16 = jnp.bfloat16
    bq = MLA_BQ
    ch = MLA_CHUNK
    qi = pl.program_id(2)
    krow = lax.broadcasted_iota(jnp.int32, (bq, ch), 0)
    qcol = lax.broadcasted_iota(jnp.int32, (bq, ch), 1)
    neg_inf = jnp.float32(-jnp.inf)
    m_ref[...] = jnp.full(m_ref.shape, neg_inf, jnp.float32)
    acc_ref[...] = jnp.zeros(acc_ref.shape, jnp.float32)

    def scores_head(buf, kb, e):
        start = pl.multiple_of(kb * bq, bq)
        sl = slice(e * LANES, (e + 1) * LANES)
        s = _dot_nt(k_ref[0, pl.ds(start, bq), sl], q_ref[0, :, sl])
        s_ref[buf, e] = s
        bmax_ref[buf, e] = jnp.max(s, axis=0, keepdims=True)

    def scores_into(buf, kb):
        for e in range(MLA_HEADS):
            scores_head(buf, kb, e)

    def softmax_head(buf, e, masked):
        out = []
        for c in range(bq // ch):
            cs = slice(c * ch, (c + 1) * ch)
            s = s_ref[buf, e, :, cs]
            if masked:
                s = jnp.where(krow <= qcol + c * ch, s, neg_inf)
                blockmax = jnp.max(s, axis=0, keepdims=True)
            else:
                blockmax = bmax_ref[buf, e, :, cs]
            m_prev = m_ref[e, :, cs]
            m_new = jnp.maximum(m_prev, blockmax)
            m_ref[e, :, cs] = m_new
            out.append((jnp.exp2(s - m_new).astype(bf16), jnp.exp2(m_prev - m_new)))
        return out

    def pv_head(kb, e, weights):
        vt = vt_ref[0, kb, e * VT_ROWS:(e + 1) * VT_ROWS, :]
        for c, (p, alpha) in enumerate(weights):
            cs = slice(c * ch, (c + 1) * ch)
            acc_ref[e, :, cs] = alpha * acc_ref[e, :, cs] + _dot(vt, p)

    def softmax_pv(buf, kb, masked):
        for e in range(MLA_HEADS):
            pv_head(kb, e, softmax_head(buf, e, masked))

    def pipelined(first_kb, n_blocks):
        blocks = [(j % 2, first_kb + j) for j in range(n_blocks)]
        units = [(buf, kb, e) for buf, kb in blocks for e in range(MLA_HEADS)]
        prods = [(1 - buf, kb + 1, e) for buf, kb in blocks for e in range(MLA_HEADS)]
        for n in range(min(MLA_LEAD, len(prods))):
            scores_head(*prods[n])
        for n, (buf, kb, e) in enumerate(units):
            pv_head(kb, e, softmax_head(buf, e, False))
            if n + MLA_LEAD < len(prods):
                scores_head(*prods[n + MLA_LEAD])

    def quad_body(t, carry):
        pipelined(4 * t, 4)
        return carry

    scores_into(0, 0)
    lax.fori_loop(0, qi // 4, quad_body, 0)

    @pl.when(qi % 4 >= 2)
    def _():
        pipelined((qi // 4) * 4, 2)

    @pl.when(qi % 2 == 1)
    def _():
        pipelined(qi - 1, 1)
        softmax_pv(1, qi, True)

    @pl.when(qi % 2 == 0)
    def _():
        softmax_pv(0, qi, True)

    outs = []
    for e in range(MLA_HEADS):
        acc = acc_ref[e]
        outs.append(acc[0:V_DIM_B] / acc[V_DIM_B:V_DIM_B + 1])
    o_ref[0] = jnp.concatenate(outs, axis=0).T.astype(bf16)


def _merge_kernel(x_ref, oa_ref, ob_ref, gpre_ref, wg_ref, woa_ref, wob_ref, wout_ref, gpost_ref,
                  gpre2_ref, x1_ref, h2_ref):
    bf16 = jnp.bfloat16
    x = x_ref[...]
    hm = x.shape[0] // 2
    halves = (slice(0, hm), slice(hm, 2 * hm))
    hb = [(_rms(x[r]) * gpre_ref[...]).astype(bf16) for r in halves]
    gate_a = jnp.concatenate([_dot_nt(h, wg_ref[0:D_MODEL, :]) for h in hb], axis=0)
    hb = jnp.concatenate(hb, axis=0)
    gate_a = jax.nn.sigmoid(gate_a)
    gate_b = jax.nn.sigmoid(_dot_nt(hb, wg_ref[D_MODEL:2 * D_MODEL, :]))
    merged = gate_a * _dot(oa_ref[...], woa_ref[...]) + gate_b * _dot(ob_ref[...], wob_ref[...])
    mb = merged.astype(bf16)
    for r in halves:
        y = _dot(mb[r], wout_ref[...])
        x1 = x[r] + _rms(y) * gpost_ref[...]
        x1_ref[r, :] = x1
        h2_ref[r, :] = (_rms(x1) * gpre2_ref[...]).astype(bf16)


def _mlp_kernel(x1_ref, h2_ref, wup_ref, wdn_ref, gpost2_ref, o_ref):
    bf16 = jnp.bfloat16
    h2 = h2_ref[...]
    hm = h2.shape[0] // 2
    halves = (slice(0, hm), slice(hm, 2 * hm))
    n_chunks = D_FF // FF_CHUNK
    chunk = lambda c: slice(c * FF_CHUNK, (c + 1) * FF_CHUNK)

    def w_up(c):
        return wup_ref[:, chunk(c)].astype(bf16)

    def w_down(c):
        return wdn_ref[chunk(c), :].astype(bf16)

    def sq_relu(v):
        v = jnp.maximum(v, 0.0)
        return (v * v).astype(bf16)

    y2 = _dot(sq_relu(_dot(h2, w_up(0))), w_down(0))
    for c in range(1, n_chunks - 1):
        y2 = y2 + _dot(sq_relu(_dot(h2, w_up(c))), w_down(c))
    last = sq_relu(_dot(h2, w_up(n_chunks - 1)))
    w_last = w_down(n_chunks - 1)
    for i, r in enumerate(halves):
        y2_half = y2[r] + _dot(last[r], w_last)
        o_ref[r, :] = x1_ref[r, :] + _rms(y2_half) * gpost2_ref[...]


def _const_spec(shape):
    return pl.BlockSpec(shape, lambda *_: (0,) * len(shape), pipeline_mode=pl.Buffered(1))


def kernel(x, positions, pre_norm_mix, w_in, q_a_norm, w_q_b, kv_a_norm, w_kv_b, sinks, w_o_a,
           w_o_b, w_out, post_norm_mix, pre_norm_mlp, w_up, w_down, post_norm_mlp):
    f32, bf16 = jnp.float32, jnp.bfloat16
    B, S, D = x.shape
    T = B * S
    depth = w_in.shape[0]
    for l in range(depth):
        assert w_in.shape[2] == D_IN
        wit = jnp.swapaxes(w_in[l], 0, 1).astype(bf16)
        ka_t = wit[IN_KA:IN_VA]
        va_t = wit[IN_VA:IN_CQ]
        kr_t = wit[IN_KR:D_IN]
        hr = QK_ROPE // 2
        hd = HEAD_DIM_A
        z = lambda n: jnp.zeros((n, D_MODEL), bf16)
        w_small_t = jnp.concatenate([
            ka_t[:hd], ka_t[:hd], ka_t[hd:], ka_t[hd:],
            z(QK_NOPE), kr_t, z(LANES - Q_HEAD_B),
            z(QK_NOPE), kr_t[hr:], kr_t[:hr], z(LANES - Q_HEAD_B)], axis=0)
        w_va_t = jnp.concatenate([va_t[:hd], z(VT_PAD), va_t[hd:], z(VT_PAD)], axis=0)

        wq = w_q_b[l].reshape(Q_LORA, N_HEADS_B, Q_HEAD_B)
        q_nope, q_rope = wq[..., :QK_NOPE], wq[..., QK_NOPE:]
        zq = lambda n: jnp.zeros((Q_LORA, N_HEADS_B, n), f32)
        wq_main = jnp.concatenate([q_nope, q_rope, zq(LANES - Q_HEAD_B)], -1)
        wq_swap = jnp.concatenate([zq(QK_NOPE), q_rope[..., hr:], q_rope[..., :hr],
                                   zq(LANES - Q_HEAD_B)], -1)
        wq_main = wq_main.reshape(Q_LORA, N_HEADS_B * LANES).astype(bf16)
        wq_swap = wq_swap.reshape(Q_LORA, N_HEADS_B * LANES).astype(bf16)

        wkv = w_kv_b[l].reshape(KV_LORA, N_HEADS_B, KV_HEAD_B)
        kv_k, kv_v = wkv[..., :QK_NOPE], wkv[..., QK_NOPE:]
        zk = jnp.zeros((KV_LORA, N_HEADS_B, HALF), f32)
        wkv_k = jnp.concatenate([kv_k, zk], -1).reshape(KV_LORA, N_HEADS_B * LANES).astype(bf16)
        wkv_vt = kv_v.reshape(KV_LORA, WIDTH_B).T.astype(bf16)

        freq_col = (ROPE_THETA ** (-jnp.arange(0, QK_ROPE, 2, dtype=f32) / QK_ROPE))[:, None]

        row = lambda g: g.reshape(1, -1).astype(f32)
        x2 = x.reshape(T, D)

        tm = PROJ_TM
        assert PROJ_SUB == MLA_BQ and tm % PROJ_SUB == 0 and S % tm == 0
        nkb = S // MLA_BQ
        n_sub = tm // PROJ_SUB
        npb = S // tm
        tok = lambda w: pl.BlockSpec((tm, w), lambda i: (i, 0))
        w_rows = lambda n, start: pl.BlockSpec((n, D), lambda i: (start // n, 0),
                                               pipeline_mode=pl.Buffered(1))
        assert IN_QA % WIDTH_A == 0 and IN_CQ % Q_LORA == 0 and IN_CKV % KV_LORA == 0
        qa, ka_x, va_t3, qb, kb, vt = pl.pallas_call(
            _proj_kernel,
            grid=(T // tm,),
            in_specs=[tok(D), pl.BlockSpec((1, 1, tm), lambda i: (i, 0, 0)), _const_spec((1, D)),
                      w_rows(WIDTH_A, IN_QA), w_rows(Q_LORA, IN_CQ), w_rows(KV_LORA, IN_CKV),
                      _const_spec((SM_ROWS, D)), _const_spec((N_KV_A * VT_ROWS, D)),
                      _const_spec((1, Q_LORA)), _const_spec((Q_LORA, N_HEADS_B * LANES)),
                      _const_spec((Q_LORA, N_HEADS_B * LANES)), _const_spec((1, KV_LORA)),
                      _const_spec((KV_LORA, N_HEADS_B * LANES)),
                      _const_spec((WIDTH_B, KV_LORA)),
                      _const_spec((QK_ROPE // 2, 1))],
            out_specs=[tok(WIDTH_A), tok(2 * LANES),
                       pl.BlockSpec((1, N_KV_A * VT_ROWS, tm), lambda i: (i // npb, 0, i % npb)),
                       tok(N_HEADS_B * LANES), tok(N_HEADS_B * LANES),
                       pl.BlockSpec((1, n_sub, N_HEADS_B * VT_ROWS, PROJ_SUB),
                                    lambda i: (i // npb, i % npb, 0, 0))],
            out_shape=[jax.ShapeDtypeStruct((T, WIDTH_A), bf16),
                       jax.ShapeDtypeStruct((T, 2 * LANES), bf16),
                       jax.ShapeDtypeStruct((B, N_KV_A * VT_ROWS, S), bf16),
                       jax.ShapeDtypeStruct((T, N_HEADS_B * LANES), bf16),
                       jax.ShapeDtypeStruct((T, N_HEADS_B * LANES), bf16),
                       jax.ShapeDtypeStruct((B, nkb, N_HEADS_B * VT_ROWS, PROJ_SUB), bf16)],
            compiler_params=pltpu.CompilerParams(dimension_semantics=("arbitrary",),
                                                 vmem_limit_bytes=VMEM_LIMIT),
            name="proj",
        )(x2, positions.reshape(T // tm, 1, tm), row(pre_norm_mix[l]), wit, wit, wit, w_small_t,
          w_va_t, row(q_a_norm[l]), wq_main, wq_swap, row(kv_a_norm[l]), wkv_k, wkv_vt,
          freq_col)

        tq = SWA_TQ
        nb_per = tq // BLOCK
        qa3 = qa.reshape(B, S, WIDTH_A)
        ka3 = ka_x.reshape(B, S, 2 * LANES)
        pos_r3 = positions.reshape(B, 1, S)
        cur = lambda w: pl.BlockSpec((1, tq, w), lambda b, i: (b, i, 0))
        prev_blk = lambda b, i: jnp.maximum(i * nb_per - 1, 0)
        out_a = pl.pallas_call(
            _swa_kernel,
            grid=(B, S // tq),
            in_specs=[pl.BlockSpec(memory_space=pltpu.SMEM),
                      cur(WIDTH_A), cur(2 * LANES),
                      pl.BlockSpec((1, BLOCK, 2 * LANES), lambda b, i: (b, prev_blk(b, i), 0)),
                      pl.BlockSpec((1, N_KV_A * VT_ROWS, tq), lambda b, i: (b, 0, i)),
                      pl.BlockSpec((1, N_KV_A * VT_ROWS, BLOCK), lambda b, i: (b, 0, prev_blk(b, i))),
                      pl.BlockSpec((1, 1, tq), lambda b, i: (b, 0, i)),
                      pl.BlockSpec((1, 1, BLOCK), lambda b, i: (b, 0, prev_blk(b, i)))],
            out_specs=cur(WIDTH_A),
            out_shape=jax.ShapeDtypeStruct((B, S, WIDTH_A), bf16),
            compiler_params=pltpu.CompilerParams(dimension_semantics=("arbitrary", "arbitrary"),
                                                 vmem_limit_bytes=VMEM_LIMIT),
            name="swa",
        )(sinks[l].astype(f32), qa3, ka3, ka3, va_t3, va_t3, pos_r3, pos_r3)

        bq, nh = MLA_BQ, MLA_HEADS
        qb3 = qb.reshape(B, S, N_HEADS_B * LANES)
        kb3 = kb.reshape(B, S, N_HEADS_B * LANES)
        out_b = pl.pallas_call(
            _mla_kernel,
            grid=(B, N_HEADS_B // nh, S // bq),
            in_specs=[pl.BlockSpec((1, bq, nh * LANES), lambda b, j, i: (b, i, j)),
                      pl.BlockSpec((1, S, nh * LANES), lambda b, j, i: (b, 0, j)),
                      pl.BlockSpec((1, nkb, nh * VT_ROWS, bq), lambda b, j, i: (b, 0, j, 0))],
            out_specs=pl.BlockSpec((1, bq, nh * V_DIM_B), lambda b, j, i: (b, i, j)),
            out_shape=jax.ShapeDtypeStruct((B, S, WIDTH_B), bf16),
            scratch_shapes=[pltpu.VMEM((2, nh, bq, bq), f32),
                            pltpu.VMEM((2, nh, 1, bq), f32),
                            pltpu.VMEM((nh, VT_ROWS, bq), f32), pltpu.VMEM((nh, 1, bq), f32)],
            compiler_params=pltpu.CompilerParams(
                dimension_semantics=("arbitrary", "arbitrary", "arbitrary"),
                vmem_limit_bytes=VMEM_LIMIT),
            name="mla",
        )(qb3, kb3, vt)

        tm = 2 * MERGE_TM
        tok = lambda w: pl.BlockSpec((tm, w), lambda i: (i, 0))
        x1, h2 = pl.pallas_call(
            _merge_kernel,
            grid=(T // tm,),
            in_specs=[tok(D), tok(WIDTH_A), tok(WIDTH_B), _const_spec((1, D)),
                      w_rows(2 * D, IN_GATES), _const_spec((WIDTH_A, D)), _const_spec((WIDTH_B, D)),
                      _const_spec((D, D)), _const_spec((1, D)), _const_spec((1, D))],
            out_specs=[tok(D), tok(D)],
            out_shape=[jax.ShapeDtypeStruct((T, D), f32), jax.ShapeDtypeStruct((T, D), bf16)],
            compiler_params=pltpu.CompilerParams(dimension_semantics=("arbitrary",),
                                                 vmem_limit_bytes=VMEM_LIMIT),
            name="merge",
        )(x2, out_a.reshape(T, WIDTH_A), out_b.reshape(T, WIDTH_B), row(pre_norm_mix[l]), wit,
          w_o_a[l].astype(bf16), w_o_b[l].astype(bf16), w_out[l].astype(bf16),
          row(post_norm_mix[l]), row(pre_norm_mlp[l]))

        tm = MERGE_TM
        tok = lambda w: pl.BlockSpec((tm, w), lambda i: (i, 0))
        x2 = pl.pallas_call(
            _mlp_kernel,
            grid=(T // tm,),
            in_specs=[tok(D), tok(D), _const_spec((D, D_FF)), _const_spec((D_FF, D)),
                      _const_spec((1, D))],
            out_specs=tok(D),
            out_shape=jax.ShapeDtypeStruct((T, D), f32),
            compiler_params=pltpu.CompilerParams(dimension_semantics=("arbitrary",),
                                                 vmem_limit_bytes=VMEM_LIMIT),
            name="mlp",
        )(x1, h2, w_up[l], w_down[l], row(post_norm_mlp[l]))
        x = x2.reshape(B, S, D)
    return x
```

```python
import jax
import jax.numpy as jnp
from jax import lax
from jax.experimental import pallas as pl
from jax.experimental.pallas import tpu as pltpu

D_MODEL = 1024
N_HEADS_A = 8
N_KV_A = 2
HEAD_DIM_A = 64
WINDOW = 128
BLOCK = 128
N_HEADS_B = 8
QK_NOPE = 64
QK_ROPE = 32
V_DIM_B = 64
Q_LORA = 256
KV_LORA = 128
ROPE_THETA = 10000.0
D_FF = 4 * D_MODEL
EPS = 1e-6

WIDTH_A = N_HEADS_A * HEAD_DIM_A
WIDTH_B = N_HEADS_B * V_DIM_B
Q_HEAD_B = QK_NOPE + QK_ROPE
KV_HEAD_B = QK_NOPE + V_DIM_B

LANES = 128
HALF = LANES // 2
VT_ROWS = LANES
VT_PAD = VT_ROWS - V_DIM_B
SCALE_A = HEAD_DIM_A ** -0.5
SCALE_B = Q_HEAD_B ** -0.5
LOG2_E = 1.4426950408889634
QSCALE_A = SCALE_A * LOG2_E
QSCALE_B = SCALE_B * LOG2_E
ALIBI_SLOPES = tuple(2.0 ** (-8.0 * (h + 1) / N_HEADS_A) for h in range(N_HEADS_A))

IN_GATES = 0
IN_QA = 2 * D_MODEL
IN_KA = IN_QA + WIDTH_A
IN_VA = IN_KA + N_KV_A * HEAD_DIM_A
IN_CQ = IN_VA + N_KV_A * HEAD_DIM_A
IN_CKV = IN_CQ + Q_LORA
IN_KR = IN_CKV + KV_LORA
D_IN = IN_KR + QK_ROPE
SM_KA = 0
SM_KR = SM_KA + 2 * LANES
SM_KRS = SM_KR + LANES
SM_ROWS = SM_KRS + LANES

PROJ_TM = 1024
PROJ_SUB = 512
SWA_TQ = 2048
SWA_LEAD = 4
MLA_BQ = 512
MLA_CHUNK = 512
MLA_HEADS = 4
MLA_LEAD = 1
MERGE_TM = 512
FF_CHUNK = 1024
V7X_VMEM_BYTES = 64 * 1024 * 1024
VMEM_LIMIT = V7X_VMEM_BYTES - 4 * 1024 * 1024


def _rms(v):
    return v * lax.rsqrt(jnp.mean(v * v, axis=-1, keepdims=True) + EPS)


def _dot(a, b):
    return jnp.dot(a, b, preferred_element_type=jnp.float32)


def _dot_nt(a, b):
    return lax.dot_general(a, b, (((1,), (1,)), ((), ())), preferred_element_type=jnp.float32)


def _proj_kernel(x_ref, pos_ref, gpre_ref, wqa_ref, wcq_ref, wckv_ref, wsm_ref, wvat_ref, gq_ref,
                 wqm_ref, wqs_ref, gkv_ref, wkk_ref, wkvt_ref, freq_ref,
                 qa_ref, ka_ref, vat_ref, qb_ref, kb_ref, vt_ref):
    bf16 = jnp.bfloat16
    sub = PROJ_SUB

    def first_stage(n):
        rows = slice(n * sub, (n + 1) * sub)
        hb = (_rms(x_ref[rows, :]) * gpre_ref[...]).astype(bf16)
        qa_ref[rows, :] = (_dot_nt(hb, wqa_ref[...]) * QSCALE_A).astype(bf16)
        small = _dot_nt(hb, wsm_ref[...])
        ka_ref[rows, :] = small[:, SM_KA:SM_KR].astype(bf16)
        va_t = _dot_nt(wvat_ref[...], hb)
        varow = lax.broadcasted_iota(jnp.int32, va_t.shape, 0)
        vat_ref[0, :, rows] = jnp.where(varow % VT_ROWS == HEAD_DIM_A, 1.0, va_t).astype(bf16)
        cq = _dot_nt(hb, wcq_ref[...])
        ckv = _dot_nt(hb, wckv_ref[...])
        pos = pos_ref[0, :, rows].astype(jnp.float32)
        ang = freq_ref[...] * pos
        cos_t, sin_t = jnp.cos(ang), jnp.sin(ang)
        one = jnp.ones((QK_NOPE, sub), jnp.float32)
        zero = jnp.zeros((QK_NOPE, sub), jnp.float32)
        pad = LANES - Q_HEAD_B
        cos = jnp.concatenate([one, cos_t, cos_t, one[:pad]], axis=0).T
        sin = jnp.concatenate([zero, -sin_t, sin_t, zero[:pad]], axis=0).T
        k_rot = small[:, SM_KR:SM_KRS] * cos + small[:, SM_KRS:SM_ROWS] * sin
        return cq, ckv, cos, sin, k_rot

    def second_stage(n, cq, ckv, cos, sin, k_rot):
        rows = slice(n * sub, (n + 1) * sub)
        cqn = (_rms(cq) * gq_ref[...]).astype(bf16)
        q_main = _dot(cqn, wqm_ref[...])
        q_swap = _dot(cqn, wqs_ref[...])
        ckvn = (_rms(ckv) * gkv_ref[...]).astype(bf16)
        k_nope = _dot(ckvn, wkk_ref[...])
        v_t = _dot_nt(wkvt_ref[...], ckvn)
        tail_row = lax.broadcasted_iota(jnp.int32, (VT_PAD, sub), 0)
        tail = jnp.where(tail_row == 0, 1.0, 0.0).astype(bf16)
        for h in range(N_HEADS_B):
            vt_ref[0, n, h * VT_ROWS:h * VT_ROWS + V_DIM_B, :] = (
                v_t[h * V_DIM_B:(h + 1) * V_DIM_B].astype(bf16))
            vt_ref[0, n, h * VT_ROWS + V_DIM_B:(h + 1) * VT_ROWS, :] = tail
        for h in range(N_HEADS_B):
            sl = slice(h * LANES, (h + 1) * LANES)
            qb_ref[rows, sl] = ((q_main[:, sl] * cos + q_swap[:, sl] * sin) * QSCALE_B).astype(bf16)
            kb_ref[rows, sl] = (k_nope[:, sl] + k_rot).astype(bf16)

    n_sub = x_ref.shape[0] // sub
    staged = [first_stage(n) for n in range(n_sub)]
    for n in range(n_sub):
        second_stage(n, *staged[n])


def _swa_kernel(sink_ref, q_ref, kc_ref, kp_ref, vtc_ref, vtp_ref, posc_ref, posp_ref, o_ref):
    bf16 = jnp.bfloat16
    i = pl.program_id(1)
    krow = lax.broadcasted_iota(jnp.int32, (2 * BLOCK, BLOCK), 0)
    qcol = lax.broadcasted_iota(jnp.int32, (2 * BLOCK, BLOCK), 1)
    ahead = krow - qcol
    band = (ahead > 0) & (ahead <= WINDOW)
    lane = lax.broadcasted_iota(jnp.int32, (BLOCK, LANES), 1)
    low_half = lane < HALF
    pos_inf = jnp.float32(jnp.inf)

    n_pairs = N_HEADS_A // 2
    chains = [(blk, pair) for blk in range(SWA_TQ // BLOCK) for pair in range(n_pairs)]
    bands = {}

    def band_of(blk):
        if blk not in bands:
            r0 = blk * BLOCK
            if blk == 0:
                kband = jnp.concatenate([kp_ref[0], kc_ref[0, 0:BLOCK, :]], axis=0)
                vtband = jnp.concatenate([vtp_ref[0], vtc_ref[0, :, 0:BLOCK]], axis=1)
                kpos = jnp.concatenate([posp_ref[0], posc_ref[0, :, 0:BLOCK]], axis=1)
                mask = band & ((krow >= BLOCK) | (i > 0))
            else:
                kband = kc_ref[0, r0 - BLOCK:r0 + BLOCK, :]
                vtband = vtc_ref[0, :, r0 - BLOCK:r0 + BLOCK]
                kpos = posc_ref[0, :, r0 - BLOCK:r0 + BLOCK]
                mask = band
            qpos = posc_ref[0, :, r0:r0 + BLOCK]
            kpos_col = jnp.broadcast_to(kpos, (BLOCK, 2 * BLOCK)).T
            dist = jnp.abs(kpos_col - qpos).astype(jnp.float32) * LOG2_E
            dist = jnp.where(mask, dist, pos_inf)
            bands[blk] = (kband, vtband, dist)
        return bands[blk]

    def scores(blk, pair):
        kv = (2 * pair) // (N_HEADS_A // N_KV_A)
        r0 = blk * BLOCK
        qp = q_ref[0, r0:r0 + BLOCK, pair * LANES:(pair + 1) * LANES]
        zero = jnp.zeros_like(qp)
        q2 = jnp.concatenate([jnp.where(low_half, qp, zero), jnp.where(low_half, zero, qp)],
                             axis=0)
        kx = band_of(blk)[0][:, kv * LANES:(kv + 1) * LANES]
        return _dot_nt(kx, q2)

    def finish(blk, pair, s2):
        kv = (2 * pair) // (N_HEADS_A // N_KV_A)
        r0 = blk * BLOCK
        _, vtband, dist = band_of(blk)
        vt = vtband[kv * VT_ROWS:(kv + 1) * VT_ROWS, :]
        ps, ms = [], []
        for e in range(2):
            h = 2 * pair + e
            s = s2[:, e * BLOCK:(e + 1) * BLOCK] - ALIBI_SLOPES[h] * dist
            m = jnp.maximum(jnp.max(s, axis=0, keepdims=True), sink_ref[h] * LOG2_E)
            ps.append(jnp.exp2(s - m).astype(bf16))
            ms.append(m)
        o2 = _dot(vt, jnp.concatenate(ps, axis=1))
        outs = []
        for e in range(2):
            h = 2 * pair + e
            o_t = o2[:, e * BLOCK:(e + 1) * BLOCK]
            denom = (o_t[HEAD_DIM_A:HEAD_DIM_A + 1]
                     + jnp.exp2(sink_ref[h] * LOG2_E - ms[e]))
            outs.append(o_t[0:HEAD_DIM_A] / denom)
        o_ref[0, r0:r0 + BLOCK, pair * LANES:(pair + 1) * LANES] = (
            jnp.concatenate(outs, axis=0).T.astype(bf16))

    pending = [scores(*c) for c in chains[:SWA_LEAD]]
    for n, c in enumerate(chains):
        s2 = pending.pop(0)
        if n + SWA_LEAD < len(chains):
            pending.append(scores(*chains[n + SWA_LEAD]))
        finish(*c, s2)


def _mla_kernel(q_ref, k_ref, vt_ref, o_ref, s_ref, bmax_ref, acc_ref, m_ref):
    bf16 = jnp.bfloat16
    bq = MLA_BQ
    ch = MLA_CHUNK
    qi = pl.program_id(2)
    krow = lax.broadcasted_iota(jnp.int32, (bq, ch), 0)
    qcol = lax.broadcasted_iota(jnp.int32, (bq, ch), 1)
    neg_inf = jnp.float32(-jnp.inf)
    m_ref[...] = jnp.full(m_ref.shape, neg_inf, jnp.float32)
    acc_ref[...] = jnp.zeros(acc_ref.shape, jnp.float32)

    def scores_head(buf, kb, e):
        start = pl.multiple_of(kb * bq, bq)
        sl = slice(e * LANES, (e + 1) * LANES)
        s = _dot_nt(k_ref[0, pl.ds(start, bq), sl], q_ref[0, :, sl])
        s_ref[buf, e] = s
        bmax_ref[buf, e] = jnp.max(s, axis=0, keepdims=True)

    def scores_into(buf, kb):
        for e in range(MLA_HEADS):
            scores_head(buf, kb, e)

    def softmax_head(buf, e, masked):
        out = []
        for c in range(bq // ch):
            cs = slice(c * ch, (c + 1) * ch)
            s = s_ref[buf, e, :, cs]
            if masked:
                s = jnp.where(krow <= qcol + c * ch, s, neg_inf)
                blockmax = jnp.max(s, axis=0, keepdims=True)
            else:
                blockmax = bmax_ref[buf, e, :, cs]
            m_prev = m_ref[e, :, cs]
            m_new = jnp.maximum(m_prev, blockmax)
            m_ref[e, :, cs] = m_new
            out.append((jnp.exp2(s - m_new).astype(bf16), jnp.exp2(m_prev - m_new)))
        return out

    def pv_head(kb, e, weights):
        vt = vt_ref[0, kb, e * VT_ROWS:(e + 1) * VT_ROWS, :]
        for c, (p, alpha) in enumerate(weights):
            cs = slice(c * ch, (c + 1) * ch)
            acc_ref[e, :, cs] = alpha * acc_ref[e, :, cs] + _dot(vt, p)

    def softmax_pv(buf, kb, masked):
        for e in range(MLA_HEADS):
            pv_head(kb, e, softmax_head(buf, e, masked))

    def pipelined(first_kb, n_blocks):
        blocks = [(j % 2, first_kb + j) for j in range(n_blocks)]
        units = [(buf, kb, e) for buf, kb in blocks for e in range(MLA_HEADS)]
        prods = [(1 - buf, kb + 1, e) for buf, kb in blocks for e in range(MLA_HEADS)]
        for n in range(min(MLA_LEAD, len(prods))):
            scores_head(*prods[n])
        for n, (buf, kb, e) in enumerate(units):
            pv_head(kb, e, softmax_head(buf, e, False))
            if n + MLA_LEAD < len(prods):
                scores_head(*prods[n + MLA_LEAD])

    def quad_body(t, carry):
        pipelined(4 * t, 4)
        return carry

    scores_into(0, 0)
    lax.fori_loop(0, qi // 4, quad_body, 0)

    @pl.when(qi % 4 >= 2)
    def _():
        pipelined((qi // 4) * 4, 2)

    @pl.when(qi % 2 == 1)
    def _():
        pipelined(qi - 1, 1)
        softmax_pv(1, qi, True)

    @pl.when(qi % 2 == 0)
    def _():
        softmax_pv(0, qi, True)

    outs = []
    for e in range(MLA_HEADS):
        acc = acc_ref[e]
        outs.append(acc[0:V_DIM_B] / acc[V_DIM_B:V_DIM_B + 1])
    o_ref[0] = jnp.concatenate(outs, axis=0).T.astype(bf16)


def _merge_kernel(x_ref, oa_ref, ob_ref, gpre_ref, wg_ref, woa_ref, wob_ref, wout_ref, gpost_ref,
                  gpre2_ref, wup_ref, wdn_ref, gpost2_ref, o_ref):
    bf16 = jnp.bfloat16
    x = x_ref[...]
    hm = x.shape[0] // 2
    halves = (slice(0, hm), slice(hm, 2 * hm))
    hb = [(_rms(x[r]) * gpre_ref[...]).astype(bf16) for r in halves]
    gate_a = jnp.concatenate([_dot_nt(h, wg_ref[0:D_MODEL, :]) for h in hb], axis=0)
    hb = jnp.concatenate(hb, axis=0)
    gate_a = jax.nn.sigmoid(gate_a)
    gate_b = jax.nn.sigmoid(_dot_nt(hb, wg_ref[D_MODEL:2 * D_MODEL, :]))
    merged = gate_a * _dot(oa_ref[...], woa_ref[...]) + gate_b * _dot(ob_ref[...], wob_ref[...])
    mb = merged.astype(bf16)

    n_chunks = D_FF // FF_CHUNK
    chunk = lambda c: slice(c * FF_CHUNK, (c + 1) * FF_CHUNK)

    def w_up(c):
        return wup_ref[:, chunk(c)].astype(bf16)

    def w_down(c):
        return wdn_ref[chunk(c), :].astype(bf16)

    def sq_relu(v):
        v = jnp.maximum(v, 0.0)
        return (v * v).astype(bf16)

    x1, h2, up0 = [], [], []
    for r in halves:
        y = _dot(mb[r], wout_ref[...])
        x1.append(x[r] + _rms(y) * gpost_ref[...])
        h2.append((_rms(x1[-1]) * gpre2_ref[...]).astype(bf16))
    w_first = w_up(0)
    for i in range(2):
        up0.append(sq_relu(_dot(h2[i], w_first)))
    h2 = jnp.concatenate(h2, axis=0)
    y2 = _dot(jnp.concatenate(up0, axis=0), w_down(0))
    for c in range(1, n_chunks - 1):
        y2 = y2 + _dot(sq_relu(_dot(h2, w_up(c))), w_down(c))
    last = sq_relu(_dot(h2, w_up(n_chunks - 1)))
    w_last = w_down(n_chunks - 1)
    for i, r in enumerate(halves):
        y2_half = y2[r] + _dot(last[r], w_last)
        o_ref[r, :] = x1[i] + _rms(y2_half) * gpost2_ref[...]


def _const_spec(shape):
    return pl.BlockSpec(shape, lambda *_: (0,) * len(shape), pipeline_mode=pl.Buffered(1))


def kernel(x, positions, pre_norm_mix, w_in, q_a_norm, w_q_b, kv_a_norm, w_kv_b, sinks, w_o_a,
           w_o_b, w_out, post_norm_mix, pre_norm_mlp, w_up, w_down, post_norm_mlp):
    f32, bf16 = jnp.float32, jnp.bfloat16
    B, S, D = x.shape
    T = B * S
    depth = w_in.shape[0]
    for l in range(depth):
        assert w_in.shape[2] == D_IN
        wit = jnp.swapaxes(w_in[l], 0, 1).astype(bf16)
        ka_t = wit[IN_KA:IN_VA]
        va_t = wit[IN_VA:IN_CQ]
        kr_t = wit[IN_KR:D_IN]
        hr = QK_ROPE // 2
        hd = HEAD_DIM_A
        z = lambda n: jnp.zeros((n, D_MODEL), bf16)
        w_small_t = jnp.concatenate([
            ka_t[:hd], ka_t[:hd], ka_t[hd:], ka_t[hd:],
            z(QK_NOPE), kr_t, z(LANES - Q_HEAD_B),
            z(QK_NOPE), kr_t[hr:], kr_t[:hr], z(LANES - Q_HEAD_B)], axis=0)
        w_va_t = jnp.concatenate([va_t[:hd], z(VT_PAD), va_t[hd:], z(VT_PAD)], axis=0)

        wq = w_q_b[l].reshape(Q_LORA, N_HEADS_B, Q_HEAD_B)
        q_nope, q_rope = wq[..., :QK_NOPE], wq[..., QK_NOPE:]
        zq = lambda n: jnp.zeros((Q_LORA, N_HEADS_B, n), f32)
        wq_main = jnp.concatenate([q_nope, q_rope, zq(LANES - Q_HEAD_B)], -1)
        wq_swap = jnp.concatenate([zq(QK_NOPE), q_rope[..., hr:], q_rope[..., :hr],
                                   zq(LANES - Q_HEAD_B)], -1)
        wq_main = wq_main.reshape(Q_LORA, N_HEADS_B * LANES).astype(bf16)
        wq_swap = wq_swap.reshape(Q_LORA, N_HEADS_B * LANES).astype(bf16)

        wkv = w_kv_b[l].reshape(KV_LORA, N_HEADS_B, KV_HEAD_B)
        kv_k, kv_v = wkv[..., :QK_NOPE], wkv[..., QK_NOPE:]
        zk = jnp.zeros((KV_LORA, N_HEADS_B, HALF), f32)
        wkv_k = jnp.concatenate([kv_k, zk], -1).reshape(KV_LORA, N_HEADS_B * LANES).astype(bf16)
        wkv_vt = kv_v.reshape(KV_LORA, WIDTH_B).T.astype(bf16)

        freq_col = (ROPE_THETA ** (-jnp.arange(0, QK_ROPE, 2, dtype=f32) / QK_ROPE))[:, None]

        row = lambda g: g.reshape(1, -1).astype(f32)
        x2 = x.reshape(T, D)

        tm = PROJ_TM
        assert PROJ_SUB == MLA_BQ and tm % PROJ_SUB == 0 and S % tm == 0
        nkb = S // MLA_BQ
        n_sub = tm // PROJ_SUB
        npb = S // tm
        tok = lambda w: pl.BlockSpec((tm, w), lambda i: (i, 0))
        w_rows = lambda n, start: pl.BlockSpec((n, D), lambda i: (start // n, 0),
                                               pipeline_mode=pl.Buffered(1))
        assert IN_QA % WIDTH_A == 0 and IN_CQ % Q_LORA == 0 and IN_CKV % KV_LORA == 0
        qa, ka_x, va_t3, qb, kb, vt = pl.pallas_call(
            _proj_kernel,
            grid=(T // tm,),
            in_specs=[tok(D), pl.BlockSpec((1, 1, tm), lambda i: (i, 0, 0)), _const_spec((1, D)),
                      w_rows(WIDTH_A, IN_QA), w_rows(Q_LORA, IN_CQ), w_rows(KV_LORA, IN_CKV),
                      _const_spec((SM_ROWS, D)), _const_spec((N_KV_A * VT_ROWS, D)),
                      _const_spec((1, Q_LORA)), _const_spec((Q_LORA, N_HEADS_B * LANES)),
                      _const_spec((Q_LORA, N_HEADS_B * LANES)), _const_spec((1, KV_LORA)),
                      _const_spec((KV_LORA, N_HEADS_B * LANES)),
                      _const_spec((WIDTH_B, KV_LORA)),
                      _const_spec((QK_ROPE // 2, 1))],
            out_specs=[tok(WIDTH_A), tok(2 * LANES),
                       pl.BlockSpec((1, N_KV_A * VT_ROWS, tm), lambda i: (i // npb, 0, i % npb)),
                       tok(N_HEADS_B * LANES), tok(N_HEADS_B * LANES),
                       pl.BlockSpec((1, n_sub, N_HEADS_B * VT_ROWS, PROJ_SUB),
                                    lambda i: (i // npb, i % npb, 0, 0))],
            out_shape=[jax.ShapeDtypeStruct((T, WIDTH_A), bf16),
                       jax.ShapeDtypeStruct((T, 2 * LANES), bf16),
                       jax.ShapeDtypeStruct((B, N_KV_A * VT_ROWS, S), bf16),
                       jax.ShapeDtypeStruct((T, N_HEADS_B * LANES), bf16),
                       jax.ShapeDtypeStruct((T, N_HEADS_B * LANES), bf16),
                       jax.ShapeDtypeStruct((B, nkb, N_HEADS_B * VT_ROWS, PROJ_SUB), bf16)],
            compiler_params=pltpu.CompilerParams(dimension_semantics=("parallel",),
                                                 vmem_limit_bytes=VMEM_LIMIT),
            name="proj",
        )(x2, positions.reshape(T // tm, 1, tm), row(pre_norm_mix[l]), wit, wit, wit, w_small_t,
          w_va_t, row(q_a_norm[l]), wq_main, wq_swap, row(kv_a_norm[l]), wkv_k, wkv_vt,
          freq_col)

        tq = SWA_TQ
        nb_per = tq // BLOCK
        qa3 = qa.reshape(B, S, WIDTH_A)
        ka3 = ka_x.reshape(B, S, 2 * LANES)
        pos_r3 = positions.reshape(B, 1, S)
        cur = lambda w: pl.BlockSpec((1, tq, w), lambda b, i: (b, i, 0))
        prev_blk = lambda b, i: jnp.maximum(i * nb_per - 1, 0)
        out_a = pl.pallas_call(
            _swa_kernel,
            grid=(B, S // tq),
            in_specs=[pl.BlockSpec(memory_space=pltpu.SMEM),
                      cur(WIDTH_A), cur(2 * LANES),
                      pl.BlockSpec((1, BLOCK, 2 * LANES), lambda b, i: (b, prev_blk(b, i), 0)),
                      pl.BlockSpec((1, N_KV_A * VT_ROWS, tq), lambda b, i: (b, 0, i)),
                      pl.BlockSpec((1, N_KV_A * VT_ROWS, BLOCK), lambda b, i: (b, 0, prev_blk(b, i))),
                      pl.BlockSpec((1, 1, tq), lambda b, i: (b, 0, i)),
                      pl.BlockSpec((1, 1, BLOCK), lambda b, i: (b, 0, prev_blk(b, i)))],
            out_specs=cur(WIDTH_A),
            out_shape=jax.ShapeDtypeStruct((B, S, WIDTH_A), bf16),
            compiler_params=pltpu.CompilerParams(dimension_semantics=("parallel", "parallel"),
                                                 vmem_limit_bytes=VMEM_LIMIT),
            name="swa",
        )(sinks[l].astype(f32), qa3, ka3, ka3, va_t3, va_t3, pos_r3, pos_r3)

        bq, nh = MLA_BQ, MLA_HEADS
        qb3 = qb.reshape(B, S, N_HEADS_B * LANES)
        kb3 = kb.reshape(B, S, N_HEADS_B * LANES)
        out_b = pl.pallas_call(
            _mla_kernel,
            grid=(B, N_HEADS_B // nh, S // bq),
            in_specs=[pl.BlockSpec((1, bq, nh * LANES), lambda b, j, i: (b, i, j)),
                      pl.BlockSpec((1, S, nh * LANES), lambda b, j, i: (b, 0, j)),
                      pl.BlockSpec((1, nkb, nh * VT_ROWS, bq), lambda b, j, i: (b, 0, j, 0))],
            out_specs=pl.BlockSpec((1, bq, nh * V_DIM_B), lambda b, j, i: (b, i, j)),
            out_shape=jax.ShapeDtypeStruct((B, S, WIDTH_B), bf16),
            scratch_shapes=[pltpu.VMEM((2, nh, bq, bq), f32),
                            pltpu.VMEM((2, nh, 1, bq), f32),
                            pltpu.VMEM((nh, VT_ROWS, bq), f32), pltpu.VMEM((nh, 1, bq), f32)],
            compiler_params=pltpu.CompilerParams(
                dimension_semantics=("parallel", "parallel", "parallel"),
                vmem_limit_bytes=VMEM_LIMIT),
            name="mla",
        )(qb3, kb3, vt)

        tm = MERGE_TM
        tok = lambda w: pl.BlockSpec((tm, w), lambda i: (i, 0))
        x2 = pl.pallas_call(
            _merge_kernel,
            grid=(T // tm,),
            in_specs=[tok(D), tok(WIDTH_A), tok(WIDTH_B), _const_spec((1, D)),
                      w_rows(2 * D, IN_GATES), _const_spec((WIDTH_A, D)), _const_spec((WIDTH_B, D)),
                      _const_spec((D, D)), _const_spec((1, D)), _const_spec((1, D)),
                      _const_spec((D, D_FF)), _const_spec((D_FF, D)), _const_spec((1, D))],
            out_specs=tok(D),
            out_shape=jax.ShapeDtypeStruct((T, D), f32),
            compiler_params=pltpu.CompilerParams(dimension_semantics=("parallel",),
                                                 vmem_limit_bytes=VMEM_LIMIT),
            name="merge_mlp",
        )(x2, out_a.reshape(T, WIDTH_A), out_b.reshape(T, WIDTH_B), row(pre_norm_mix[l]), wit,
          w_o_a[l].astype(bf16), w_o_b[l].astype(bf16), w_out[l].astype(bf16),
          row(post_norm_mix[l]), row(pre_norm_mlp[l]), w_up[l],
          w_down[l], row(post_norm_mlp[l]))
        x = x2.reshape(B, S, D)
    return x
```
